```python
import math
import jax, jax.numpy as jnp
from jax import lax
import numpy as np


D_MODEL = 1024
BATCH = 2
SEQ = 8192
DEPTH = 1

HEAD_DIM = 64
DSA_HEADS = 8
DSA_WIDTH = DSA_HEADS * HEAD_DIM
IDX_HEADS = 8
IDX_DIM = 32
TOPK_MAX = 256
DIFF_HEADS = 4
DIFF_DIM = 64
DIFF_VDIM = 2 * DIFF_DIM
DIFF_QK_WIDTH = DIFF_HEADS * 2 * DIFF_DIM
DIFF_WIDTH = DIFF_HEADS * DIFF_VDIM
DIFF_SUBLN_EPS = 1e-5
N_BRANCH = 2
ROPE_THETA = 500000.0
ROT_FRACTION = 4
ROT_DIM_HEAD = HEAD_DIM // ROT_FRACTION
ROT_DIM_IDX = IDX_DIM // ROT_FRACTION
Q_BLOCK = 128
FFN_MULT = 256
D_FF = -(-8 * D_MODEL // (3 * FFN_MULT)) * FFN_MULT
NORM_EPS = 1e-6

IN_SPLITS = (
    DSA_WIDTH,
    DSA_WIDTH,
    DSA_WIDTH,
    IDX_HEADS * IDX_DIM,
    IDX_DIM,
    IDX_HEADS,
    DIFF_QK_WIDTH,
    DIFF_QK_WIDTH,
    DIFF_WIDTH,
    N_BRANCH * D_MODEL,
)
D_IN = sum(IN_SPLITS)

kernel_name = 'hybrid_dsa_diffattn_gated_block'


def rmsnorm(x, g, eps=NORM_EPS):
    xf = x.astype(jnp.float32)
    y = xf * lax.rsqrt(jnp.mean(xf * xf, axis=-1, keepdims=True) + eps)
    return (y * g.astype(jnp.float32)).astype(x.dtype)


def layernorm(x, g, b, eps=NORM_EPS):
    xf = x.astype(jnp.float32)
    mu = jnp.mean(xf, axis=-1, keepdims=True)
    xc = xf - mu
    y = xc * lax.rsqrt(jnp.mean(xc * xc, axis=-1, keepdims=True) + eps)
    return (y * g.astype(jnp.float32) + b.astype(jnp.float32)).astype(x.dtype)


def rope_tables(positions, rot_dim):
    inv_freq = jnp.power(jnp.float32(ROPE_THETA), -jnp.arange(0, rot_dim, 2, dtype=jnp.float32) / rot_dim)
    ang = positions.astype(jnp.float32)[..., None] * inv_freq
    return jnp.cos(ang), jnp.sin(ang)


def apply_partial_rope(x, cos, sin):
    half = cos.shape[-1]
    x1 = x[..., :half]
    x2 = x[..., half:2 * half]
    xp = x[..., 2 * half:]
    c = cos[:, :, None, :].astype(x.dtype)
    s = sin[:, :, None, :].astype(x.dtype)
    return jnp.concatenate([x1 * c - x2 * s, x2 * c + x1 * s, xp], axis=-1)


def split_columns(proj):
    outs = []
    start = 0
    for size in IN_SPLITS:
        outs.append(proj[..., start:start + size])
        start += size
    return outs


def dsa_sparse_attention(q, k, v, q_idx, k_idx, w_idx, n_top):
    B, S = q.shape[0], q.shape[1]
    att_scale = HEAD_DIM ** -0.5
    w_idx = w_idx * (IDX_HEADS ** -0.5 * IDX_DIM ** -0.5)
    s_pos = jnp.arange(S)

    def block(i):
        q0 = i * Q_BLOCK
        qb = lax.dynamic_slice_in_dim(q, q0, Q_BLOCK, axis=1)
        qib = lax.dynamic_slice_in_dim(q_idx, q0, Q_BLOCK, axis=1)
        wib = lax.dynamic_slice_in_dim(w_idx, q0, Q_BLOCK, axis=1)
        t_pos = q0 + jnp.arange(Q_BLOCK)
        causal = s_pos[None, :] <= t_pos[:, None]
        logits = jnp.einsum('bqhd,bsd->bqhs', qib, k_idx)
        score = jnp.einsum('bqhs,bqh->bqs', jax.nn.relu(logits), wib).astype(jnp.float32)
        score = jnp.where(causal[None], score, -jnp.inf)
        _, sel = lax.top_k(score, n_top)
        valid = sel <= t_pos[None, :, None]
        kg = jax.vmap(lambda kb, ib: kb[ib])(k, sel)
        vg = jax.vmap(lambda vb, ib: vb[ib])(v, sel)
        sc = jnp.einsum('bqhd,bqkhd->bhqk', qb, kg).astype(jnp.float32) * att_scale
        sc = jnp.where(valid[:, None], sc, -jnp.inf)
        p = jax.nn.softmax(sc, axis=-1).astype(v.dtype)
        return jnp.einsum('bhqk,bqkhd->bqhd', p, vg)

    out = lax.map(block, jnp.arange(S // Q_BLOCK))
    return jnp.moveaxis(out, 0, 1).reshape(B, S, DSA_WIDTH)


def differential_attention(q, k, v, lam):
    B, S = q.shape[0], q.shape[1]
    scale = DIFF_DIM ** -0.5
    s_pos = jnp.arange(S)

    def block(i):
        q0 = i * Q_BLOCK
        qb = lax.dynamic_slice_in_dim(q, q0, Q_BLOCK, axis=1)
        t_pos = q0 + jnp.arange(Q_BLOCK)
        causal = s_pos[None, :] <= t_pos[:, None]
        sc = jnp.einsum('bqhmd,bshmd->bhmqs', qb, k).astype(jnp.float32) * scale
        sc = jnp.where(causal, sc, -jnp.inf)
        a = jax.nn.softmax(sc, axis=-1)
        a = a[:, :, 0] - lam * a[:, :, 1]
        return jnp.einsum('bhqs,bshe->bqhe', a.astype(v.dtype), v)

    out = lax.map(block, jnp.arange(S // Q_BLOCK))
    return jnp.moveaxis(out, 0, 1).reshape(B, S, DIFF_HEADS, DIFF_VDIM)


def setup_inputs(seed: int = 0) -> dict:
    key = jax.random.key(seed)
    ks = jax.random.split(key, 20)
    f32 = jnp.float32

    def normal(k, shape, std):
        return jax.random.normal(k, shape, dtype=f32) * std

    def gain(k, shape):
        return 1.0 + normal(k, shape, 0.02)

    return {
        'x': normal(ks[0], (BATCH, SEQ, D_MODEL), 1.0),
        'positions': jnp.broadcast_to(jnp.arange(SEQ, dtype=jnp.int32), (BATCH, SEQ)),
        'norm_mix_g': gain(ks[1], (DEPTH, D_MODEL)),
        'w_in': normal(ks[2], (DEPTH, D_MODEL, D_IN), D_MODEL ** -0.5),
        'idx_k_norm_g': gain(ks[3], (DEPTH, IDX_DIM)),
        'idx_k_norm_b': normal(ks[4], (DEPTH, IDX_DIM), 0.02),
        'diff_lambda_q1': normal(ks[5], (DEPTH, DIFF_DIM), 0.1),
        'diff_lambda_k1': normal(ks[6], (DEPTH, DIFF_DIM), 0.1),
        'diff_lambda_q2': normal(ks[7], (DEPTH, DIFF_DIM), 0.1),
        'diff_lambda_k2': normal(ks[8], (DEPTH, DIFF_DIM), 0.1),
        'diff_subln_g': gain(ks[9], (DEPTH, DIFF_VDIM)),
        'gate_b': normal(ks[10], (DEPTH, N_BRANCH * D_MODEL), 0.02),
        'w_branch_dsa': normal(ks[11], (DEPTH, DSA_WIDTH, D_MODEL), DSA_WIDTH ** -0.5),
        'w_branch_diff': normal(ks[12], (DEPTH, DIFF_WIDTH, D_MODEL), DIFF_WIDTH ** -0.5),
        'w_out': normal(ks[13], (DEPTH, D_MODEL, D_MODEL), D_MODEL ** -0.5),
        'norm_ffn_g': gain(ks[14], (DEPTH, D_MODEL)),
        'w_ffn_in': normal(ks[15], (DEPTH, D_MODEL, 2 * D_FF), D_MODEL ** -0.5),
        'w_ffn_out': normal(ks[16], (DEPTH, D_FF, D_MODEL), D_FF ** -0.5),
        'norm_final_g': gain(ks[17], (D_MODEL,)),
    }


def reference(x, positions, norm_mix_g, w_in, idx_k_norm_g, idx_k_norm_b,
              diff_lambda_q1, diff_lambda_k1, diff_lambda_q2, diff_lambda_k2,
              diff_subln_g, gate_b, w_branch_dsa, w_branch_diff, w_out,
              norm_ffn_g, w_ffn_in, w_ffn_out, norm_final_g):
    B, S = x.shape[0], x.shape[1]
    n_top = min(TOPK_MAX, S // 4)
    cos_h, sin_h = rope_tables(positions, ROT_DIM_HEAD)
    cos_i, sin_i = rope_tables(positions, ROT_DIM_IDX)

    for l in range(DEPTH):
        lam_init = 0.8 - 0.6 * math.exp(-0.3 * l)

        h = rmsnorm(x, norm_mix_g[l])
        proj = h @ w_in[l]
        (q_a, k_a, v_a, q_i, k_i, w_i, q_b, k_b, v_b, g) = split_columns(proj)

        q_a = apply_partial_rope(q_a.reshape(B, S, DSA_HEADS, HEAD_DIM), cos_h, sin_h)
        k_a = apply_partial_rope(k_a.reshape(B, S, DSA_HEADS, HEAD_DIM), cos_h, sin_h)
        v_a = v_a.reshape(B, S, DSA_HEADS, HEAD_DIM)
        q_i = apply_partial_rope(q_i.reshape(B, S, IDX_HEADS, IDX_DIM), cos_i, sin_i)
        k_i = layernorm(k_i, idx_k_norm_g[l], idx_k_norm_b[l])
        k_i = apply_partial_rope(k_i[:, :, None, :], cos_i, sin_i)[:, :, 0, :]
        o_a = dsa_sparse_attention(q_a, k_a, v_a, q_i, k_i, w_i, n_top)

        q_b = apply_partial_rope(q_b.reshape(B, S, 2 * DIFF_HEADS, DIFF_DIM), cos_h, sin_h)
        k_b = apply_partial_rope(k_b.reshape(B, S, 2 * DIFF_HEADS, DIFF_DIM), cos_h, sin_h)
        q_b = q_b.reshape(B, S, DIFF_HEADS, 2, DIFF_DIM)
        k_b = k_b.reshape(B, S, DIFF_HEADS, 2, DIFF_DIM)
        v_b = v_b.reshape(B, S, DIFF_HEADS, DIFF_VDIM)
        lam = (jnp.exp(jnp.sum(diff_lambda_q1[l] * diff_lambda_k1[l]).astype(jnp.float32))
               - jnp.exp(jnp.sum(diff_lambda_q2[l] * diff_lambda_k2[l]).astype(jnp.float32))
               + lam_init)
        o_b = differential_attention(q_b, k_b, v_b, lam)
        o_b = rmsnorm(o_b, diff_subln_g[l], eps=DIFF_SUBLN_EPS) * (1.0 - lam_init)
        o_b = o_b.reshape(B, S, DIFF_WIDTH)

        gates = jax.nn.sigmoid(g + gate_b[l]).reshape(B, S, N_BRANCH, D_MODEL)
        merged = (gates[:, :, 0] * (o_a @ w_branch_dsa[l])
                  + gates[:, :, 1] * (o_b @ w_branch_diff[l]))
        x = x + merged @ w_out[l]

        h = rmsnorm(x, norm_ffn_g[l])
        gu = h @ w_ffn_in[l]
        x = x + (jax.nn.silu(gu[..., :D_FF]) * gu[..., D_FF:]) @ w_ffn_out[l]

    return rmsnorm(x, norm_final_g)
```

```python
import functools
import math

import jax
import jax.numpy as jnp
from jax import lax
from jax.experimental import pallas as pl
from jax.experimental.pallas import tpu as pltpu

F32 = jnp.float32
BF16 = jnp.bfloat16

D_MODEL = 1024
HEAD_DIM = 64
DSA_HEADS = 8
DSA_WIDTH = DSA_HEADS * HEAD_DIM
IDX_HEADS = 8
IDX_DIM = 32
IDX_QW = IDX_HEADS * IDX_DIM
TOPK_MAX = 256
DIFF_HEADS = 4
DIFF_DIM = 64
DIFF_VDIM = 2 * DIFF_DIM
DIFF_QK_WIDTH = DIFF_HEADS * 2 * DIFF_DIM
DIFF_WIDTH = DIFF_HEADS * DIFF_VDIM
DIFF_SUBLN_EPS = 1e-5
N_BRANCH = 2
ROPE_THETA = 500000.0
ROT_DIM_HEAD = HEAD_DIM // 4
ROT_DIM_IDX = IDX_DIM // 4
FFN_MULT = 256
D_FF = -(-8 * D_MODEL // (3 * FFN_MULT)) * FFN_MULT
NORM_EPS = 1e-6

LANES = 128
VMEM_LIMIT = 56 * 1024 * 1024
MASKED = -1e30
MAX_BISECT = 512

NT_DIMS = (((1,), (1,)), ((), ()))


def _cparams(sem):
    return pltpu.CompilerParams(dimension_semantics=sem, vmem_limit_bytes=VMEM_LIMIT)


def _rmsnorm(x, g, eps):
    return x * lax.rsqrt(jnp.mean(x * x, axis=-1, keepdims=True) + eps) * g


def _rope_block(y, cos, sin_lo, sin_hi, half):
    return (y * cos + pltpu.roll(y, half, 1) * sin_hi
            + pltpu.roll(y, LANES - half, 1) * sin_lo)


def _rope_tables(pos, inv_lane, period, half):
    ang = pos * inv_lane
    cos = jnp.cos(ang)
    sin = jnp.sin(ang)
    d = lax.broadcasted_iota(jnp.int32, (1, LANES), 1) % period
    sin_lo = jnp.where(d < half, -sin, 0.0)
    sin_hi = jnp.where((d >= half) & (d < 2 * half), sin, 0.0)
    return cos, sin_lo, sin_hi


def _inproj_kernel(x_ref, pos_ref, g_ref, wa_ref, wb_ref, wi_ref, invh_ref, invi_ref,
                   lng_ref, lnb_ref,
                   qa_ref, ka_ref, va_ref, qb_ref, kb_ref, vb_ref, qi_ref, ki_ref, wi_out_ref):
    h = _rmsnorm(x_ref[...], g_ref[...], NORM_EPS).astype(BF16)
    pos = pos_ref[...]
    cos_h, slo_h, shi_h = _rope_tables(pos, invh_ref[...], HEAD_DIM, ROT_DIM_HEAD // 2)
    cos_i, slo_i, shi_i = _rope_tables(pos, invi_ref[...], IDX_DIM, ROT_DIM_IDX // 2)
    qk_scale = HEAD_DIM ** -0.5

    def branch(w_ref, q_ref, k_ref, v_ref):
        for grp, (o_ref, rope, scale) in enumerate(
                ((q_ref, True, qk_scale), (k_ref, True, 1.0), (v_ref, False, 1.0))):
            y = jnp.dot(h, w_ref[:, grp * 512:(grp + 1) * 512], preferred_element_type=F32)
            for j in range(512 // LANES):
                blk = y[:, j * LANES:(j + 1) * LANES]
                if rope:
                    blk = _rope_block(blk, cos_h, slo_h, shi_h, ROT_DIM_HEAD // 2)
                if scale != 1.0:
                    blk = blk * scale
                o_ref[:, j * LANES:(j + 1) * LANES] = blk.astype(o_ref.dtype)

    branch(wa_ref, qa_ref, ka_ref, va_ref)
    branch(wb_ref, qb_ref, kb_ref, vb_ref)

    r = jnp.dot(h, wi_ref[...], preferred_element_type=F32)
    for j in range(IDX_QW // LANES):
        blk = _rope_block(r[:, j * LANES:(j + 1) * LANES], cos_i, slo_i, shi_i, ROT_DIM_IDX // 2)
        qi_ref[:, j * LANES:(j + 1) * LANES] = blk.astype(qi_ref.dtype)
    kw = r[:, IDX_QW:IDX_QW + LANES]
    lane = lax.broadcasted_iota(jnp.int32, (1, LANES), 1)
    is_k = lane < IDX_DIM
    mu = jnp.sum(jnp.where(is_k, kw, 0.0), axis=-1, keepdims=True) * (1.0 / IDX_DIM)
    xc = jnp.where(is_k, kw - mu, 0.0)
    var = jnp.sum(xc * xc, axis=-1, keepdims=True) * (1.0 / IDX_DIM)
    kn = xc * lax.rsqrt(var + NORM_EPS) * lng_ref[...] + lnb_ref[...]
    kn = _rope_block(kn, cos_i, slo_i, shi_i, ROT_DIM_IDX // 2)
    ki_ref[...] = kn[:, :IDX_DIM].astype(ki_ref.dtype)
    w_scale = IDX_HEADS ** -0.5 * IDX_DIM ** -0.5
    wi_out_ref[...] = kw[:, IDX_DIM:IDX_DIM + IDX_HEADS] * w_scale


def _in_projection(x2, posf, g, wa, wb, wi, invh, invi, lng, lnb, tm=256):
    M = x2.shape[0]
    row = lambda w: pl.BlockSpec((tm, w), lambda i: (i, 0))
    full = lambda a: pl.BlockSpec(a.shape, lambda i: (0,) * a.ndim)
    out_shape = ([jax.ShapeDtypeStruct((M, 512), BF16)] * 6
                 + [jax.ShapeDtypeStruct((M, IDX_QW), BF16),
                    jax.ShapeDtypeStruct((M, IDX_DIM), BF16),
                    jax.ShapeDtypeStruct((M, IDX_HEADS), F32)])
    return pl.pallas_call(
        _inproj_kernel,
        grid=(M // tm,),
        in_specs=[row(D_MODEL), row(1), full(g), full(wa), full(wb), full(wi),
                  full(invh), full(invi), full(lng), full(lnb)],
        out_specs=[row(512)] * 6 + [row(IDX_QW), row(IDX_DIM), row(IDX_HEADS)],
        out_shape=out_shape,
        compiler_params=_cparams(("parallel",)),
        name="in_projection",
    )(x2, posf, g, wa, wb, wi, invh, invi, lng, lnb)


def _indexer_kernel(q_ref, w_ref, k_ref, mask_ref, score_ref, wb_ref, *, tq, tk, seq, ktop):
    qi = pl.program_id(1)
    nck = ((qi + 1) * tq + tk - 1) // tk
    nsub = tk // LANES

    q = q_ref[0]
    q2 = jnp.concatenate([q[:, h * IDX_DIM:(h + 1) * IDX_DIM] for h in range(IDX_HEADS)], axis=0)
    w = w_ref[0]
    for h in range(IDX_HEADS):
        wb_ref[h] = jnp.broadcast_to(w[:, h:h + 1], (tq, LANES))

    row = qi * tq + lax.broadcasted_iota(jnp.int32, (tq, LANES), 0)
    lane = lax.broadcasted_iota(jnp.int32, (tq, LANES), 1)

    def score_body(c, carry):
        mx, mn = carry
        off = pl.multiple_of(c * tk, tk)
        kt = k_ref[0, pl.ds(off, tk), :]
        logits = lax.dot_general(q2, kt, NT_DIMS, preferred_element_type=F32)
        for j in range(nsub):
            sc = jnp.zeros((tq, LANES), F32)
            for h in range(IDX_HEADS):
                lh = logits[h * tq:(h + 1) * tq, j * LANES:(j + 1) * LANES]
                sc = sc + jnp.maximum(lh, 0.0) * wb_ref[h]
            causal = (off + j * LANES + lane) <= row
            score_ref[:, pl.ds(pl.multiple_of(off + j * LANES, LANES), LANES)] = (
                jnp.where(causal, sc, -jnp.inf))
            mx = jnp.maximum(mx, jnp.where(causal, sc, -jnp.inf))
            mn = jnp.minimum(mn, jnp.where(causal, sc, jnp.inf))
        return mx, mn

    mx, mn = lax.fori_loop(0, nck, score_body,
                           (jnp.full((tq, LANES), -jnp.inf, F32), jnp.full((tq, LANES), jnp.inf, F32)))
    hi0 = jnp.max(mx, axis=1, keepdims=True)
    lo0 = jnp.min(mn, axis=1, keepdims=True)

    def count(pred):
        def body(c, acc):
            off = pl.multiple_of(c * tk, tk)
            for j in range(nsub):
                x = score_ref[:, pl.ds(pl.multiple_of(off + j * LANES, LANES), LANES)]
                acc = acc + jnp.where(pred(x), 1.0, 0.0)
            return acc
        acc = lax.fori_loop(0, nck, body, jnp.zeros((tq, LANES), F32))
        return jnp.sum(acc, axis=1, keepdims=True)

    n_valid = (qi * tq + lax.broadcasted_iota(jnp.int32, (tq, 1), 0) + 1).astype(F32)
    kk = jnp.minimum(n_valid, float(ktop))

    def bis_cond(st):
        it, lo, hi, clo, fin = st
        return jnp.logical_and(it < MAX_BISECT, jnp.min(fin) < 0.5)

    def bis_body(st):
        it, lo, hi, clo, fin = st
        mid0 = 0.5 * lo + 0.5 * hi
        collapsed = jnp.logical_or(mid0 <= lo, mid0 >= hi)
        mid = jnp.where(collapsed, hi, mid0)
        midb = jnp.broadcast_to(mid, (tq, LANES))
        c = count(lambda x: x >= midb)
        ge = c >= kk
        live = fin < 0.5
        up = jnp.logical_and(live, ge)
        dn = jnp.logical_and(live, jnp.logical_not(ge))
        lo = jnp.where(up, mid, lo)
        clo = jnp.where(up, c, clo)
        hi = jnp.where(dn, mid, hi)
        done = jnp.logical_or(c == kk, collapsed)
        fin = jnp.where(jnp.logical_and(live, done), 1.0, fin)
        return it + 1, lo, hi, clo, fin

    fin0 = jnp.where(n_valid <= kk, 1.0, 0.0)
    _, thr, _, clo, _ = lax.while_loop(bis_cond, bis_body, (jnp.int32(0), lo0, hi0, n_valid, fin0))
    has_tie = jnp.max(jnp.where(clo > kk, 1.0, 0.0)) > 0.5

    mask_ref[...] = jnp.zeros(mask_ref.shape, mask_ref.dtype)
    thrb = jnp.broadcast_to(thr, (tq, LANES))

    @pl.when(jnp.logical_not(has_tie))
    def _():
        def body(c, _):
            off = pl.multiple_of(c * tk, tk)
            x = score_ref[:, pl.ds(off, tk)]
            sel = jnp.where(x >= jnp.tile(thrb, (1, nsub)), 1, 0)
            mask_ref[0, :, pl.ds(off, tk)] = sel.astype(mask_ref.dtype)
            return 0
        lax.fori_loop(0, nck, body, 0)

    @pl.when(has_tie)
    def _():
        n_gt = count(lambda x: x > thrb)
        need = kk - n_gt
        tri = (lax.broadcasted_iota(jnp.int32, (tk, tk), 0)
               <= lax.broadcasted_iota(jnp.int32, (tk, tk), 1))
        tri = jnp.where(tri, 1.0, 0.0).astype(BF16)

        def body(c, seen):
            off = pl.multiple_of(c * tk, tk)
            x = score_ref[:, pl.ds(off, tk)]
            thr_t = jnp.tile(thrb, (1, nsub))
            eq = jnp.where(x == thr_t, 1.0, 0.0)
            rank = jnp.dot(eq.astype(BF16), tri, preferred_element_type=F32) + seen
            keep_eq = jnp.where(rank <= need, eq, 0.0)
            sel = jnp.where(x > thr_t, 1.0, keep_eq)
            mask_ref[0, :, pl.ds(off, tk)] = sel.astype(jnp.int32).astype(mask_ref.dtype)
            return seen + jnp.sum(eq, axis=1, keepdims=True)
        lax.fori_loop(0, nck, body, jnp.zeros((tq, 1), F32))


def _indexer_mask(q_i, w_i, k_i, tq=128, tk=512):
    B, S, _ = q_i.shape
    ktop = min(TOPK_MAX, S // 4)
    kern = functools.partial(_indexer_kernel, tq=tq, tk=tk, seq=S, ktop=ktop)
    return pl.pallas_call(
        kern,
        grid=(B, S // tq),
        in_specs=[pl.BlockSpec((1, tq, IDX_QW), lambda b, i: (b, i, 0)),
                  pl.BlockSpec((1, tq, IDX_HEADS), lambda b, i: (b, i, 0)),
                  pl.BlockSpec((1, S, IDX_DIM), lambda b, i: (b, 0, 0))],
        out_specs=pl.BlockSpec((1, tq, S), lambda b, i: (b, i, 0)),
        out_shape=jax.ShapeDtypeStruct((B, S, S), jnp.int8),
        scratch_shapes=[pltpu.VMEM((tq, S), F32), pltpu.VMEM((IDX_HEADS, tq, LANES), F32)],
        compiler_params=_cparams(("parallel", "parallel")),
        name="indexer_topk_mask",
    )(q_i, w_i, k_i)


def _flash_stream(q_s, k_ref, v_ref, mask_ref, qi, *, tq, tkv, kcol, vcol, dv):
    nkv = ((qi + 1) * tq + tkv - 1) // tkv
    row = qi * tq + lax.broadcasted_iota(jnp.int32, (tq, tkv), 0)
    colbase = lax.broadcasted_iota(jnp.int32, (tq, tkv), 1)

    def body(c, carry):
        m, l, acc = carry
        off = pl.multiple_of(c * tkv, tkv)
        k_c = k_ref[0, pl.ds(off, tkv), kcol:kcol + HEAD_DIM]
        v_c = v_ref[0, pl.ds(off, tkv), vcol:vcol + dv]
        s = lax.dot_general(q_s, k_c, NT_DIMS, preferred_element_type=F32)
        if mask_ref is None:
            ok = (off + colbase) <= row
        else:
            ok = mask_ref[0, :, pl.ds(off, tkv)].astype(F32) > 0.5
        s = jnp.where(ok, s, MASKED)
        m_new = jnp.maximum(m, jnp.max(s, axis=1, keepdims=True))
        alpha = jnp.exp(m - m_new)
        p = jnp.exp(s - m_new)
        l = alpha * l + jnp.sum(p, axis=1, keepdims=True)
        acc = alpha * acc + jnp.dot(p.astype(BF16), v_c, preferred_element_type=F32)
        return m_new, l, acc

    m0 = jnp.full((tq, 1), MASKED, F32)
    _, l, acc = lax.fori_loop(0, nkv, body, (m0, jnp.zeros((tq, 1), F32), jnp.zeros((tq, dv), F32)))
    return acc, l


def _dsa_attn_kernel(q_ref, k_ref, v_ref, mask_ref, o_ref, *, tq, tkv):
    qi = pl.program_id(1)
    for h in range(DSA_HEADS):
        c0 = h * HEAD_DIM
        acc, l = _flash_stream(q_ref[0, :, c0:c0 + HEAD_DIM], k_ref, v_ref, mask_ref, qi,
                               tq=tq, tkv=tkv, kcol=c0, vcol=c0, dv=HEAD_DIM)
        o_ref[0, :, c0:c0 + HEAD_DIM] = (acc / l).astype(o_ref.dtype)


def _diff_attn_kernel(q_ref, k_ref, v_ref, lq1_ref, lk1_ref, lq2_ref, lk2_ref, g_ref, o_ref,
                      *, tq, tkv, lam_init):
    qi = pl.program_id(1)
    lam = (jnp.exp(jnp.sum(lq1_ref[...] * lk1_ref[...], axis=-1, keepdims=True))
           - jnp.exp(jnp.sum(lq2_ref[...] * lk2_ref[...], axis=-1, keepdims=True)) + lam_init)
    for h in range(DIFF_HEADS):
        outs = []
        for mp in range(2):
            c0 = (2 * h + mp) * DIFF_DIM
            acc, l = _flash_stream(q_ref[0, :, c0:c0 + DIFF_DIM], k_ref, v_ref, None, qi,
                                   tq=tq, tkv=tkv, kcol=c0, vcol=h * DIFF_VDIM, dv=DIFF_VDIM)
            outs.append(acc / l)
        o = outs[0] - lam * outs[1]
        o = _rmsnorm(o, g_ref[...], DIFF_SUBLN_EPS) * (1.0 - lam_init)
        o_ref[0, :, h * DIFF_VDIM:(h + 1) * DIFF_VDIM] = o.astype(o_ref.dtype)


def _resident(shape):
    return pl.BlockSpec(shape, lambda b, i: (b, 0, 0))


def _dsa_attention(q, k, v, mask, tq=256, tkv=512):
    B, S, W = q.shape
    kern = functools.partial(_dsa_attn_kernel, tq=tq, tkv=tkv)
    return pl.pallas_call(
        kern,
        grid=(B, S // tq),
        in_specs=[pl.BlockSpec((1, tq, W), lambda b, i: (b, i, 0)),
                  _resident((1, S, W)), _resident((1, S, W)),
                  pl.BlockSpec((1, tq, S), lambda b, i: (b, i, 0))],
        out_specs=pl.BlockSpec((1, tq, W), lambda b, i: (b, i, 0)),
        out_shape=jax.ShapeDtypeStruct((B, S, W), BF16),
        compiler_params=_cparams(("parallel", "parallel")),
        name="dsa_attention",
    )(q, k, v, mask)


def _diff_attention(q, k, v, lq1, lk1, lq2, lk2, g, lam_init, tq=256, tkv=512):
    B, S, W = q.shape
    kern = functools.partial(_diff_attn_kernel, tq=tq, tkv=tkv, lam_init=lam_init)
    vec = lambda a: pl.BlockSpec(a.shape, lambda b, i: (0, 0))
    return pl.pallas_call(
        kern,
        grid=(B, S // tq),
        in_specs=[pl.BlockSpec((1, tq, W), lambda b, i: (b, i, 0)),
                  _resident((1, S, W)), _resident((1, S, W)),
                  vec(lq1), vec(lk1), vec(lq2), vec(lk2), vec(g)],
        out_specs=pl.BlockSpec((1, tq, W), lambda b, i: (b, i, 0)),
        out_shape=jax.ShapeDtypeStruct((B, S, W), BF16),
        compiler_params=_cparams(("parallel", "parallel")),
        name="diff_attention",
    )(q, k, v, lq1, lk1, lq2, lk2, g)


def _merge_kernel(x_ref, oa_ref, ob_ref, g_ref, wg_ref, gb_ref, wa_ref, wb_ref, wo_ref, y_ref):
    x = x_ref[...]
    h = _rmsnorm(x, g_ref[...], NORM_EPS).astype(BF16)
    gates = jax.nn.sigmoid(jnp.dot(h, wg_ref[...], preferred_element_type=F32) + gb_ref[...])
    pa = jnp.dot(oa_ref[...], wa_ref[...], preferred_element_type=F32)
    pb = jnp.dot(ob_ref[...], wb_ref[...], preferred_element_type=F32)
    merged = gates[:, :D_MODEL] * pa + gates[:, D_MODEL:] * pb
    y_ref[...] = x + jnp.dot(merged.astype(BF16), wo_ref[...], preferred_element_type=F32)


def _merge_project(x2, oa, ob, g, wg, gb, wa, wb, wo, tm=256):
    M = x2.shape[0]
    row = lambda w: pl.BlockSpec((tm, w), lambda i: (i, 0))
    full = lambda a: pl.BlockSpec(a.shape, lambda i: (0,) * a.ndim)
    return pl.pallas_call(
        _merge_kernel,
        grid=(M // tm,),
        in_specs=[row(D_MODEL), row(512), row(512), full(g), full(wg), full(gb),
                  full(wa), full(wb), full(wo)],
        out_specs=row(D_MODEL),
        out_shape=jax.ShapeDtypeStruct((M, D_MODEL), F32),
        compiler_params=_cparams(("parallel",)),
        name="merge_project",
    )(x2, oa, ob, g, wg, gb, wa, wb, wo)


def _ffn_kernel(x_ref, g_ref, wgate_ref, wup_ref, wdown_ref, gf_ref, y_ref, *, final_norm):
    x = x_ref[...]
    h = _rmsnorm(x, g_ref[...], NORM_EPS).astype(BF16)
    gate = jnp.dot(h, wgate_ref[...], preferred_element_type=F32)
    up = jnp.dot(h, wup_ref[...], preferred_element_type=F32)
    act = (gate * jax.nn.sigmoid(gate) * up).astype(BF16)
    y = x + jnp.dot(act, wdown_ref[...], preferred_element_type=F32)
    if final_norm:
        y = _rmsnorm(y, gf_ref[...], NORM_EPS)
    y_ref[...] = y


def _ffn(x2, g, wgate, wup, wdown, gf, final_norm, tm=256):
    M = x2.shape[0]
    row = lambda w: pl.BlockSpec((tm, w), lambda i: (i, 0))
    full = lambda a: pl.BlockSpec(a.shape, lambda i: (0,) * a.ndim)
    return pl.pallas_call(
        functools.partial(_ffn_kernel, final_norm=final_norm),
        grid=(M // tm,),
        in_specs=[row(D_MODEL), full(g), full(wgate), full(wup), full(wdown), full(gf)],
        out_specs=row(D_MODEL),
        out_shape=jax.ShapeDtypeStruct((M, D_MODEL), F32),
        compiler_params=_cparams(("parallel",)),
        name="swiglu_ffn",
    )(x2, g, wgate, wup, wdown, gf)


def _inv_freq_lanes(rot_dim, period):
    inv_freq = jnp.power(jnp.float32(ROPE_THETA), -jnp.arange(0, rot_dim, 2, dtype=F32) / rot_dim)
    d = jnp.arange(LANES) % period
    return jnp.where(d < rot_dim, inv_freq[d % (rot_dim // 2)], 0.0).astype(F32)[None, :]


def _pad_lanes(v):
    return jnp.pad(v.astype(F32), (0, LANES - v.shape[0]))[None, :]


def kernel(x, positions, norm_mix_g, w_in, idx_k_norm_g, idx_k_norm_b, diff_lambda_q1,
           diff_lambda_k1, diff_lambda_q2, diff_lambda_k2, diff_subln_g, gate_b, w_branch_dsa,
           w_branch_diff, w_out, norm_ffn_g, w_ffn_in, w_ffn_out, norm_final_g):
    B, S, D = x.shape
    M = B * S
    depth = w_in.shape[0]
    x2 = x.reshape(M, D)
    posf = positions.astype(F32).reshape(M, 1)
    invh = _inv_freq_lanes(ROT_DIM_HEAD, HEAD_DIM)
    invi = _inv_freq_lanes(ROT_DIM_IDX, IDX_DIM)
    row = lambda v: v.astype(F32)[None, :]

    o_a, o_i, o_b, o_g = 0, 3 * DSA_WIDTH, 3 * DSA_WIDTH + IDX_QW + IDX_DIM + IDX_HEADS, None
    n_idx = IDX_QW + IDX_DIM + IDX_HEADS
    o_g = o_b + 2 * DIFF_QK_WIDTH + DIFF_WIDTH

    for l in range(depth):
        lam_init = 0.8 - 0.6 * math.exp(-0.3 * l)
        wl = w_in[l]
        wa = wl[:, o_a:o_a + 3 * DSA_WIDTH].astype(BF16)
        wi = jnp.pad(wl[:, o_i:o_i + n_idx], ((0, 0), (0, IDX_QW + LANES - n_idx))).astype(BF16)
        wb = wl[:, o_b:o_b + 3 * 512].astype(BF16)
        wg = wl[:, o_g:o_g + N_BRANCH * D_MODEL].astype(BF16)

        (q_a, k_a, v_a, q_b, k_b, v_b, q_i, k_i, w_i) = _in_projection(
            x2, posf, row(norm_mix_g[l]), wa, wb, wi, invh, invi,
            _pad_lanes(idx_k_norm_g[l]), _pad_lanes(idx_k_norm_b[l]))

        r3 = lambda a: a.reshape(B, S, a.shape[-1])
        mask = _indexer_mask(r3(q_i), r3(w_i), r3(k_i))
        o_dsa = _dsa_attention(r3(q_a), r3(k_a), r3(v_a), mask)
        o_diff = _diff_attention(r3(q_b), r3(k_b), r3(v_b),
                                 row(diff_lambda_q1[l]), row(diff_lambda_k1[l]),
                                 row(diff_lambda_q2[l]), row(diff_lambda_k2[l]),
                                 row(diff_subln_g[l]), lam_init)

        x2 = _merge_project(x2, o_dsa.reshape(M, DSA_WIDTH), o_diff.reshape(M, DIFF_WIDTH),
                            row(norm_mix_g[l]), wg, row(gate_b[l]),
                            w_branch_dsa[l].astype(BF16), w_branch_diff[l].astype(BF16),
                            w_out[l].astype(BF16))
        wf = w_ffn_in[l]
        x2 = _ffn(x2, row(norm_ffn_g[l]), wf[:, :D_FF].astype(BF16), wf[:, D_FF:].astype(BF16),
                  w_ffn_out[l].astype(BF16), row(norm_final_g), final_norm=(l == depth - 1))

    if depth == 0:
        raise ValueError("depth must be positive")
    return x2.reshape(B, S, D)
```

```python
import functools
import math

import jax
import jax.numpy as jnp
from jax import lax
from jax.experimental import pallas as pl
from jax.experimental.pallas import tpu as pltpu

F32 = jnp.float32
BF16 = jnp.bfloat16

D_MODEL = 1024
HEAD_DIM = 64
DSA_HEADS = 8
DSA_WIDTH = DSA_HEADS * HEAD_DIM
IDX_HEADS = 8
IDX_DIM = 32
IDX_QW = IDX_HEADS * IDX_DIM
TOPK_MAX = 256
DIFF_HEADS = 4
DIFF_DIM = 64
DIFF_VDIM = 2 * DIFF_DIM
DIFF_QK_WIDTH = DIFF_HEADS * 2 * DIFF_DIM
DIFF_WIDTH = DIFF_HEADS * DIFF_VDIM
DIFF_SUBLN_EPS = 1e-5
N_BRANCH = 2
ROPE_THETA = 500000.0
ROT_DIM_HEAD = HEAD_DIM // 4
ROT_DIM_IDX = IDX_DIM // 4
FFN_MULT = 256
D_FF = -(-8 * D_MODEL // (3 * FFN_MULT)) * FFN_MULT
NORM_EPS = 1e-6

LANES = 128
VMEM_LIMIT = 56 * 1024 * 1024
MASKED = -1e30
MAX_BISECT = 512
LOG2E = 1.4426950408889634
N_STREAMS = 8
ROWS = 64

NT_DIMS = (((1,), (1,)), ((), ()))


def _cparams(sem, flags=None):
    return pltpu.CompilerParams(dimension_semantics=sem, vmem_limit_bytes=VMEM_LIMIT, flags=flags)


def _rmsnorm(x, g, eps):
    return x * lax.rsqrt(jnp.mean(x * x, axis=-1, keepdims=True) + eps) * g


def _rope_block(y, cos, sin_lo, sin_hi, half):
    return (y * cos + pltpu.roll(y, half, 1) * sin_hi
            + pltpu.roll(y, LANES - half, 1) * sin_lo)


def _rope_tables(pos, inv_lane, period, half):
    ang = pos * inv_lane
    cos = jnp.cos(ang)
    sin = jnp.sin(ang)
    d = lax.broadcasted_iota(jnp.int32, (1, LANES), 1) % period
    sin_lo = jnp.where(d < half, -sin, 0.0)
    sin_hi = jnp.where((d >= half) & (d < 2 * half), sin, 0.0)
    return cos, sin_lo, sin_hi


def _inproj_kernel(x_ref, pos_ref, g_ref, wa_ref, wb_ref, wi_ref, invh_ref, invi_ref,
                   lng_ref, lnb_ref,
                   qa_ref, ka_ref, vat_ref, qb_ref, kb_ref, vbt_ref, qi_ref, ki_ref, wit_ref):
    h = _rmsnorm(x_ref[0], g_ref[...], NORM_EPS).astype(BF16)
    pos = pos_ref[0]
    cos_h, slo_h, shi_h = _rope_tables(pos, invh_ref[...], HEAD_DIM, ROT_DIM_HEAD // 2)
    cos_i, slo_i, shi_i = _rope_tables(pos, invi_ref[...], IDX_DIM, ROT_DIM_IDX // 2)
    q_scale = HEAD_DIM ** -0.5 * LOG2E

    def branch(w_ref, q_ref, k_ref, vt_ref):
        for grp, (o_ref, scale) in enumerate(((q_ref, q_scale), (k_ref, 1.0))):
            y = jnp.dot(h, w_ref[:, grp * 512:(grp + 1) * 512], preferred_element_type=F32)
            for j in range(512 // LANES):
                blk = _rope_block(y[:, j * LANES:(j + 1) * LANES], cos_h, slo_h, shi_h,
                                  ROT_DIM_HEAD // 2)
                if scale != 1.0:
                    blk = blk * scale
                o_ref[0, :, j * LANES:(j + 1) * LANES] = blk.astype(o_ref.dtype)
        v = jnp.dot(h, w_ref[:, 1024:1536], preferred_element_type=F32)
        vt_ref[0] = v.T.astype(vt_ref.dtype)

    branch(wa_ref, qa_ref, ka_ref, vat_ref)
    branch(wb_ref, qb_ref, kb_ref, vbt_ref)

    r = jnp.dot(h, wi_ref[...], preferred_element_type=F32)
    for j in range(IDX_QW // LANES):
        blk = _rope_block(r[:, j * LANES:(j + 1) * LANES], cos_i, slo_i, shi_i, ROT_DIM_IDX // 2)
        qi_ref[0, :, j * LANES:(j + 1) * LANES] = blk.astype(qi_ref.dtype)
    kw = r[:, IDX_QW:IDX_QW + LANES]
    lane = lax.broadcasted_iota(jnp.int32, (1, LANES), 1)
    is_k = lane < IDX_DIM
    mu = jnp.sum(jnp.where(is_k, kw, 0.0), axis=-1, keepdims=True) * (1.0 / IDX_DIM)
    xc = jnp.where(is_k, kw - mu, 0.0)
    var = jnp.sum(xc * xc, axis=-1, keepdims=True) * (1.0 / IDX_DIM)
    kn = xc * lax.rsqrt(var + NORM_EPS) * lng_ref[...] + lnb_ref[...]
    kn = _rope_block(kn, cos_i, slo_i, shi_i, ROT_DIM_IDX // 2)
    ki_ref[0] = kn[:, :IDX_DIM].astype(ki_ref.dtype)
    w_scale = IDX_HEADS ** -0.5 * IDX_DIM ** -0.5
    wit_ref[0] = kw.T[IDX_DIM:IDX_DIM + IDX_HEADS, :] * w_scale


def _in_projection(x, posf, g, wa, wb, wi, invh, invi, lng, lnb, tm=256):
    B, S, _ = x.shape
    row = lambda w: pl.BlockSpec((1, tm, w), lambda b, i: (b, i, 0))
    col = lambda r: pl.BlockSpec((1, r, tm), lambda b, i: (b, 0, i))
    full = lambda a: pl.BlockSpec(a.shape, lambda b, i: (0,) * a.ndim)
    sds = jax.ShapeDtypeStruct
    out_shape = [sds((B, S, 512), BF16), sds((B, S, 512), BF16), sds((B, 512, S), BF16),
                 sds((B, S, 512), BF16), sds((B, S, 512), BF16), sds((B, 512, S), BF16),
                 sds((B, S, IDX_QW), BF16), sds((B, S, IDX_DIM), BF16), sds((B, IDX_HEADS, S), F32)]
    return pl.pallas_call(
        _inproj_kernel,
        grid=(B, S // tm),
        in_specs=[row(D_MODEL), row(1), full(g), full(wa), full(wb), full(wi),
                  full(invh), full(invi), full(lng), full(lnb)],
        out_specs=[row(512), row(512), col(512), row(512), row(512), col(512),
                   row(IDX_QW), row(IDX_DIM), col(IDX_HEADS)],
        out_shape=out_shape,
        compiler_params=_cparams(("parallel", "parallel")),
        name="in_projection",
    )(x, posf, g, wa, wb, wi, invh, invi, lng, lnb)


def _init_streams(q_ref, qpad_ref, m_ref, l_ref, acc_ref):
    lane = lax.broadcasted_iota(jnp.int32, (1, LANES), 1)
    for s in range(N_STREAMS):
        blk = q_ref[0, :, (s // 2) * LANES:(s // 2 + 1) * LANES]
        mine = (lane >= HEAD_DIM) if s % 2 else (lane < HEAD_DIM)
        qpad_ref[s] = jnp.where(mine, blk, jnp.zeros_like(blk))
    m_ref[...] = jnp.full(m_ref.shape, MASKED, F32)
    l_ref[...] = jnp.zeros(l_ref.shape, F32)
    acc_ref[...] = jnp.zeros(acc_ref.shape, F32)


def _attend_chunk(k_ref, vt_ref, qpad_ref, m_ref, l_ref, acc_ref, st_ref, off, bias,
                  *, tkv, dv, v_head):
    for s in range(N_STREAMS):
        kp = k_ref[0, pl.ds(off, tkv), (s // 2) * LANES:(s // 2 + 1) * LANES]
        st = lax.dot_general(kp, qpad_ref[s], NT_DIMS, preferred_element_type=F32)
        if bias is not None:
            st = st + bias
        st_ref[s] = st
    for s in range(N_STREAMS):
        st = st_ref[s]
        m_old = m_ref[s]
        m_new = jnp.maximum(m_old, jnp.max(st, axis=0, keepdims=True))
        alpha = jnp.exp2(m_old - m_new)
        p = jnp.exp2(st - m_new)
        l_ref[s] = alpha * l_ref[s] + jnp.sum(p, axis=0, keepdims=True)
        r0 = v_head(s) * dv
        pv = jnp.dot(vt_ref[0, r0:r0 + dv, pl.ds(off, tkv)], p.astype(BF16),
                     preferred_element_type=F32)
        acc_ref[s] = alpha * acc_ref[s] + pv
        m_ref[s] = m_new


def _dsa_kernel(qi_ref, wt_ref, kidx_ref, q_ref, k_ref, vt_ref, o_ref,
                score_ref, thr_ref, qpad_ref, m_ref, l_ref, acc_ref, ot_ref, st_ref,
                *, tq, tk, ktop):
    qi = pl.program_id(1)
    nck = ((qi + 1) * tq + tk - 1) // tk
    nsub = tk // ROWS

    qx = qi_ref[0]
    q2 = jnp.concatenate([qx[:, h * IDX_DIM:(h + 1) * IDX_DIM] for h in range(IDX_HEADS)], axis=0)
    wt = wt_ref[0]
    qpos = qi * tq + lax.broadcasted_iota(jnp.int32, (1, tq), 1)
    krow = lax.broadcasted_iota(jnp.int32, (tk, tq), 0)

    def score_body(c, carry):
        mx, mn, c0, cp = carry
        off = pl.multiple_of(c * tk, tk)
        kc = kidx_ref[0, pl.ds(off, tk), :]
        logits = lax.dot_general(kc, q2, NT_DIMS, preferred_element_type=F32)
        sc = jnp.maximum(logits[:, :tq], 0.0) * wt[0:1, :]
        for h in range(1, IDX_HEADS):
            sc = sc + jnp.maximum(logits[:, h * tq:(h + 1) * tq], 0.0) * wt[h:h + 1, :]
        causal = (off + krow) <= qpos
        s_hi = jnp.where(causal, sc, -jnp.inf)
        s_lo = jnp.where(causal, sc, jnp.inf)
        score_ref[pl.ds(off, tk), :] = s_hi
        for j in range(nsub):
            b_hi = s_hi[j * ROWS:(j + 1) * ROWS]
            mx = jnp.maximum(mx, b_hi)
            mn = jnp.minimum(mn, s_lo[j * ROWS:(j + 1) * ROWS])
            c0 = c0 + jnp.where(b_hi >= 0.0, 1.0, 0.0)
            cp = cp + jnp.where(b_hi > 0.0, 1.0, 0.0)
        return mx, mn, c0, cp

    zero = jnp.zeros((ROWS, tq), F32)
    mx, mn, c0, cp = lax.fori_loop(0, nck, score_body,
                                   (zero - jnp.inf, zero + jnp.inf, zero, zero))
    hi0 = jnp.max(mx, axis=0, keepdims=True)
    lo0 = jnp.min(mn, axis=0, keepdims=True)
    n_ge0 = jnp.sum(c0, axis=0, keepdims=True)
    n_gt0 = jnp.sum(cp, axis=0, keepdims=True)

    def count(pred):
        def body(c, acc):
            off = pl.multiple_of(c * tk, tk)
            for j in range(nsub):
                x = score_ref[pl.ds(pl.multiple_of(off + j * ROWS, ROWS), ROWS), :]
                acc = acc + jnp.where(pred(x), 1.0, 0.0)
            return acc
        return jnp.sum(lax.fori_loop(0, nck, body, zero), axis=0, keepdims=True)

    n_valid = (qpos + 1).astype(F32)
    kk = jnp.minimum(n_valid, float(ktop))

    all_in = n_valid <= kk
    above = jnp.logical_and(jnp.logical_not(all_in), n_gt0 >= kk)
    below = jnp.logical_and(jnp.logical_not(all_in), n_ge0 < kk)
    lo = jnp.where(all_in, lo0, jnp.where(below, lo0, 0.0))
    clo = jnp.where(all_in, n_valid, jnp.where(below, n_valid, n_ge0))
    hi = jnp.where(below, 0.0, hi0)
    fin = jnp.where(jnp.logical_or(above, below), 0.0, 1.0)

    def bis_cond(st):
        it, lo, hi, clo, fin = st
        return jnp.logical_and(it < MAX_BISECT, jnp.min(fin) < 0.5)

    def bis_body(st):
        it, lo, hi, clo, fin = st
        mid0 = 0.5 * lo + 0.5 * hi
        collapsed = jnp.logical_or(mid0 <= lo, mid0 >= hi)
        mid = jnp.where(collapsed, hi, mid0)
        c = count(lambda x: x >= mid)
        ge = c >= kk
        live = fin < 0.5
        up = jnp.logical_and(live, ge)
        dn = jnp.logical_and(live, jnp.logical_not(ge))
        lo = jnp.where(up, mid, lo)
        clo = jnp.where(up, c, clo)
        hi = jnp.where(dn, mid, hi)
        done = jnp.logical_or(c == kk, collapsed)
        fin = jnp.where(jnp.logical_and(live, done), 1.0, fin)
        return it + 1, lo, hi, clo, fin

    _, thr, _, clo, _ = lax.while_loop(bis_cond, bis_body, (jnp.int32(0), lo, hi, clo, fin))
    thr_ref[...] = thr
    has_tie = jnp.max(jnp.where(clo > kk, 1.0, 0.0)) > 0.5

    @pl.when(has_tie)
    def _():
        n_gt = count(lambda x: x > thr)
        need = kk - n_gt
        lower = (lax.broadcasted_iota(jnp.int32, (tk, tk), 0)
                 >= lax.broadcasted_iota(jnp.int32, (tk, tk), 1))
        lower = jnp.where(lower, 1.0, 0.0).astype(BF16)

        def body(c, seen):
            off = pl.multiple_of(c * tk, tk)
            x = score_ref[pl.ds(off, tk), :]
            eq = jnp.where(x == thr, 1.0, 0.0)
            rank = jnp.dot(lower, eq.astype(BF16), preferred_element_type=F32) + seen
            keep = jnp.where(x > thr, 1.0, jnp.where(rank <= need, eq, 0.0))
            score_ref[pl.ds(off, tk), :] = jnp.where(keep > 0.5, 1.0, -1.0)
            return seen + jnp.sum(eq, axis=0, keepdims=True)
        lax.fori_loop(0, nck, body, jnp.zeros((1, tq), F32))
        thr_ref[...] = jnp.zeros((1, tq), F32)

    _init_streams(q_ref, qpad_ref, m_ref, l_ref, acc_ref)
    thr_sel = thr_ref[...]

    def att_body(c, _):
        off = pl.multiple_of(c * tk, tk)
        bias = jnp.where(score_ref[pl.ds(off, tk), :] >= thr_sel, 0.0, MASKED)
        _attend_chunk(k_ref, vt_ref, qpad_ref, m_ref, l_ref, acc_ref, st_ref, off, bias,
                      tkv=tk, dv=HEAD_DIM, v_head=lambda s: s)
        return 0
    lax.fori_loop(0, nck, att_body, 0)

    for s in range(N_STREAMS):
        ot_ref[s * HEAD_DIM:(s + 1) * HEAD_DIM, :] = acc_ref[s] / l_ref[s]
    o_ref[0] = ot_ref[...].T.astype(o_ref.dtype)


def _resident(shape):
    return pl.BlockSpec(shape, lambda b, i: (b, 0, 0), pipeline_mode=pl.Buffered(1))


def _dsa_branch(q_i, w_it, k_i, q, k, vt, tq=256, tk=512):
    B, S, W = q.shape
    ktop = min(TOPK_MAX, S // 4)
    kern = functools.partial(_dsa_kernel, tq=tq, tk=tk, ktop=ktop)
    return pl.pallas_call(
        kern,
        grid=(B, S // tq),
        in_specs=[pl.BlockSpec((1, tq, IDX_QW), lambda b, i: (b, i, 0)),
                  pl.BlockSpec((1, IDX_HEADS, tq), lambda b, i: (b, 0, i)),
                  _resident((1, S, IDX_DIM)),
                  pl.BlockSpec((1, tq, W), lambda b, i: (b, i, 0)),
                  _resident((1, S, W)), _resident((1, W, S))],
        out_specs=pl.BlockSpec((1, tq, W), lambda b, i: (b, i, 0)),
        out_shape=jax.ShapeDtypeStruct((B, S, W), BF16),
        scratch_shapes=[pltpu.VMEM((S, tq), F32), pltpu.VMEM((1, tq), F32),
                        pltpu.VMEM((N_STREAMS, tq, LANES), BF16),
                        pltpu.VMEM((N_STREAMS, 1, tq), F32), pltpu.VMEM((N_STREAMS, 1, tq), F32),
                        pltpu.VMEM((N_STREAMS, HEAD_DIM, tq), F32), pltpu.VMEM((W, tq), F32),
                        pltpu.VMEM((N_STREAMS, tk, tq), F32)],
        compiler_params=_cparams(("parallel", "arbitrary")),
        name="dsa_branch",
    )(q_i, w_it, k_i, q, k, vt)


def _diff_kernel(q_ref, k_ref, vt_ref, lq1_ref, lk1_ref, lq2_ref, lk2_ref, g_ref, o_ref,
                 qpad_ref, m_ref, l_ref, acc_ref, ot_ref, st_ref, *, tq, tkv, lam_init):
    qi = pl.program_id(1)
    nck = ((qi + 1) * tq + tkv - 1) // tkv
    nfull = (qi * tq + 1) // tkv
    lam = (jnp.exp(jnp.sum(lq1_ref[...] * lk1_ref[...], axis=-1, keepdims=True))
           - jnp.exp(jnp.sum(lq2_ref[...] * lk2_ref[...], axis=-1, keepdims=True)) + lam_init)
    _init_streams(q_ref, qpad_ref, m_ref, l_ref, acc_ref)
    attend = functools.partial(_attend_chunk, k_ref, vt_ref, qpad_ref, m_ref, l_ref, acc_ref,
                               st_ref, tkv=tkv, dv=DIFF_VDIM, v_head=lambda s: s // 2)

    def full_body(c, _):
        attend(pl.multiple_of(c * tkv, tkv), None)
        return 0
    lax.fori_loop(0, nfull, full_body, 0)

    qpos = qi * tq + lax.broadcasted_iota(jnp.int32, (1, tq), 1)
    krow = lax.broadcasted_iota(jnp.int32, (tkv, tq), 0)

    def diag_body(c, _):
        off = pl.multiple_of(c * tkv, tkv)
        attend(off, jnp.where((off + krow) <= qpos, 0.0, MASKED))
        return 0
    lax.fori_loop(nfull, nck, diag_body, 0)

    for h in range(DIFF_HEADS):
        o1 = acc_ref[2 * h] / l_ref[2 * h]
        o2 = acc_ref[2 * h + 1] / l_ref[2 * h + 1]
        ot_ref[h * DIFF_VDIM:(h + 1) * DIFF_VDIM, :] = o1 - lam * o2
    o = ot_ref[...].T
    for h in range(DIFF_HEADS):
        blk = _rmsnorm(o[:, h * DIFF_VDIM:(h + 1) * DIFF_VDIM], g_ref[...], DIFF_SUBLN_EPS)
        o_ref[0, :, h * DIFF_VDIM:(h + 1) * DIFF_VDIM] = (blk * (1.0 - lam_init)).astype(o_ref.dtype)


def _diff_branch(q, k, vt, lq1, lk1, lq2, lk2, g, lam_init, tq=256, tkv=512):
    B, S, W = q.shape
    kern = functools.partial(_diff_kernel, tq=tq, tkv=tkv, lam_init=lam_init)
    vec = lambda a: pl.BlockSpec(a.shape, lambda b, i: (0, 0))
    return pl.pallas_call(
        kern,
        grid=(B, S // tq),
        in_specs=[pl.BlockSpec((1, tq, W), lambda b, i: (b, i, 0)),
                  _resident((1, S, W)), _resident((1, W, S)),
                  vec(lq1), vec(lk1), vec(lq2), vec(lk2), vec(g)],
        out_specs=pl.BlockSpec((1, tq, W), lambda b, i: (b, i, 0)),
        out_shape=jax.ShapeDtypeStruct((B, S, W), BF16),
        scratch_shapes=[pltpu.VMEM((N_STREAMS, tq, LANES), BF16),
                        pltpu.VMEM((N_STREAMS, 1, tq), F32), pltpu.VMEM((N_STREAMS, 1, tq), F32),
                        pltpu.VMEM((N_STREAMS, DIFF_VDIM, tq), F32), pltpu.VMEM((W, tq), F32),
                        pltpu.VMEM((N_STREAMS, tkv, tq), F32)],
        compiler_params=_cparams(("parallel", "arbitrary")),
        name="diff_branch",
    )(q, k, vt, lq1, lk1, lq2, lk2, g)


def _merge_kernel(x_ref, oa_ref, ob_ref, g_ref, wg_ref, gb_ref, wa_ref, wb_ref, wo_ref, y_ref):
    x = x_ref[...]
    h = _rmsnorm(x, g_ref[...], NORM_EPS).astype(BF16)
    gates = jax.nn.sigmoid(jnp.dot(h, wg_ref[...], preferred_element_type=F32) + gb_ref[...])
    pa = jnp.dot(oa_ref[...], wa_ref[...], preferred_element_type=F32)
    pb = jnp.dot(ob_ref[...], wb_ref[...], preferred_element_type=F32)
    merged = gates[:, :D_MODEL] * pa + gates[:, D_MODEL:] * pb
    y_ref[...] = x + jnp.dot(merged.astype(BF16), wo_ref[...], preferred_element_type=F32)


def _merge_project(x2, oa, ob, g, wg, gb, wa, wb, wo, tm=256):
    M = x2.shape[0]
    row = lambda w: pl.BlockSpec((tm, w), lambda i: (i, 0))
    full = lambda a: pl.BlockSpec(a.shape, lambda i: (0,) * a.ndim)
    return pl.pallas_call(
        _merge_kernel,
        grid=(M // tm,),
        in_specs=[row(D_MODEL), row(512), row(512), full(g), full(wg), full(gb),
                  full(wa), full(wb), full(wo)],
        out_specs=row(D_MODEL),
        out_shape=jax.ShapeDtypeStruct((M, D_MODEL), F32),
        compiler_params=_cparams(("parallel",)),
        name="merge_project",
    )(x2, oa, ob, g, wg, gb, wa, wb, wo)


def _ffn_kernel(x_ref, g_ref, wgate_ref, wup_ref, wdown_ref, gf_ref, y_ref, *, final_norm):
    x = x_ref[...]
    h = _rmsnorm(x, g_ref[...], NORM_EPS).astype(BF16)
    gate = jnp.dot(h, wgate_ref[...], preferred_element_type=F32)
    up = jnp.dot(h, wup_ref[...], preferred_element_type=F32)
    act = (gate * jax.nn.sigmoid(gate) * up).astype(BF16)
    y = x + jnp.dot(act, wdown_ref[...], preferred_element_type=F32)
    if final_norm:
        y = _rmsnorm(y, gf_ref[...], NORM_EPS)
    y_ref[...] = y


def _ffn(x2, g, wgate, wup, wdown, gf, final_norm, tm=256):
    M = x2.shape[0]
    row = lambda w: pl.BlockSpec((tm, w), lambda i: (i, 0))
    full = lambda a: pl.BlockSpec(a.shape, lambda i: (0,) * a.ndim)
    return pl.pallas_call(
        functools.partial(_ffn_kernel, final_norm=final_norm),
        grid=(M // tm,),
        in_specs=[row(D_MODEL), full(g), full(wgate), full(wup), full(wdown), full(gf)],
        out_specs=row(D_MODEL),
        out_shape=jax.ShapeDtypeStruct((M, D_MODEL), F32),
        compiler_params=_cparams(("parallel",)),
        name="swiglu_ffn",
    )(x2, g, wgate, wup, wdown, gf)


def _inv_freq_lanes(rot_dim, period):
    inv_freq = jnp.power(jnp.float32(ROPE_THETA), -jnp.arange(0, rot_dim, 2, dtype=F32) / rot_dim)
    d = jnp.arange(LANES) % period
    return jnp.where(d < rot_dim, inv_freq[d % (rot_dim // 2)], 0.0).astype(F32)[None, :]


def _pad_lanes(v):
    return jnp.pad(v.astype(F32), (0, LANES - v.shape[0]))[None, :]


def kernel(x, positions, norm_mix_g, w_in, idx_k_norm_g, idx_k_norm_b, diff_lambda_q1,
           diff_lambda_k1, diff_lambda_q2, diff_lambda_k2, diff_subln_g, gate_b, w_branch_dsa,
           w_branch_diff, w_out, norm_ffn_g, w_ffn_in, w_ffn_out, norm_final_g):
    B, S, D = x.shape
    M = B * S
    depth = w_in.shape[0]
    if depth == 0:
        raise ValueError("depth must be positive")
    posf = positions.astype(F32)[..., None]
    invh = _inv_freq_lanes(ROT_DIM_HEAD, HEAD_DIM)
    invi = _inv_freq_lanes(ROT_DIM_IDX, IDX_DIM)
    row = lambda v: v.astype(F32)[None, :]

    n_idx = IDX_QW + IDX_DIM + IDX_HEADS
    o_a = 0
    o_i = 3 * DSA_WIDTH
    o_b = o_i + n_idx
    o_g = o_b + 2 * DIFF_QK_WIDTH + DIFF_WIDTH

    x2 = x.reshape(M, D)
    for l in range(depth):
        lam_init = 0.8 - 0.6 * math.exp(-0.3 * l)
        wl = w_in[l]
        wa = wl[:, o_a:o_a + 3 * DSA_WIDTH].astype(BF16)
        wi = jnp.pad(wl[:, o_i:o_i + n_idx], ((0, 0), (0, IDX_QW + LANES - n_idx))).astype(BF16)
        wb = wl[:, o_b:o_b + 3 * 512].astype(BF16)
        wg = wl[:, o_g:o_g + N_BRANCH * D_MODEL].astype(BF16)

        (q_a, k_a, vt_a, q_b, k_b, vt_b, q_i, k_i, w_it) = _in_projection(
            x2.reshape(B, S, D), posf, row(norm_mix_g[l]), wa, wb, wi, invh, invi,
            _pad_lanes(idx_k_norm_g[l]), _pad_lanes(idx_k_norm_b[l]))

        o_dsa = _dsa_branch(q_i, w_it, k_i, q_a, k_a, vt_a)
        o_diff = _diff_branch(q_b, k_b, vt_b,
                              row(diff_lambda_q1[l]), row(diff_lambda_k1[l]),
                              row(diff_lambda_q2[l]), row(diff_lambda_k2[l]),
                              row(diff_subln_g[l]), lam_init)

        x2 = _merge_project(x2, o_dsa.reshape(M, DSA_WIDTH), o_diff.reshape(M, DIFF_WIDTH),
                            row(norm_mix_g[l]), wg, row(gate_b[l]),
                            w_branch_dsa[l].astype(BF16), w_branch_diff[l].astype(BF16),
                            w_out[l].astype(BF16))
        wf = w_ffn_in[l]
        x2 = _ffn(x2, row(norm_ffn_g[l]), wf[:, :D_FF].astype(BF16), wf[:, D_FF:].astype(BF16),
                  w_ffn_out[l].astype(BF16), row(norm_final_g), final_norm=(l == depth - 1))

    return x2.reshape(B, S, D)
```

```python
import functools
import math

import jax
import jax.numpy as jnp
from jax import lax
from jax.experimental import pallas as pl
from jax.experimental.pallas import tpu as pltpu

F32 = jnp.float32
BF16 = jnp.bfloat16

D_MODEL = 1024
HEAD_DIM = 64
DSA_HEADS = 8
DSA_WIDTH = DSA_HEADS * HEAD_DIM
IDX_HEADS = 8
IDX_DIM = 32
IDX_QW = IDX_HEADS * IDX_DIM
TOPK_MAX = 256
DIFF_HEADS = 4
DIFF_DIM = 64
DIFF_VDIM = 2 * DIFF_DIM
DIFF_QK_WIDTH = DIFF_HEADS * 2 * DIFF_DIM
DIFF_WIDTH = DIFF_HEADS * DIFF_VDIM
DIFF_SUBLN_EPS = 1e-5
N_BRANCH = 2
ROPE_THETA = 500000.0
ROT_DIM_HEAD = HEAD_DIM // 4
ROT_DIM_IDX = IDX_DIM // 4
FFN_MULT = 256
D_FF = -(-8 * D_MODEL // (3 * FFN_MULT)) * FFN_MULT
NORM_EPS = 1e-6

LANES = 128
VMEM_LIMIT = 56 * 1024 * 1024
MASKED = -1e30
MAX_BISECT = 512
LOG2E = 1.4426950408889634
N_STREAMS = 8
ROWS = 64
DEN_ROWS = 16
TRIG_ROWS = 16
TRIG_ONE = 12
TRIG_ZERO = TRIG_ROWS + 12

NT_DIMS = (((1,), (1,)), ((), ()))


def _cparams(sem, flags=None):
    return pltpu.CompilerParams(dimension_semantics=sem, vmem_limit_bytes=VMEM_LIMIT, flags=flags)


def _rmsnorm(x, g, eps):
    return x * lax.rsqrt(jnp.mean(x * x, axis=-1, keepdims=True) + eps) * g


def _rope_block(y, cos, sin_lo, sin_hi, half):
    return (y * cos + pltpu.roll(y, half, 1) * sin_hi
            + pltpu.roll(y, LANES - half, 1) * sin_lo)


def _trig_table(pos_t, inv_col):
    ang_t = inv_col * pos_t
    pad = jnp.zeros((LANES - 2 * TRIG_ROWS, ang_t.shape[1]), F32)
    return jnp.concatenate([jnp.cos(ang_t), jnp.sin(ang_t), pad], axis=0).T


def _rope_tables(tbl, first, period, half):
    d = lax.broadcasted_iota(jnp.int32, tbl.shape, 1) % period
    rot = d < 2 * half
    take = lambda idx: jnp.take_along_axis(tbl, idx, axis=1)
    cos = take(jnp.where(rot, first + d % half, TRIG_ONE))
    sin_lo = -take(jnp.where(d < half, TRIG_ROWS + first + d, TRIG_ZERO))
    sin_hi = take(jnp.where(rot & (d >= half), TRIG_ROWS + first + d % half, TRIG_ZERO))
    return cos, sin_lo, sin_hi


def _inproj_kernel(x_ref, post_ref, g_ref, wa_ref, wb_ref, wi_ref, invcol_ref,
                   lng_ref, lnb_ref,
                   qa_ref, ka_ref, vat_ref, qb_ref, kb_ref, vbt_ref, qi_ref, ki_ref, wit_ref):
    h = _rmsnorm(x_ref[0], g_ref[...], NORM_EPS).astype(BF16)
    tbl = _trig_table(post_ref[0], invcol_ref[...])
    cos_h, slo_h, shi_h = _rope_tables(tbl, 0, HEAD_DIM, ROT_DIM_HEAD // 2)
    cos_i, slo_i, shi_i = _rope_tables(tbl, ROT_DIM_HEAD // 2, IDX_DIM, ROT_DIM_IDX // 2)
    q_scale = HEAD_DIM ** -0.5 * LOG2E

    def branch(w_ref, q_ref, k_ref, vt_ref):
        for grp, (o_ref, scale) in enumerate(((q_ref, q_scale), (k_ref, 1.0))):
            y = jnp.dot(h, w_ref[:, grp * 512:(grp + 1) * 512], preferred_element_type=F32)
            for j in range(512 // LANES):
                blk = _rope_block(y[:, j * LANES:(j + 1) * LANES], cos_h, slo_h, shi_h,
                                  ROT_DIM_HEAD // 2)
                if scale != 1.0:
                    blk = blk * scale
                o_ref[0, :, j * LANES:(j + 1) * LANES] = blk.astype(o_ref.dtype)
        v = jnp.dot(h, w_ref[:, 1024:1536], preferred_element_type=F32)
        vt_ref[0] = v.T.astype(vt_ref.dtype)

    branch(wa_ref, qa_ref, ka_ref, vat_ref)
    branch(wb_ref, qb_ref, kb_ref, vbt_ref)

    r = jnp.dot(h, wi_ref[...], preferred_element_type=F32)
    for j in range(IDX_QW // LANES):
        blk = _rope_block(r[:, j * LANES:(j + 1) * LANES], cos_i, slo_i, shi_i, ROT_DIM_IDX // 2)
        qi_ref[0, :, j * LANES:(j + 1) * LANES] = blk.astype(qi_ref.dtype)
    kw = r[:, IDX_QW:IDX_QW + LANES]
    lane = lax.broadcasted_iota(jnp.int32, (1, LANES), 1)
    is_k = lane < IDX_DIM
    mu = jnp.sum(jnp.where(is_k, kw, 0.0), axis=-1, keepdims=True) * (1.0 / IDX_DIM)
    xc = jnp.where(is_k, kw - mu, 0.0)
    var = jnp.sum(xc * xc, axis=-1, keepdims=True) * (1.0 / IDX_DIM)
    kn = xc * lax.rsqrt(var + NORM_EPS) * lng_ref[...] + lnb_ref[...]
    kn = _rope_block(kn, cos_i, slo_i, shi_i, ROT_DIM_IDX // 2)
    ki_ref[0] = kn[:, :IDX_DIM].astype(ki_ref.dtype)
    w_scale = IDX_HEADS ** -0.5 * IDX_DIM ** -0.5
    wit_ref[0] = kw.T[IDX_DIM:IDX_DIM + IDX_HEADS, :] * w_scale


def _in_projection(x, pos_t, g, wa, wb, wi, inv_col, lng, lnb, tm=256):
    B, S, _ = x.shape
    row = lambda w: pl.BlockSpec((1, tm, w), lambda b, i: (b, i, 0))
    col = lambda r: pl.BlockSpec((1, r, tm), lambda b, i: (b, 0, i))
    full = lambda a: pl.BlockSpec(a.shape, lambda b, i: (0,) * a.ndim,
                                  pipeline_mode=pl.Buffered(1))
    sds = jax.ShapeDtypeStruct
    out_shape = [sds((B, S, 512), BF16), sds((B, S, 512), BF16), sds((B, 512, S), BF16),
                 sds((B, S, 512), BF16), sds((B, S, 512), BF16), sds((B, 512, S), BF16),
                 sds((B, S, IDX_QW), BF16), sds((B, S, IDX_DIM), BF16), sds((B, IDX_HEADS, S), F32)]
    return pl.pallas_call(
        _inproj_kernel,
        grid=(B, S // tm),
        in_specs=[row(D_MODEL), col(1), full(g), full(wa), full(wb), full(wi),
                  full(inv_col), full(lng), full(lnb)],
        out_specs=[row(512), row(512), col(512), row(512), row(512), col(512),
                   row(IDX_QW), row(IDX_DIM), col(IDX_HEADS)],
        out_shape=out_shape,
        compiler_params=_cparams(("parallel", "parallel")),
        name="in_projection",
    )(x, pos_t, g, wa, wb, wi, inv_col, lng, lnb)


def _init_streams(q_ref, qpad_ref, m_ref, acc_ref):
    lane = lax.broadcasted_iota(jnp.int32, (1, LANES), 1)
    for s in range(N_STREAMS):
        blk = q_ref[0, :, (s // 2) * LANES:(s // 2 + 1) * LANES]
        mine = (lane >= HEAD_DIM) if s % 2 else (lane < HEAD_DIM)
        qpad_ref[s] = jnp.where(mine, blk, jnp.zeros_like(blk))
    m_ref[...] = jnp.full(m_ref.shape, MASKED, F32)
    acc_ref[...] = jnp.zeros(acc_ref.shape, F32)


def _attend_chunk(k_ref, vt_ref, qpad_ref, m_ref, acc_ref, st_ref, off, bias,
                  *, tkv, dv, v_head, den_on_mxu):
    col_max = []
    for s in range(N_STREAMS):
        kp = k_ref[0, pl.ds(off, tkv), (s // 2) * LANES:(s // 2 + 1) * LANES]
        st = lax.dot_general(kp, qpad_ref[s], NT_DIMS, preferred_element_type=F32)
        if bias is not None:
            st = st + bias
        st_ref[s] = st
        col_max.append(jnp.max(st, axis=0, keepdims=True))
    for s in range(N_STREAMS):
        m_old = m_ref[s]
        m_new = jnp.maximum(m_old, col_max[s])
        alpha = jnp.exp2(m_old - m_new)
        r0 = v_head(s) * dv
        v_c = vt_ref[0, r0:r0 + dv, pl.ds(off, tkv)]
        if den_on_mxu:
            p = jnp.exp2((st_ref[s] - m_new).astype(BF16))
            v_aug = jnp.concatenate([v_c, jnp.ones((DEN_ROWS, tkv), BF16)], axis=0)
            acc_ref[s] = alpha * acc_ref[s] + jnp.dot(v_aug, p, preferred_element_type=F32)
        else:
            p = jnp.exp2(st_ref[s] - m_new)
            acc_ref[s, 0:dv, :] = alpha * acc_ref[s, 0:dv, :] + jnp.dot(
                v_c, p.astype(BF16), preferred_element_type=F32)
            acc_ref[s, dv:dv + 1, :] = (alpha * acc_ref[s, dv:dv + 1, :]
                                        + jnp.sum(p, axis=0, keepdims=True))
        m_ref[s] = m_new


def _normalised(acc_ref, s, dv):
    return acc_ref[s, 0:dv, :] / acc_ref[s, dv:dv + 1, :]


def _dsa_kernel(qi_ref, wt_ref, kidx_ref, q_ref, k_ref, vt_ref, o_ref,
                score_ref, thr_ref, qpad_ref, m_ref, acc_ref, ot_ref, st_ref,
                *, tq, tk, ktop):
    qi = pl.program_id(1)
    nck = ((qi + 1) * tq + tk - 1) // tk
    nsub = tk // ROWS

    qx = qi_ref[0]
    q2 = jnp.concatenate([qx[:, h * IDX_DIM:(h + 1) * IDX_DIM] for h in range(IDX_HEADS)], axis=0)
    wt = wt_ref[0]
    qpos = qi * tq + lax.broadcasted_iota(jnp.int32, (1, tq), 1)
    krow = lax.broadcasted_iota(jnp.int32, (tk, tq), 0)

    def score_body(c, carry):
        mx, mn, c0, cp = carry
        off = pl.multiple_of(c * tk, tk)
        kc = kidx_ref[0, pl.ds(off, tk), :]
        logits = lax.dot_general(kc, q2, NT_DIMS, preferred_element_type=F32)
        sc = jnp.maximum(logits[:, :tq], 0.0) * wt[0:1, :]
        for h in range(1, IDX_HEADS):
            sc = sc + jnp.maximum(logits[:, h * tq:(h + 1) * tq], 0.0) * wt[h:h + 1, :]
        causal = (off + krow) <= qpos
        s_hi = jnp.where(causal, sc, -jnp.inf)
        s_lo = jnp.where(causal, sc, jnp.inf)
        score_ref[pl.ds(off, tk), :] = s_hi
        for j in range(nsub):
            b_hi = s_hi[j * ROWS:(j + 1) * ROWS]
            mx = jnp.maximum(mx, b_hi)
            mn = jnp.minimum(mn, s_lo[j * ROWS:(j + 1) * ROWS])
            c0 = c0 + jnp.where(b_hi >= 0.0, 1.0, 0.0)
            cp = cp + jnp.where(b_hi > 0.0, 1.0, 0.0)
        return mx, mn, c0, cp

    zero = jnp.zeros((ROWS, tq), F32)
    mx, mn, c0, cp = lax.fori_loop(0, nck, score_body,
                                   (zero - jnp.inf, zero + jnp.inf, zero, zero))
    hi0 = jnp.max(mx, axis=0, keepdims=True)
    lo0 = jnp.min(mn, axis=0, keepdims=True)
    n_ge0 = jnp.sum(c0, axis=0, keepdims=True)
    n_gt0 = jnp.sum(cp, axis=0, keepdims=True)

    def count(pred):
        def body(c, acc):
            off = pl.multiple_of(c * tk, tk)
            for j in range(nsub):
                x = score_ref[pl.ds(pl.multiple_of(off + j * ROWS, ROWS), ROWS), :]
                acc = acc + jnp.where(pred(x), 1.0, 0.0)
            return acc
        return jnp.sum(lax.fori_loop(0, nck, body, zero), axis=0, keepdims=True)

    n_valid = (qpos + 1).astype(F32)
    kk = jnp.minimum(n_valid, float(ktop))

    all_in = n_valid <= kk
    above = jnp.logical_and(jnp.logical_not(all_in), n_gt0 >= kk)
    below = jnp.logical_and(jnp.logical_not(all_in), n_ge0 < kk)
    lo = jnp.where(all_in, lo0, jnp.where(below, lo0, 0.0))
    clo = jnp.where(all_in, n_valid, jnp.where(below, n_valid, n_ge0))
    hi = jnp.where(below, 0.0, hi0)
    fin = jnp.where(jnp.logical_or(above, below), 0.0, 1.0)

    def bis_cond(st):
        it, lo, hi, clo, fin = st
        return jnp.logical_and(it < MAX_BISECT, jnp.min(fin) < 0.5)

    def bis_body(st):
        it, lo, hi, clo, fin = st
        mid0 = 0.5 * lo + 0.5 * hi
        collapsed = jnp.logical_or(mid0 <= lo, mid0 >= hi)
        mid = jnp.where(collapsed, hi, mid0)
        c = count(lambda x: x >= mid)
        ge = c >= kk
        live = fin < 0.5
        up = jnp.logical_and(live, ge)
        dn = jnp.logical_and(live, jnp.logical_not(ge))
        lo = jnp.where(up, mid, lo)
        clo = jnp.where(up, c, clo)
        hi = jnp.where(dn, mid, hi)
        done = jnp.logical_or(c == kk, collapsed)
        fin = jnp.where(jnp.logical_and(live, done), 1.0, fin)
        return it + 1, lo, hi, clo, fin

    _, thr, _, clo, _ = lax.while_loop(bis_cond, bis_body, (jnp.int32(0), lo, hi, clo, fin))
    thr_ref[...] = thr
    has_tie = jnp.max(jnp.where(clo > kk, 1.0, 0.0)) > 0.5

    @pl.when(has_tie)
    def _():
        n_gt = count(lambda x: x > thr)
        need = kk - n_gt
        lower = (lax.broadcasted_iota(jnp.int32, (tk, tk), 0)
                 >= lax.broadcasted_iota(jnp.int32, (tk, tk), 1))
        lower = jnp.where(lower, 1.0, 0.0).astype(BF16)

        def body(c, seen):
            off = pl.multiple_of(c * tk, tk)
            x = score_ref[pl.ds(off, tk), :]
            eq = jnp.where(x == thr, 1.0, 0.0)
            rank = jnp.dot(lower, eq.astype(BF16), preferred_element_type=F32) + seen
            keep = jnp.where(x > thr, 1.0, jnp.where(rank <= need, eq, 0.0))
            score_ref[pl.ds(off, tk), :] = jnp.where(keep > 0.5, 1.0, -1.0)
            return seen + jnp.sum(eq, axis=0, keepdims=True)
        lax.fori_loop(0, nck, body, jnp.zeros((1, tq), F32))
        thr_ref[...] = jnp.zeros((1, tq), F32)

    _init_streams(q_ref, qpad_ref, m_ref, acc_ref)
    thr_sel = thr_ref[...]

    def att_body(c, _):
        off = pl.multiple_of(c * tk, tk)
        bias = jnp.where(score_ref[pl.ds(off, tk), :] >= thr_sel, 0.0, MASKED)
        _attend_chunk(k_ref, vt_ref, qpad_ref, m_ref, acc_ref, st_ref, off, bias,
                      tkv=tk, dv=HEAD_DIM, v_head=lambda s: s, den_on_mxu=True)
        return 0
    lax.fori_loop(0, nck, att_body, 0)

    for s in range(N_STREAMS):
        ot_ref[s * HEAD_DIM:(s + 1) * HEAD_DIM, :] = _normalised(acc_ref, s, HEAD_DIM)
    o_ref[0] = ot_ref[...].T.astype(o_ref.dtype)


def _resident(shape):
    return pl.BlockSpec(shape, lambda b, i: (b, 0, 0), pipeline_mode=pl.Buffered(1))


def _dsa_branch(q_i, w_it, k_i, q, k, vt, tq=256, tk=512):
    B, S, W = q.shape
    ktop = min(TOPK_MAX, S // 4)
    kern = functools.partial(_dsa_kernel, tq=tq, tk=tk, ktop=ktop)
    return pl.pallas_call(
        kern,
        grid=(B, S // tq),
        in_specs=[pl.BlockSpec((1, tq, IDX_QW), lambda b, i: (b, i, 0)),
                  pl.BlockSpec((1, IDX_HEADS, tq), lambda b, i: (b, 0, i)),
                  _resident((1, S, IDX_DIM)),
                  pl.BlockSpec((1, tq, W), lambda b, i: (b, i, 0)),
                  _resident((1, S, W)), _resident((1, W, S))],
        out_specs=pl.BlockSpec((1, tq, W), lambda b, i: (b, i, 0)),
        out_shape=jax.ShapeDtypeStruct((B, S, W), BF16),
        scratch_shapes=[pltpu.VMEM((S, tq), F32), pltpu.VMEM((1, tq), F32),
                        pltpu.VMEM((N_STREAMS, tq, LANES), BF16),
                        pltpu.VMEM((N_STREAMS, 1, tq), F32),
                        pltpu.VMEM((N_STREAMS, HEAD_DIM + DEN_ROWS, tq), F32),
                        pltpu.VMEM((W, tq), F32),
                        pltpu.VMEM((N_STREAMS, tk, tq), F32)],
        compiler_params=_cparams(("parallel", "arbitrary")),
        name="dsa_branch",
    )(q_i, w_it, k_i, q, k, vt)


def _diff_kernel(q_ref, k_ref, vt_ref, lq1_ref, lk1_ref, lq2_ref, lk2_ref, g_ref, o_ref,
                 qpad_ref, m_ref, acc_ref, ot_ref, st_ref, *, tq, tkv, lam_init):
    qi = pl.program_id(1)
    nck = ((qi + 1) * tq + tkv - 1) // tkv
    nfull = (qi * tq + 1) // tkv
    lam = (jnp.exp(jnp.sum(lq1_ref[...] * lk1_ref[...], axis=-1, keepdims=True))
           - jnp.exp(jnp.sum(lq2_ref[...] * lk2_ref[...], axis=-1, keepdims=True)) + lam_init)
    _init_streams(q_ref, qpad_ref, m_ref, acc_ref)
    attend = functools.partial(_attend_chunk, k_ref, vt_ref, qpad_ref, m_ref, acc_ref,
                               st_ref, tkv=tkv, dv=DIFF_VDIM, v_head=lambda s: s // 2,
                               den_on_mxu=False)

    def full_body(c, _):
        attend(pl.multiple_of(c * tkv, tkv), None)
        return 0
    lax.fori_loop(0, nfull, full_body, 0)

    qpos = qi * tq + lax.broadcasted_iota(jnp.int32, (1, tq), 1)
    krow = lax.broadcasted_iota(jnp.int32, (tkv, tq), 0)

    def diag_body(c, _):
        off = pl.multiple_of(c * tkv, tkv)
        attend(off, jnp.where((off + krow) <= qpos, 0.0, MASKED))
        return 0
    lax.fori_loop(nfull, nck, diag_body, 0)

    for h in range(DIFF_HEADS):
        o1 = _normalised(acc_ref, 2 * h, DIFF_VDIM)
        o2 = _normalised(acc_ref, 2 * h + 1, DIFF_VDIM)
        ot_ref[h * DIFF_VDIM:(h + 1) * DIFF_VDIM, :] = o1 - lam * o2
    o = ot_ref[...].T
    for h in range(DIFF_HEADS):
        blk = _rmsnorm(o[:, h * DIFF_VDIM:(h + 1) * DIFF_VDIM], g_ref[...], DIFF_SUBLN_EPS)
        o_ref[0, :, h * DIFF_VDIM:(h + 1) * DIFF_VDIM] = (blk * (1.0 - lam_init)).astype(o_ref.dtype)


def _diff_branch(q, k, vt, lq1, lk1, lq2, lk2, g, lam_init, tq=256, tkv=512):
    B, S, W = q.shape
    kern = functools.partial(_diff_kernel, tq=tq, tkv=tkv, lam_init=lam_init)
    vec = lambda a: pl.BlockSpec(a.shape, lambda b, i: (0, 0))
    return pl.pallas_call(
        kern,
        grid=(B, S // tq),
        in_specs=[pl.BlockSpec((1, tq, W), lambda b, i: (b, i, 0)),
                  _resident((1, S, W)), _resident((1, W, S)),
                  vec(lq1), vec(lk1), vec(lq2), vec(lk2), vec(g)],
        out_specs=pl.BlockSpec((1, tq, W), lambda b, i: (b, i, 0)),
        out_shape=jax.ShapeDtypeStruct((B, S, W), BF16),
        scratch_shapes=[pltpu.VMEM((N_STREAMS, tq, LANES), BF16),
                        pltpu.VMEM((N_STREAMS, 1, tq), F32),
                        pltpu.VMEM((N_STREAMS, DIFF_VDIM + DEN_ROWS, tq), F32),
                        pltpu.VMEM((W, tq), F32),
                        pltpu.VMEM((N_STREAMS, tkv, tq), F32)],
        compiler_params=_cparams(("parallel", "arbitrary")),
        name="diff_branch",
    )(q, k, vt, lq1, lk1, lq2, lk2, g)


def _merge_kernel(x_ref, oa_ref, ob_ref, g_ref, wg_ref, gb_ref, wa_ref, wb_ref, wo_ref, y_ref):
    x = x_ref[...]
    h = _rmsnorm(x, g_ref[...], NORM_EPS).astype(BF16)
    gates = jax.nn.sigmoid(jnp.dot(h, wg_ref[...], preferred_element_type=F32) + gb_ref[...])
    pa = jnp.dot(oa_ref[...], wa_ref[...], preferred_element_type=F32)
    pb = jnp.dot(ob_ref[...], wb_ref[...], preferred_element_type=F32)
    merged = gates[:, :D_MODEL] * pa + gates[:, D_MODEL:] * pb
    y_ref[...] = x + jnp.dot(merged.astype(BF16), wo_ref[...], preferred_element_type=F32)


def _merge_project(x2, oa, ob, g, wg, gb, wa, wb, wo, tm=512):
    M = x2.shape[0]
    row = lambda w: pl.BlockSpec((tm, w), lambda i: (i, 0))
    full = lambda a: pl.BlockSpec(a.shape, lambda i: (0,) * a.ndim, pipeline_mode=pl.Buffered(1))
    return pl.pallas_call(
        _merge_kernel,
        grid=(M // tm,),
        in_specs=[row(D_MODEL), row(512), row(512), full(g), full(wg), full(gb),
                  full(wa), full(wb), full(wo)],
        out_specs=row(D_MODEL),
        out_shape=jax.ShapeDtypeStruct((M, D_MODEL), F32),
        compiler_params=_cparams(("parallel",)),
        name="merge_project",
    )(x2, oa, ob, g, wg, gb, wa, wb, wo)


def _ffn_kernel(x_ref, g_ref, wgate_ref, wup_ref, wdown_ref, gf_ref, y_ref, *, final_norm):
    x = x_ref[...]
    h = _rmsnorm(x, g_ref[...], NORM_EPS).astype(BF16)
    gate = jnp.dot(h, wgate_ref[...], preferred_element_type=F32)
    up = jnp.dot(h, wup_ref[...], preferred_element_type=F32)
    act = (gate * jax.nn.sigmoid(gate) * up).astype(BF16)
    y = x + jnp.dot(act, wdown_ref[...], preferred_element_type=F32)
    if final_norm:
        y = _rmsnorm(y, gf_ref[...], NORM_EPS)
    y_ref[...] = y


def _ffn(x2, g, wgate, wup, wdown, gf, final_norm, tm=512):
    M = x2.shape[0]
    row = lambda w: pl.BlockSpec((tm, w), lambda i: (i, 0))
    full = lambda a: pl.BlockSpec(a.shape, lambda i: (0,) * a.ndim, pipeline_mode=pl.Buffered(1))
    return pl.pallas_call(
        functools.partial(_ffn_kernel, final_norm=final_norm),
        grid=(M // tm,),
        in_specs=[row(D_MODEL), full(g), full(wgate), full(wup), full(wdown), full(gf)],
        out_specs=row(D_MODEL),
        out_shape=jax.ShapeDtypeStruct((M, D_MODEL), F32),
        compiler_params=_cparams(("parallel",)),
        name="swiglu_ffn",
    )(x2, g, wgate, wup, wdown, gf)


def _inv_freq(rot_dim):
    return jnp.power(jnp.float32(ROPE_THETA), -jnp.arange(0, rot_dim, 2, dtype=F32) / rot_dim)


def _inv_freq_column():
    inv = jnp.concatenate([_inv_freq(ROT_DIM_HEAD), _inv_freq(ROT_DIM_IDX)])
    return jnp.pad(inv, (0, TRIG_ROWS - inv.shape[0]))[:, None]


def _pad_lanes(v):
    return jnp.pad(v.astype(F32), (0, LANES - v.shape[0]))[None, :]


def kernel(x, positions, norm_mix_g, w_in, idx_k_norm_g, idx_k_norm_b, diff_lambda_q1,
           diff_lambda_k1, diff_lambda_q2, diff_lambda_k2, diff_subln_g, gate_b, w_branch_dsa,
           w_branch_diff, w_out, norm_ffn_g, w_ffn_in, w_ffn_out, norm_final_g):
    B, S, D = x.shape
    M = B * S
    depth = w_in.shape[0]
    if depth == 0:
        raise ValueError("depth must be positive")
    pos_t = positions.astype(F32)[:, None, :]
    inv_col = _inv_freq_column()
    row = lambda v: v.astype(F32)[None, :]

    n_idx = IDX_QW + IDX_DIM + IDX_HEADS
    o_a = 0
    o_i = 3 * DSA_WIDTH
    o_b = o_i + n_idx
    o_g = o_b + 2 * DIFF_QK_WIDTH + DIFF_WIDTH

    x2 = x.reshape(M, D)
    for l in range(depth):
        lam_init = 0.8 - 0.6 * math.exp(-0.3 * l)
        wl = w_in[l]
        wa = wl[:, o_a:o_a + 3 * DSA_WIDTH].astype(BF16)
        wi = jnp.pad(wl[:, o_i:o_i + n_idx], ((0, 0), (0, IDX_QW + LANES - n_idx))).astype(BF16)
        wb = wl[:, o_b:o_b + 3 * 512].astype(BF16)
        wg = wl[:, o_g:o_g + N_BRANCH * D_MODEL].astype(BF16)

        (q_a, k_a, vt_a, q_b, k_b, vt_b, q_i, k_i, w_it) = _in_projection(
            x2.reshape(B, S, D), pos_t, row(norm_mix_g[l]), wa, wb, wi, inv_col,
            _pad_lanes(idx_k_norm_g[l]), _pad_lanes(idx_k_norm_b[l]))

        o_dsa = _dsa_branch(q_i, w_it, k_i, q_a, k_a, vt_a)
        o_diff = _diff_branch(q_b, k_b, vt_b,
                              row(diff_lambda_q1[l]), row(diff_lambda_k1[l]),
                              row(diff_lambda_q2[l]), row(diff_lambda_k2[l]),
                              row(diff_subln_g[l]), lam_init)

        x2 = _merge_project(x2, o_dsa.reshape(M, DSA_WIDTH), o_diff.reshape(M, DIFF_WIDTH),
                            row(norm_mix_g[l]), wg, row(gate_b[l]),
                            w_branch_dsa[l].astype(BF16), w_branch_diff[l].astype(BF16),
                            w_out[l].astype(BF16))
        wf = w_ffn_in[l]
        x2 = _ffn(x2, row(norm_ffn_g[l]), wf[:, :D_FF].astype(BF16), wf[:, D_FF:].astype(BF16),
                  w_ffn_out[l].astype(BF16), row(norm_final_g), final_norm=(l == depth - 1))

    return x2.reshape(B, S, D)
```

```python
import functools
import math

import jax
import jax.numpy as jnp
from jax import lax
from jax.experimental import pallas as pl
from jax.experimental.pallas import tpu as pltpu

F32 = jnp.float32
BF16 = jnp.bfloat16

D_MODEL = 1024
HEAD_DIM = 64
DSA_HEADS = 8
DSA_WIDTH = DSA_HEADS * HEAD_DIM
IDX_HEADS = 8
IDX_DIM = 32
IDX_QW = IDX_HEADS * IDX_DIM
TOPK_MAX = 256
DIFF_HEADS = 4
DIFF_DIM = 64
DIFF_VDIM = 2 * DIFF_DIM
DIFF_QK_WIDTH = DIFF_HEADS * 2 * DIFF_DIM
DIFF_WIDTH = DIFF_HEADS * DIFF_VDIM
DIFF_SUBLN_EPS = 1e-5
N_BRANCH = 2
ROPE_THETA = 500000.0
ROT_DIM_HEAD = HEAD_DIM // 4
ROT_DIM_IDX = IDX_DIM // 4
FFN_MULT = 256
D_FF = -(-8 * D_MODEL // (3 * FFN_MULT)) * FFN_MULT
NORM_EPS = 1e-6

LANES = 128
VMEM_LIMIT = 56 * 1024 * 1024
MASKED = -1e30
MAX_BISECT = 512
LOG2E = 1.4426950408889634
N_STREAMS = 8
ROWS = 64
DEN_ROWS = 16
TRIG_ROWS = 16
TRIG_ONE = 12
TRIG_ZERO = TRIG_ROWS + 12

NT_DIMS = (((1,), (1,)), ((), ()))


def _cparams(sem, flags=None):
    return pltpu.CompilerParams(dimension_semantics=sem, vmem_limit_bytes=VMEM_LIMIT, flags=flags)


def _rmsnorm(x, g, eps):
    return x * lax.rsqrt(jnp.mean(x * x, axis=-1, keepdims=True) + eps) * g


def _rope_block(y, cos, sin_lo, sin_hi, half):
    return (y * cos + pltpu.roll(y, half, 1) * sin_hi
            + pltpu.roll(y, LANES - half, 1) * sin_lo)


def _trig_table(pos_t, inv_col):
    ang_t = inv_col * pos_t
    pad = jnp.zeros((LANES - 2 * TRIG_ROWS, ang_t.shape[1]), F32)
    return jnp.concatenate([jnp.cos(ang_t), jnp.sin(ang_t), pad], axis=0).T


def _rope_tables(tbl, first, period, half):
    d = lax.broadcasted_iota(jnp.int32, tbl.shape, 1) % period
    rot = d < 2 * half
    take = lambda idx: jnp.take_along_axis(tbl, idx, axis=1)
    cos = take(jnp.where(rot, first + d % half, TRIG_ONE))
    sin_lo = -take(jnp.where(d < half, TRIG_ROWS + first + d, TRIG_ZERO))
    sin_hi = take(jnp.where(rot & (d >= half), TRIG_ROWS + first + d % half, TRIG_ZERO))
    return cos, sin_lo, sin_hi


def _inproj_kernel(x_ref, post_ref, g_ref, wa_ref, wb_ref, wi_ref, invcol_ref,
                   lng_ref, lnb_ref,
                   qa_ref, ka_ref, vat_ref, qb_ref, kb_ref, vbt_ref, qi_ref, ki_ref, wit_ref):
    h = _rmsnorm(x_ref[0], g_ref[...], NORM_EPS).astype(BF16)
    tbl = _trig_table(post_ref[0], invcol_ref[...])
    cos_h, slo_h, shi_h = _rope_tables(tbl, 0, HEAD_DIM, ROT_DIM_HEAD // 2)
    cos_i, slo_i, shi_i = _rope_tables(tbl, ROT_DIM_HEAD // 2, IDX_DIM, ROT_DIM_IDX // 2)
    q_scale = HEAD_DIM ** -0.5 * LOG2E

    def branch(w_ref, q_ref, k_ref, vt_ref):
        for grp, (o_ref, scale) in enumerate(((q_ref, q_scale), (k_ref, 1.0))):
            y = jnp.dot(h, w_ref[:, grp * 512:(grp + 1) * 512], preferred_element_type=F32)
            for j in range(512 // LANES):
                blk = _rope_block(y[:, j * LANES:(j + 1) * LANES], cos_h, slo_h, shi_h,
                                  ROT_DIM_HEAD // 2)
                if scale != 1.0:
                    blk = blk * scale
                o_ref[0, :, j * LANES:(j + 1) * LANES] = blk.astype(o_ref.dtype)
        v = jnp.dot(h, w_ref[:, 1024:1536], preferred_element_type=F32)
        vt_ref[0] = v.T.astype(vt_ref.dtype)

    branch(wa_ref, qa_ref, ka_ref, vat_ref)
    branch(wb_ref, qb_ref, kb_ref, vbt_ref)

    r = jnp.dot(h, wi_ref[...], preferred_element_type=F32)
    for j in range(IDX_QW // LANES):
        blk = _rope_block(r[:, j * LANES:(j + 1) * LANES], cos_i, slo_i, shi_i, ROT_DIM_IDX // 2)
        qi_ref[0, :, j * LANES:(j + 1) * LANES] = blk.astype(qi_ref.dtype)
    kw = r[:, IDX_QW:IDX_QW + LANES]
    lane = lax.broadcasted_iota(jnp.int32, (1, LANES), 1)
    is_k = lane < IDX_DIM
    mu = jnp.sum(jnp.where(is_k, kw, 0.0), axis=-1, keepdims=True) * (1.0 / IDX_DIM)
    xc = jnp.where(is_k, kw - mu, 0.0)
    var = jnp.sum(xc * xc, axis=-1, keepdims=True) * (1.0 / IDX_DIM)
    kn = xc * lax.rsqrt(var + NORM_EPS) * lng_ref[...] + lnb_ref[...]
    kn = _rope_block(kn, cos_i, slo_i, shi_i, ROT_DIM_IDX // 2)
    ki_ref[0] = kn[:, :IDX_DIM].astype(ki_ref.dtype)
    w_scale = IDX_HEADS ** -0.5 * IDX_DIM ** -0.5
    wit_ref[0] = kw.T[IDX_DIM:IDX_DIM + IDX_HEADS, :] * w_scale


def _in_projection(x, pos_t, g, wa, wb, wi, inv_col, lng, lnb, tm=256):
    B, S, _ = x.shape
    row = lambda w: pl.BlockSpec((1, tm, w), lambda b, i: (b, i, 0))
    col = lambda r: pl.BlockSpec((1, r, tm), lambda b, i: (b, 0, i))
    full = lambda a: pl.BlockSpec(a.shape, lambda b, i: (0,) * a.ndim,
                                  pipeline_mode=pl.Buffered(1))
    sds = jax.ShapeDtypeStruct
    out_shape = [sds((B, S, 512), BF16), sds((B, S, 512), BF16), sds((B, 512, S), BF16),
                 sds((B, S, 512), BF16), sds((B, S, 512), BF16), sds((B, 512, S), BF16),
                 sds((B, S, IDX_QW), BF16), sds((B, S, IDX_DIM), BF16), sds((B, IDX_HEADS, S), F32)]
    return pl.pallas_call(
        _inproj_kernel,
        grid=(B, S // tm),
        in_specs=[row(D_MODEL), col(1), full(g), full(wa), full(wb), full(wi),
                  full(inv_col), full(lng), full(lnb)],
        out_specs=[row(512), row(512), col(512), row(512), row(512), col(512),
                   row(IDX_QW), row(IDX_DIM), col(IDX_HEADS)],
        out_shape=out_shape,
        compiler_params=_cparams(("parallel", "parallel")),
        name="in_projection",
    )(x, pos_t, g, wa, wb, wi, inv_col, lng, lnb)


def _init_streams(q_ref, qpad_ref, m_ref, acc_ref):
    lane = lax.broadcasted_iota(jnp.int32, (1, LANES), 1)
    for s in range(N_STREAMS):
        blk = q_ref[0, :, (s // 2) * LANES:(s // 2 + 1) * LANES]
        mine = (lane >= HEAD_DIM) if s % 2 else (lane < HEAD_DIM)
        qpad_ref[s] = jnp.where(mine, blk, jnp.zeros_like(blk))
    m_ref[...] = jnp.full(m_ref.shape, MASKED, F32)
    acc_ref[...] = jnp.zeros(acc_ref.shape, F32)


def _attend_chunk(k_ref, vt_ref, qpad_ref, m_ref, acc_ref, st_ref, off, bias,
                  *, tkv, dv, v_head, den_on_mxu):
    col_max = []
    for s in range(N_STREAMS):
        kp = k_ref[0, pl.ds(off, tkv), (s // 2) * LANES:(s // 2 + 1) * LANES]
        st = lax.dot_general(kp, qpad_ref[s], NT_DIMS, preferred_element_type=F32)
        if bias is not None:
            st = st + bias
        st_ref[s] = st
        col_max.append(jnp.max(st, axis=0, keepdims=True))
    for s in range(N_STREAMS):
        m_old = m_ref[s]
        m_new = jnp.maximum(m_old, col_max[s])
        alpha = jnp.exp2(m_old - m_new)
        r0 = v_head(s) * dv
        v_c = vt_ref[0, r0:r0 + dv, pl.ds(off, tkv)]
        if den_on_mxu:
            p = jnp.exp2((st_ref[s] - m_new).astype(BF16))
            v_aug = jnp.concatenate([v_c, jnp.ones((DEN_ROWS, tkv), BF16)], axis=0)
            acc_ref[s] = alpha * acc_ref[s] + jnp.dot(v_aug, p, preferred_element_type=F32)
        else:
            p = jnp.exp2(st_ref[s] - m_new)
            acc_ref[s, 0:dv, :] = alpha * acc_ref[s, 0:dv, :] + jnp.dot(
                v_c, p.astype(BF16), preferred_element_type=F32)
            acc_ref[s, dv:dv + 1, :] = (alpha * acc_ref[s, dv:dv + 1, :]
                                        + jnp.sum(p, axis=0, keepdims=True))
        m_ref[s] = m_new


def _normalised(acc_ref, s, dv):
    return acc_ref[s, 0:dv, :] / acc_ref[s, dv:dv + 1, :]


def _indexer_operands(qi_ref, wt_ref):
    qx = qi_ref[0]
    q2 = jnp.concatenate([qx[:, h * IDX_DIM:(h + 1) * IDX_DIM] for h in range(IDX_HEADS)], axis=0)
    return q2, wt_ref[0]


def _score_chunk(kidx_ref, score_ref, slot, q2, wt, qpos, off, carry, *, tq, tk):
    mx, mn, c0, cp = carry
    kc = kidx_ref[0, pl.ds(off, tk), :]
    logits = lax.dot_general(kc, q2, NT_DIMS, preferred_element_type=F32)
    sc = jnp.maximum(logits[:, :tq], 0.0) * wt[0:1, :]
    for h in range(1, IDX_HEADS):
        sc = sc + jnp.maximum(logits[:, h * tq:(h + 1) * tq], 0.0) * wt[h:h + 1, :]
    causal = (off + lax.broadcasted_iota(jnp.int32, (tk, tq), 0)) <= qpos
    s_hi = jnp.where(causal, sc, -jnp.inf)
    s_lo = jnp.where(causal, sc, jnp.inf)
    score_ref[slot, pl.ds(off, tk), :] = s_hi
    for j in range(tk // ROWS):
        b_hi = s_hi[j * ROWS:(j + 1) * ROWS]
        mx = jnp.maximum(mx, b_hi)
        mn = jnp.minimum(mn, s_lo[j * ROWS:(j + 1) * ROWS])
        c0 = c0 + jnp.where(b_hi >= 0.0, 1.0, 0.0)
        cp = cp + jnp.where(b_hi > 0.0, 1.0, 0.0)
    return mx, mn, c0, cp


def _score_carry(tq):
    zero = jnp.zeros((ROWS, tq), F32)
    return zero - jnp.inf, zero + jnp.inf, zero, zero


def _select_threshold(score_ref, thr_ref, slot, nck, qpos, carry, *, tq, tk, ktop):
    mx, mn, c0, cp = carry
    hi0 = jnp.max(mx, axis=0, keepdims=True)
    lo0 = jnp.min(mn, axis=0, keepdims=True)
    n_ge0 = jnp.sum(c0, axis=0, keepdims=True)
    n_gt0 = jnp.sum(cp, axis=0, keepdims=True)
    zero = jnp.zeros((ROWS, tq), F32)

    def count(pred):
        def body(c, acc):
            off = pl.multiple_of(c * tk, tk)
            for j in range(tk // ROWS):
                x = score_ref[slot, pl.ds(pl.multiple_of(off + j * ROWS, ROWS), ROWS), :]
                acc = acc + jnp.where(pred(x), 1.0, 0.0)
            return acc
        return jnp.sum(lax.fori_loop(0, nck, body, zero), axis=0, keepdims=True)

    n_valid = (qpos + 1).astype(F32)
    kk = jnp.minimum(n_valid, float(ktop))

    all_in = n_valid <= kk
    above = jnp.logical_and(jnp.logical_not(all_in), n_gt0 >= kk)
    below = jnp.logical_and(jnp.logical_not(all_in), n_ge0 < kk)
    lo = jnp.where(all_in, lo0, jnp.where(below, lo0, 0.0))
    clo = jnp.where(all_in, n_valid, jnp.where(below, n_valid, n_ge0))
    hi = jnp.where(below, 0.0, hi0)
    fin = jnp.where(jnp.logical_or(above, below), 0.0, 1.0)

    def bis_cond(st):
        it, lo, hi, clo, fin = st
        return jnp.logical_and(it < MAX_BISECT, jnp.min(fin) < 0.5)

    def bis_body(st):
        it, lo, hi, clo, fin = st
        mid0 = 0.5 * lo + 0.5 * hi
        collapsed = jnp.logical_or(mid0 <= lo, mid0 >= hi)
        mid = jnp.where(collapsed, hi, mid0)
        c = count(lambda x: x >= mid)
        ge = c >= kk
        live = fin < 0.5
        up = jnp.logical_and(live, ge)
        dn = jnp.logical_and(live, jnp.logical_not(ge))
        lo = jnp.where(up, mid, lo)
        clo = jnp.where(up, c, clo)
        hi = jnp.where(dn, mid, hi)
        done = jnp.logical_or(c == kk, collapsed)
        fin = jnp.where(jnp.logical_and(live, done), 1.0, fin)
        return it + 1, lo, hi, clo, fin

    _, thr, _, clo, _ = lax.while_loop(bis_cond, bis_body, (jnp.int32(0), lo, hi, clo, fin))
    thr_ref[slot] = thr
    has_tie = jnp.max(jnp.where(clo > kk, 1.0, 0.0)) > 0.5

    @pl.when(has_tie)
    def _():
        n_gt = count(lambda x: x > thr)
        need = kk - n_gt
        lower = (lax.broadcasted_iota(jnp.int32, (tk, tk), 0)
                 >= lax.broadcasted_iota(jnp.int32, (tk, tk), 1))
        lower = jnp.where(lower, 1.0, 0.0).astype(BF16)

        def body(c, seen):
            off = pl.multiple_of(c * tk, tk)
            x = score_ref[slot, pl.ds(off, tk), :]
            eq = jnp.where(x == thr, 1.0, 0.0)
            rank = jnp.dot(lower, eq.astype(BF16), preferred_element_type=F32) + seen
            keep = jnp.where(x > thr, 1.0, jnp.where(rank <= need, eq, 0.0))
            score_ref[slot, pl.ds(off, tk), :] = jnp.where(keep > 0.5, 1.0, -1.0)
            return seen + jnp.sum(eq, axis=0, keepdims=True)
        lax.fori_loop(0, nck, body, jnp.zeros((1, tq), F32))
        thr_ref[slot] = jnp.zeros((1, tq), F32)


def _dsa_kernel(qi_ref, wt_ref, qin_ref, wtn_ref, kidx_ref, q_ref, k_ref, vt_ref, o_ref,
                score_ref, thr_ref, qpad_ref, m_ref, acc_ref, ot_ref, st_ref,
                *, tq, tk, ktop):
    qi = pl.program_id(1)
    slot = qi % 2
    nslot = 1 - slot
    chunks = lambda blk: ((blk + 1) * tq + tk - 1) // tk
    nck = chunks(qi)
    lanes_q = lax.broadcasted_iota(jnp.int32, (1, tq), 1)
    select = functools.partial(_select_threshold, score_ref, thr_ref, tq=tq, tk=tk, ktop=ktop)

    @pl.when(qi == 0)
    def _():
        q2, wt = _indexer_operands(qi_ref, wt_ref)

        def body(c, carry):
            return _score_chunk(kidx_ref, score_ref, 0, q2, wt, lanes_q,
                                pl.multiple_of(c * tk, tk), carry, tq=tq, tk=tk)
        select(0, nck, lanes_q, lax.fori_loop(0, nck, body, _score_carry(tq)))

    _init_streams(q_ref, qpad_ref, m_ref, acc_ref)
    thr_sel = thr_ref[slot]

    def attend(off):
        bias = jnp.where(score_ref[slot, pl.ds(off, tk), :] >= thr_sel, 0.0, MASKED)
        _attend_chunk(k_ref, vt_ref, qpad_ref, m_ref, acc_ref, st_ref, off, bias,
                      tkv=tk, dv=HEAD_DIM, v_head=lambda s: s, den_on_mxu=True)

    has_next = qi + 1 < pl.num_programs(1)

    @pl.when(has_next)
    def _():
        q2, wt = _indexer_operands(qin_ref, wtn_ref)
        qpos = (qi + 1) * tq + lanes_q
        score = functools.partial(_score_chunk, kidx_ref, score_ref, nslot, q2, wt, qpos,
                                  tq=tq, tk=tk)

        def paired(c, carry):
            off = pl.multiple_of(c * tk, tk)
            carry = score(off, carry)
            attend(off)
            return carry
        carry = lax.fori_loop(0, nck, paired, _score_carry(tq))
        nck_next = chunks(qi + 1)
        carry = lax.fori_loop(nck, nck_next,
                              lambda c, carry: score(pl.multiple_of(c * tk, tk), carry), carry)
        select(nslot, nck_next, qpos, carry)

    @pl.when(jnp.logical_not(has_next))
    def _():
        def body(c, _):
            attend(pl.multiple_of(c * tk, tk))
            return 0
        lax.fori_loop(0, nck, body, 0)

    for s in range(N_STREAMS):
        ot_ref[s * HEAD_DIM:(s + 1) * HEAD_DIM, :] = _normalised(acc_ref, s, HEAD_DIM)
    o_ref[0] = ot_ref[...].T.astype(o_ref.dtype)


def _resident(shape):
    return pl.BlockSpec(shape, lambda b, i: (b, 0, 0), pipeline_mode=pl.Buffered(1))


def _dsa_branch(q_i, w_it, k_i, q, k, vt, tq=256, tk=512):
    B, S, W = q.shape
    nq = S // tq
    ktop = min(TOPK_MAX, S // 4)
    kern = functools.partial(_dsa_kernel, tq=tq, tk=tk, ktop=ktop)
    nxt = lambda i: jnp.minimum(i + 1, nq - 1)
    return pl.pallas_call(
        kern,
        grid=(B, nq),
        in_specs=[pl.BlockSpec((1, tq, IDX_QW), lambda b, i: (b, i, 0)),
                  pl.BlockSpec((1, IDX_HEADS, tq), lambda b, i: (b, 0, i)),
                  pl.BlockSpec((1, tq, IDX_QW), lambda b, i: (b, nxt(i), 0)),
                  pl.BlockSpec((1, IDX_HEADS, tq), lambda b, i: (b, 0, nxt(i))),
                  _resident((1, S, IDX_DIM)),
                  pl.BlockSpec((1, tq, W), lambda b, i: (b, i, 0)),
                  _resident((1, S, W)), _resident((1, W, S))],
        out_specs=pl.BlockSpec((1, tq, W), lambda b, i: (b, i, 0)),
        out_shape=jax.ShapeDtypeStruct((B, S, W), BF16),
        scratch_shapes=[pltpu.VMEM((2, S, tq), F32), pltpu.VMEM((2, 1, tq), F32),
                        pltpu.VMEM((N_STREAMS, tq, LANES), BF16),
                        pltpu.VMEM((N_STREAMS, 1, tq), F32),
                        pltpu.VMEM((N_STREAMS, HEAD_DIM + DEN_ROWS, tq), F32),
                        pltpu.VMEM((W, tq), F32),
                        pltpu.VMEM((N_STREAMS, tk, tq), F32)],
        compiler_params=_cparams(("arbitrary", "arbitrary")),
        name="dsa_branch",
    )(q_i, w_it, q_i, w_it, k_i, q, k, vt)


def _diff_kernel(q_ref, k_ref, vt_ref, lq1_ref, lk1_ref, lq2_ref, lk2_ref, g_ref, o_ref,
                 qpad_ref, m_ref, acc_ref, ot_ref, st_ref, *, tq, tkv, lam_init):
    qi = pl.program_id(1)
    nck = ((qi + 1) * tq + tkv - 1) // tkv
    nfull = (qi * tq + 1) // tkv
    lam = (jnp.exp(jnp.sum(lq1_ref[...] * lk1_ref[...], axis=-1, keepdims=True))
           - jnp.exp(jnp.sum(lq2_ref[...] * lk2_ref[...], axis=-1, keepdims=True)) + lam_init)
    _init_streams(q_ref, qpad_ref, m_ref, acc_ref)
    attend = functools.partial(_attend_chunk, k_ref, vt_ref, qpad_ref, m_ref, acc_ref,
                               st_ref, tkv=tkv, dv=DIFF_VDIM, v_head=lambda s: s // 2,
                               den_on_mxu=False)

    def full_body(c, _):
        attend(pl.multiple_of(c * tkv, tkv), None)
        return 0
    lax.fori_loop(0, nfull, full_body, 0)

    qpos = qi * tq + lax.broadcasted_iota(jnp.int32, (1, tq), 1)
    krow = lax.broadcasted_iota(jnp.int32, (tkv, tq), 0)

    def diag_body(c, _):
        off = pl.multiple_of(c * tkv, tkv)
        attend(off, jnp.where((off + krow) <= qpos, 0.0, MASKED))
        return 0
    lax.fori_loop(nfull, nck, diag_body, 0)

    for h in range(DIFF_HEADS):
        o1 = _normalised(acc_ref, 2 * h, DIFF_VDIM)
        o2 = _normalised(acc_ref, 2 * h + 1, DIFF_VDIM)
        ot_ref[h * DIFF_VDIM:(h + 1) * DIFF_VDIM, :] = o1 - lam * o2
    o = ot_ref[...].T
    for h in range(DIFF_HEADS):
        blk = _rmsnorm(o[:, h * DIFF_VDIM:(h + 1) * DIFF_VDIM], g_ref[...], DIFF_SUBLN_EPS)
        o_ref[0, :, h * DIFF_VDIM:(h + 1) * DIFF_VDIM] = (blk * (1.0 - lam_init)).astype(o_ref.dtype)


def _diff_branch(q, k, vt, lq1, lk1, lq2, lk2, g, lam_init, tq=256, tkv=512):
    B, S, W = q.shape
    kern = functools.partial(_diff_kernel, tq=tq, tkv=tkv, lam_init=lam_init)
    vec = lambda a: pl.BlockSpec(a.shape, lambda b, i: (0, 0))
    return pl.pallas_call(
        kern,
        grid=(B, S // tq),
        in_specs=[pl.BlockSpec((1, tq, W), lambda b, i: (b, i, 0)),
                  _resident((1, S, W)), _resident((1, W, S)),
                  vec(lq1), vec(lk1), vec(lq2), vec(lk2), vec(g)],
        out_specs=pl.BlockSpec((1, tq, W), lambda b, i: (b, i, 0)),
        out_shape=jax.ShapeDtypeStruct((B, S, W), BF16),
        scratch_shapes=[pltpu.VMEM((N_STREAMS, tq, LANES), BF16),
                        pltpu.VMEM((N_STREAMS, 1, tq), F32),
                        pltpu.VMEM((N_STREAMS, DIFF_VDIM + DEN_ROWS, tq), F32),
                        pltpu.VMEM((W, tq), F32),
                        pltpu.VMEM((N_STREAMS, tkv, tq), F32)],
        compiler_params=_cparams(("parallel", "arbitrary")),
        name="diff_branch",
    )(q, k, vt, lq1, lk1, lq2, lk2, g)


def _merge_kernel(x_ref, oa_ref, ob_ref, g_ref, wg_ref, gb_ref, wa_ref, wb_ref, wo_ref, y_ref):
    x = x_ref[...]
    h = _rmsnorm(x, g_ref[...], NORM_EPS).astype(BF16)
    gates = jax.nn.sigmoid(jnp.dot(h, wg_ref[...], preferred_element_type=F32) + gb_ref[...])
    pa = jnp.dot(oa_ref[...], wa_ref[...], preferred_element_type=F32)
    pb = jnp.dot(ob_ref[...], wb_ref[...], preferred_element_type=F32)
    merged = gates[:, :D_MODEL] * pa + gates[:, D_MODEL:] * pb
    y_ref[...] = x + jnp.dot(merged.astype(BF16), wo_ref[...], preferred_element_type=F32)


def _merge_project(x2, oa, ob, g, wg, gb, wa, wb, wo, tm=512):
    M = x2.shape[0]
    row = lambda w: pl.BlockSpec((tm, w), lambda i: (i, 0))
    full = lambda a: pl.BlockSpec(a.shape, lambda i: (0,) * a.ndim, pipeline_mode=pl.Buffered(1))
    return pl.pallas_call(
        _merge_kernel,
        grid=(M // tm,),
        in_specs=[row(D_MODEL), row(512), row(512), full(g), full(wg), full(gb),
                  full(wa), full(wb), full(wo)],
        out_specs=row(D_MODEL),
        out_shape=jax.ShapeDtypeStruct((M, D_MODEL), F32),
        compiler_params=_cparams(("parallel",)),
        name="merge_project",
    )(x2, oa, ob, g, wg, gb, wa, wb, wo)


def _ffn_kernel(x_ref, g_ref, wgate_ref, wup_ref, wdown_ref, gf_ref, y_ref, *, final_norm):
    x = x_ref[...]
    h = _rmsnorm(x, g_ref[...], NORM_EPS).astype(BF16)
    gate = jnp.dot(h, wgate_ref[...], preferred_element_type=F32)
    up = jnp.dot(h, wup_ref[...], preferred_element_type=F32)
    act = (gate * jax.nn.sigmoid(gate) * up).astype(BF16)
    y = x + jnp.dot(act, wdown_ref[...], preferred_element_type=F32)
    if final_norm:
        y = _rmsnorm(y, gf_ref[...], NORM_EPS)
    y_ref[...] = y


def _ffn(x2, g, wgate, wup, wdown, gf, final_norm, tm=512):
    M = x2.shape[0]
    row = lambda w: pl.BlockSpec((tm, w), lambda i: (i, 0))
    full = lambda a: pl.BlockSpec(a.shape, lambda i: (0,) * a.ndim, pipeline_mode=pl.Buffered(1))
    return pl.pallas_call(
        functools.partial(_ffn_kernel, final_norm=final_norm),
        grid=(M // tm,),
        in_specs=[row(D_MODEL), full(g), full(wgate), full(wup), full(wdown), full(gf)],
        out_specs=row(D_MODEL),
        out_shape=jax.ShapeDtypeStruct((M, D_MODEL), F32),
        compiler_params=_cparams(("parallel",)),
        name="swiglu_ffn",
    )(x2, g, wgate, wup, wdown, gf)


def _inv_freq(rot_dim):
    return jnp.power(jnp.float32(ROPE_THETA), -jnp.arange(0, rot_dim, 2, dtype=F32) / rot_dim)


def _inv_freq_column():
    inv = jnp.concatenate([_inv_freq(ROT_DIM_HEAD), _inv_freq(ROT_DIM_IDX)])
    return jnp.pad(inv, (0, TRIG_ROWS - inv.shape[0]))[:, None]


def _pad_lanes(v):
    return jnp.pad(v.astype(F32), (0, LANES - v.shape[0]))[None, :]


def kernel(x, positions, norm_mix_g, w_in, idx_k_norm_g, idx_k_norm_b, diff_lambda_q1,
           diff_lambda_k1, diff_lambda_q2, diff_lambda_k2, diff_subln_g, gate_b, w_branch_dsa,
           w_branch_diff, w_out, norm_ffn_g, w_ffn_in, w_ffn_out, norm_final_g):
    B, S, D = x.shape
    M = B * S
    depth = w_in.shape[0]
    if depth == 0:
        raise ValueError("depth must be positive")
    pos_t = positions.astype(F32)[:, None, :]
    inv_col = _inv_freq_column()
    row = lambda v: v.astype(F32)[None, :]

    n_idx = IDX_QW + IDX_DIM + IDX_HEADS
    o_a = 0
    o_i = 3 * DSA_WIDTH
    o_b = o_i + n_idx
    o_g = o_b + 2 * DIFF_QK_WIDTH + DIFF_WIDTH

    x2 = x.reshape(M, D)
    for l in range(depth):
        lam_init = 0.8 - 0.6 * math.exp(-0.3 * l)
        wl = w_in[l]
        wa = wl[:, o_a:o_a + 3 * DSA_WIDTH].astype(BF16)
        wi = jnp.pad(wl[:, o_i:o_i + n_idx], ((0, 0), (0, IDX_QW + LANES - n_idx))).astype(BF16)
        wb = wl[:, o_b:o_b + 3 * 512].astype(BF16)
        wg = wl[:, o_g:o_g + N_BRANCH * D_MODEL].astype(BF16)

        (q_a, k_a, vt_a, q_b, k_b, vt_b, q_i, k_i, w_it) = _in_projection(
            x2.reshape(B, S, D), pos_t, row(norm_mix_g[l]), wa, wb, wi, inv_col,
            _pad_lanes(idx_k_norm_g[l]), _pad_lanes(idx_k_norm_b[l]))

        o_dsa = _dsa_branch(q_i, w_it, k_i, q_a, k_a, vt_a)
        o_diff = _diff_branch(q_b, k_b, vt_b,
                              row(diff_lambda_q1[l]), row(diff_lambda_k1[l]),
                              row(diff_lambda_q2[l]), row(diff_lambda_k2[l]),
                              row(diff_subln_g[l]), lam_init)

        x2 = _merge_project(x2, o_dsa.reshape(M, DSA_WIDTH), o_diff.reshape(M, DIFF_WIDTH),
                            row(norm_mix_g[l]), wg, row(gate_b[l]),
                            w_branch_dsa[l].astype(BF16), w_branch_diff[l].astype(BF16),
                            w_out[l].astype(BF16))
        wf = w_ffn_in[l]
        x2 = _ffn(x2, row(norm_ffn_g[l]), wf[:, :D_FF].astype(BF16), wf[:, D_FF:].astype(BF16),
                  w_ffn_out[l].astype(BF16), row(norm_final_g), final_norm=(l == depth - 1))

    return x2.reshape(B, S, D)
```

```python
import functools
import math

import jax
import jax.numpy as jnp
from jax import lax
from jax.experimental import pallas as pl
from jax.experimental.pallas import tpu as pltpu

F32 = jnp.float32
BF16 = jnp.bfloat16

D_MODEL = 1024
HEAD_DIM = 64
DSA_HEADS = 8
DSA_WIDTH = DSA_HEADS * HEAD_DIM
IDX_HEADS = 8
IDX_DIM = 32
IDX_QW = IDX_HEADS * IDX_DIM
TOPK_MAX = 256
DIFF_HEADS = 4
DIFF_DIM = 64
DIFF_VDIM = 2 * DIFF_DIM
DIFF_QK_WIDTH = DIFF_HEADS * 2 * DIFF_DIM
DIFF_WIDTH = DIFF_HEADS * DIFF_VDIM
DIFF_SUBLN_EPS = 1e-5
N_BRANCH = 2
ROPE_THETA = 500000.0
ROT_DIM_HEAD = HEAD_DIM // 4
ROT_DIM_IDX = IDX_DIM // 4
FFN_MULT = 256
D_FF = -(-8 * D_MODEL // (3 * FFN_MULT)) * FFN_MULT
NORM_EPS = 1e-6

LANES = 128
VMEM_LIMIT = 56 * 1024 * 1024
MASKED = -1e30
MAX_BISECT = 512
LOG2E = 1.4426950408889634
N_STREAMS = 8
ROWS = 64
DEN_ROWS = 16
TRIG_ROWS = 16
TRIG_ONE = 12
TRIG_ZERO = TRIG_ROWS + 12

NT_DIMS = (((1,), (1,)), ((), ()))


def _cparams(sem, flags=None):
    return pltpu.CompilerParams(dimension_semantics=sem, vmem_limit_bytes=VMEM_LIMIT, flags=flags)


def _rmsnorm(x, g, eps):
    return x * lax.rsqrt(jnp.mean(x * x, axis=-1, keepdims=True) + eps) * g


def _rope_block(y, cos, sin_lo, sin_hi, half):
    return (y * cos + pltpu.roll(y, half, 1) * sin_hi
            + pltpu.roll(y, LANES - half, 1) * sin_lo)


def _trig_table(pos_t, inv_col):
    ang_t = inv_col * pos_t
    pad = jnp.zeros((LANES - 2 * TRIG_ROWS, ang_t.shape[1]), F32)
    return jnp.concatenate([jnp.cos(ang_t), jnp.sin(ang_t), pad], axis=0).T


def _rope_tables(tbl, first, period, half):
    d = lax.broadcasted_iota(jnp.int32, tbl.shape, 1) % period
    rot = d < 2 * half
    take = lambda idx: jnp.take_along_axis(tbl, idx, axis=1)
    cos = take(jnp.where(rot, first + d % half, TRIG_ONE))
    sin_lo = -take(jnp.where(d < half, TRIG_ROWS + first + d, TRIG_ZERO))
    sin_hi = take(jnp.where(rot & (d >= half), TRIG_ROWS + first + d % half, TRIG_ZERO))
    return cos, sin_lo, sin_hi


def _rope_tables_paired(tbl):
    d = lax.broadcasted_iota(jnp.int32, tbl.shape, 1)
    x1 = d < ROT_DIM_HEAD
    rot = x1 | ((d >= LANES // 2) & (d < LANES // 2 + ROT_DIM_HEAD))
    f = d % (ROT_DIM_HEAD // 2)
    cos = jnp.take_along_axis(tbl, jnp.where(rot, f, TRIG_ONE), axis=1)
    sin = jnp.take_along_axis(tbl, jnp.where(rot, TRIG_ROWS + f, TRIG_ZERO), axis=1)
    return cos, jnp.where(x1, -sin, sin)


def _paired_head_lanes(second):
    lane = lax.broadcasted_iota(jnp.int32, (1, LANES), 1)
    half = ROT_DIM_HEAD // 2
    in_second = ((lane >= half) & (lane < 2 * half)) | (lane >= LANES // 2 + half)
    return in_second if second else jnp.logical_not(in_second)


def _paired_columns(width):
    half = ROT_DIM_HEAD // 2
    perm = list(range(LANES))
    perm[half:2 * half] = range(HEAD_DIM, HEAD_DIM + half)
    perm[LANES // 2:LANES // 2 + half] = range(half, 2 * half)
    return jnp.asarray([(c // LANES) * LANES + perm[c % LANES] for c in range(width)], jnp.int32)


def _inproj_kernel(x_ref, post_ref, g_ref, wa_ref, wavt_ref, wb_ref, wbvt_ref, wi_ref, invcol_ref,
                   lng_ref, lnb_ref,
                   qa_ref, ka_ref, vat_ref, qb_ref, kb_ref, vbt_ref, qi_ref, ki_ref, wit_ref):
    h = _rmsnorm(x_ref[0], g_ref[...], NORM_EPS).astype(BF16)
    tbl = _trig_table(post_ref[0], invcol_ref[...])
    cos_k, sin_k = _rope_tables_paired(tbl)
    cos_i, slo_i, shi_i = _rope_tables(tbl, ROT_DIM_HEAD // 2, IDX_DIM, ROT_DIM_IDX // 2)
    q_scale = HEAD_DIM ** -0.5 * LOG2E
    cos_q, sin_q = cos_k * q_scale, sin_k * q_scale

    def branch(w_ref, wvt_ref, q_ref, k_ref, vt_ref):
        for grp, (o_ref, cos, sin) in enumerate(((q_ref, cos_q, sin_q), (k_ref, cos_k, sin_k))):
            y = jnp.dot(h, w_ref[:, grp * 512:(grp + 1) * 512], preferred_element_type=F32)
            for j in range(512 // LANES):
                blk = y[:, j * LANES:(j + 1) * LANES]
                blk = blk * cos + pltpu.roll(blk, LANES // 2, 1) * sin
                o_ref[0, :, j * LANES:(j + 1) * LANES] = blk.astype(o_ref.dtype)
        vt_ref[0] = lax.dot_general(wvt_ref[...], h, NT_DIMS,
                                    preferred_element_type=F32).astype(vt_ref.dtype)

    branch(wa_ref, wavt_ref, qa_ref, ka_ref, vat_ref)
    branch(wb_ref, wbvt_ref, qb_ref, kb_ref, vbt_ref)

    r = jnp.dot(h, wi_ref[...], preferred_element_type=F32)
    for j in range(IDX_QW // LANES):
        blk = _rope_block(r[:, j * LANES:(j + 1) * LANES], cos_i, slo_i, shi_i, ROT_DIM_IDX // 2)
        qi_ref[0, :, j * LANES:(j + 1) * LANES] = blk.astype(qi_ref.dtype)
    kw = r[:, IDX_QW:IDX_QW + LANES]
    lane = lax.broadcasted_iota(jnp.int32, (1, LANES), 1)
    is_k = lane < IDX_DIM
    mu = jnp.sum(jnp.where(is_k, kw, 0.0), axis=-1, keepdims=True) * (1.0 / IDX_DIM)
    xc = jnp.where(is_k, kw - mu, 0.0)
    var = jnp.sum(xc * xc, axis=-1, keepdims=True) * (1.0 / IDX_DIM)
    kn = xc * lax.rsqrt(var + NORM_EPS) * lng_ref[...] + lnb_ref[...]
    kn = _rope_block(kn, cos_i, slo_i, shi_i, ROT_DIM_IDX // 2)
    ki_ref[0] = kn[:, :IDX_DIM].astype(ki_ref.dtype)
    w_scale = IDX_HEADS ** -0.5 * IDX_DIM ** -0.5
    wit_ref[0] = kw.T[IDX_DIM:IDX_DIM + IDX_HEADS, :] * w_scale


def _in_projection(x, pos_t, g, wa, wavt, wb, wbvt, wi, inv_col, lng, lnb, tm=256):
    B, S, _ = x.shape
    row = lambda w: pl.BlockSpec((1, tm, w), lambda b, i: (b, i, 0))
    col = lambda r: pl.BlockSpec((1, r, tm), lambda b, i: (b, 0, i))
    full = lambda a: pl.BlockSpec(a.shape, lambda b, i: (0,) * a.ndim,
                                  pipeline_mode=pl.Buffered(1))
    sds = jax.ShapeDtypeStruct
    out_shape = [sds((B, S, 512), BF16), sds((B, S, 512), BF16), sds((B, 512, S), BF16),
                 sds((B, S, 512), BF16), sds((B, S, 512), BF16), sds((B, 512, S), BF16),
                 sds((B, S, IDX_QW), BF16), sds((B, S, IDX_DIM), BF16), sds((B, IDX_HEADS, S), F32)]
    return pl.pallas_call(
        _inproj_kernel,
        grid=(B, S // tm),
        in_specs=[row(D_MODEL), col(1), full(g), full(wa), full(wavt), full(wb), full(wbvt),
                  full(wi), full(inv_col), full(lng), full(lnb)],
        out_specs=[row(512), row(512), col(512), row(512), row(512), col(512),
                   row(IDX_QW), row(IDX_DIM), col(IDX_HEADS)],
        out_shape=out_shape,
        compiler_params=_cparams(("parallel", "parallel")),
        name="in_projection",
    )(x, pos_t, g, wa, wavt, wb, wbvt, wi, inv_col, lng, lnb)


def _init_streams(q_ref, qpad_ref, m_ref, acc_ref):
    for s in range(N_STREAMS):
        blk = q_ref[0, :, (s // 2) * LANES:(s // 2 + 1) * LANES]
        qpad_ref[s] = jnp.where(_paired_head_lanes(s % 2 == 1), blk, jnp.zeros_like(blk))
    m_ref[...] = jnp.full(m_ref.shape, MASKED, F32)
    acc_ref[...] = jnp.zeros(acc_ref.shape, F32)


def _attend_chunk(k_ref, vt_ref, qpad_ref, m_ref, acc_ref, st_ref, off, bias,
                  *, tkv, dv, v_head, den_on_mxu):
    col_max = []
    for s in range(N_STREAMS):
        kp = k_ref[0, pl.ds(off, tkv), (s // 2) * LANES:(s // 2 + 1) * LANES]
        st = lax.dot_general(kp, qpad_ref[s], NT_DIMS, preferred_element_type=F32)
        if bias is not None:
            st = st + bias
        st_ref[s] = st
        col_max.append(jnp.max(st, axis=0, keepdims=True))
    for s in range(N_STREAMS):
        m_old = m_ref[s]
        m_new = jnp.maximum(m_old, col_max[s])
        alpha = jnp.exp2(m_old - m_new)
        r0 = v_head(s) * dv
        v_c = vt_ref[0, r0:r0 + dv, pl.ds(off, tkv)]
        if den_on_mxu:
            p = jnp.exp2((st_ref[s] - m_new).astype(BF16))
            v_aug = jnp.concatenate([v_c, jnp.ones((DEN_ROWS, tkv), BF16)], axis=0)
            acc_ref[s] = alpha * acc_ref[s] + jnp.dot(v_aug, p, preferred_element_type=F32)
        else:
            p = jnp.exp2(st_ref[s] - m_new)
            acc_ref[s, 0:dv, :] = alpha * acc_ref[s, 0:dv, :] + jnp.dot(
                v_c, p.astype(BF16), preferred_element_type=F32)
            acc_ref[s, dv:dv + 1, :] = (alpha * acc_ref[s, dv:dv + 1, :]
                                        + jnp.sum(p, axis=0, keepdims=True))
        m_ref[s] = m_new


def _normalised(acc_ref, s, dv):
    return acc_ref[s, 0:dv, :] / acc_ref[s, dv:dv + 1, :]


def _indexer_operands(qi_ref, wt_ref):
    qx = qi_ref[0]
    q2 = jnp.concatenate([qx[:, h * IDX_DIM:(h + 1) * IDX_DIM] for h in range(IDX_HEADS)], axis=0)
    return q2, wt_ref[0]


def _score_chunk(kidx_ref, score_ref, slot, q2, wt, qpos, off, carry, *, tq, tk):
    mx, mn, c0, cp = carry
    kc = kidx_ref[0, pl.ds(off, tk), :]
    logits = lax.dot_general(kc, q2, NT_DIMS, preferred_element_type=F32)
    sc = jnp.maximum(logits[:, :tq], 0.0) * wt[0:1, :]
    for h in range(1, IDX_HEADS):
        sc = sc + jnp.maximum(logits[:, h * tq:(h + 1) * tq], 0.0) * wt[h:h + 1, :]
    causal = (off + lax.broadcasted_iota(jnp.int32, (tk, tq), 0)) <= qpos
    s_hi = jnp.where(causal, sc, -jnp.inf)
    s_lo = jnp.where(causal, sc, jnp.inf)
    score_ref[slot, pl.ds(off, tk), :] = s_hi
    for j in range(tk // ROWS):
        b_hi = s_hi[j * ROWS:(j + 1) * ROWS]
        mx = jnp.maximum(mx, b_hi)
        mn = jnp.minimum(mn, s_lo[j * ROWS:(j + 1) * ROWS])
        c0 = c0 + jnp.where(b_hi >= 0.0, 1.0, 0.0)
        cp = cp + jnp.where(b_hi > 0.0, 1.0, 0.0)
    return mx, mn, c0, cp


def _score_carry(tq):
    zero = jnp.zeros((ROWS, tq), F32)
    return zero - jnp.inf, zero + jnp.inf, zero, zero


def _select_threshold(score_ref, thr_ref, slot, nck, qpos, carry, *, tq, tk, ktop):
    mx, mn, c0, cp = carry
    hi0 = jnp.max(mx, axis=0, keepdims=True)
    lo0 = jnp.min(mn, axis=0, keepdims=True)
    n_ge0 = jnp.sum(c0, axis=0, keepdims=True)
    n_gt0 = jnp.sum(cp, axis=0, keepdims=True)
    zero = jnp.zeros((ROWS, tq), F32)

    def count(pred):
        def body(c, acc):
            off = pl.multiple_of(c * tk, tk)
            for j in range(tk // ROWS):
                x = score_ref[slot, pl.ds(pl.multiple_of(off + j * ROWS, ROWS), ROWS), :]
                acc = acc + jnp.where(pred(x), 1.0, 0.0)
            return acc
        return jnp.sum(lax.fori_loop(0, nck, body, zero), axis=0, keepdims=True)

    n_valid = (qpos + 1).astype(F32)
    kk = jnp.minimum(n_valid, float(ktop))

    all_in = n_valid <= kk
    above = jnp.logical_and(jnp.logical_not(all_in), n_gt0 >= kk)
    below = jnp.logical_and(jnp.logical_not(all_in), n_ge0 < kk)
    lo = jnp.where(all_in, lo0, jnp.where(below, lo0, 0.0))
    clo = jnp.where(all_in, n_valid, jnp.where(below, n_valid, n_ge0))
    hi = jnp.where(below, 0.0, hi0)
    fin = jnp.where(jnp.logical_or(above, below), 0.0, 1.0)

    def bis_cond(st):
        it, lo, hi, clo, fin = st
        return jnp.logical_and(it < MAX_BISECT, jnp.min(fin) < 0.5)

    def bis_body(st):
        it, lo, hi, clo, fin = st
        mid0 = 0.5 * lo + 0.5 * hi
        collapsed = jnp.logical_or(mid0 <= lo, mid0 >= hi)
        mid = jnp.where(collapsed, hi, mid0)
        c = count(lambda x: x >= mid)
        ge = c >= kk
        live = fin < 0.5
        up = jnp.logical_and(live, ge)
        dn = jnp.logical_and(live, jnp.logical_not(ge))
        lo = jnp.where(up, mid, lo)
        clo = jnp.where(up, c, clo)
        hi = jnp.where(dn, mid, hi)
        done = jnp.logical_or(c == kk, collapsed)
        fin = jnp.where(jnp.logical_and(live, done), 1.0, fin)
        return it + 1, lo, hi, clo, fin

    _, thr, _, clo, _ = lax.while_loop(bis_cond, bis_body, (jnp.int32(0), lo, hi, clo, fin))
    thr_ref[slot] = thr
    has_tie = jnp.max(jnp.where(clo > kk, 1.0, 0.0)) > 0.5

    @pl.when(has_tie)
    def _():
        n_gt = count(lambda x: x > thr)
        need = kk - n_gt
        lower = (lax.broadcasted_iota(jnp.int32, (tk, tk), 0)
                 >= lax.broadcasted_iota(jnp.int32, (tk, tk), 1))
        lower = jnp.where(lower, 1.0, 0.0).astype(BF16)

        def body(c, seen):
            off = pl.multiple_of(c * tk, tk)
            x = score_ref[slot, pl.ds(off, tk), :]
            eq = jnp.where(x == thr, 1.0, 0.0)
            rank = jnp.dot(lower, eq.astype(BF16), preferred_element_type=F32) + seen
            keep = jnp.where(x > thr, 1.0, jnp.where(rank <= need, eq, 0.0))
            score_ref[slot, pl.ds(off, tk), :] = jnp.where(keep > 0.5, 1.0, -1.0)
            return seen + jnp.sum(eq, axis=0, keepdims=True)
        lax.fori_loop(0, nck, body, jnp.zeros((1, tq), F32))
        thr_ref[slot] = jnp.zeros((1, tq), F32)


def _dsa_kernel(qi_ref, wt_ref, qin_ref, wtn_ref, kidx_ref, q_ref, k_ref, vt_ref, o_ref,
                score_ref, thr_ref, qpad_ref, m_ref, acc_ref, ot_ref, st_ref,
                *, tq, tk, ktop):
    qi = pl.program_id(1)
    slot = qi % 2
    nslot = 1 - slot
    chunks = lambda blk: ((blk + 1) * tq + tk - 1) // tk
    nck = chunks(qi)
    lanes_q = lax.broadcasted_iota(jnp.int32, (1, tq), 1)
    select = functools.partial(_select_threshold, score_ref, thr_ref, tq=tq, tk=tk, ktop=ktop)

    @pl.when(qi == 0)
    def _():
        q2, wt = _indexer_operands(qi_ref, wt_ref)

        def body(c, carry):
            return _score_chunk(kidx_ref, score_ref, 0, q2, wt, lanes_q,
                                pl.multiple_of(c * tk, tk), carry, tq=tq, tk=tk)
        select(0, nck, lanes_q, lax.fori_loop(0, nck, body, _score_carry(tq)))

    _init_streams(q_ref, qpad_ref, m_ref, acc_ref)
    thr_sel = thr_ref[slot]

    def attend(off):
        bias = jnp.where(score_ref[slot, pl.ds(off, tk), :] >= thr_sel, 0.0, MASKED)
        _attend_chunk(k_ref, vt_ref, qpad_ref, m_ref, acc_ref, st_ref, off, bias,
                      tkv=tk, dv=HEAD_DIM, v_head=lambda s: s, den_on_mxu=True)

    has_next = qi + 1 < pl.num_programs(1)

    @pl.when(has_next)
    def _():
        q2, wt = _indexer_operands(qin_ref, wtn_ref)
        qpos = (qi + 1) * tq + lanes_q
        score = functools.partial(_score_chunk, kidx_ref, score_ref, nslot, q2, wt, qpos,
                                  tq=tq, tk=tk)

        def paired(c, carry):
            off = pl.multiple_of(c * tk, tk)
            carry = score(off, carry)
            attend(off)
            return carry
        carry = lax.fori_loop(0, nck, paired, _score_carry(tq))
        nck_next = chunks(qi + 1)
        carry = lax.fori_loop(nck, nck_next,
                              lambda c, carry: score(pl.multiple_of(c * tk, tk), carry), carry)
        select(nslot, nck_next, qpos, carry)

    @pl.when(jnp.logical_not(has_next))
    def _():
        def body(c, _):
            attend(pl.multiple_of(c * tk, tk))
            return 0
        lax.fori_loop(0, nck, body, 0)

    for s in range(N_STREAMS):
        ot_ref[s * HEAD_DIM:(s + 1) * HEAD_DIM, :] = _normalised(acc_ref, s, HEAD_DIM)
    o_ref[0] = ot_ref[...].T.astype(o_ref.dtype)


def _resident(shape):
    return pl.BlockSpec(shape, lambda b, i: (b, 0, 0), pipeline_mode=pl.Buffered(1))


def _dsa_branch(q_i, w_it, k_i, q, k, vt, tq=256, tk=512):
    B, S, W = q.shape
    nq = S // tq
    ktop = min(TOPK_MAX, S // 4)
    kern = functools.partial(_dsa_kernel, tq=tq, tk=tk, ktop=ktop)
    nxt = lambda i: jnp.minimum(i + 1, nq - 1)
    return pl.pallas_call(
        kern,
        grid=(B, nq),
        in_specs=[pl.BlockSpec((1, tq, IDX_QW), lambda b, i: (b, i, 0)),
                  pl.BlockSpec((1, IDX_HEADS, tq), lambda b, i: (b, 0, i)),
                  pl.BlockSpec((1, tq, IDX_QW), lambda b, i: (b, nxt(i), 0)),
                  pl.BlockSpec((1, IDX_HEADS, tq), lambda b, i: (b, 0, nxt(i))),
                  _resident((1, S, IDX_DIM)),
                  pl.BlockSpec((1, tq, W), lambda b, i: (b, i, 0)),
                  _resident((1, S, W)), _resident((1, W, S))],
        out_specs=pl.BlockSpec((1, tq, W), lambda b, i: (b, i, 0)),
        out_shape=jax.ShapeDtypeStruct((B, S, W), BF16),
        scratch_shapes=[pltpu.VMEM((2, S, tq), F32), pltpu.VMEM((2, 1, tq), F32),
                        pltpu.VMEM((N_STREAMS, tq, LANES), BF16),
                        pltpu.VMEM((N_STREAMS, 1, tq), F32),
                        pltpu.VMEM((N_STREAMS, HEAD_DIM + DEN_ROWS, tq), F32),
                        pltpu.VMEM((W, tq), F32),
                        pltpu.VMEM((N_STREAMS, tk, tq), F32)],
        compiler_params=_cparams(("arbitrary", "arbitrary")),
        name="dsa_branch",
    )(q_i, w_it, q_i, w_it, k_i, q, k, vt)


def _diff_kernel(q_ref, k_ref, vt_ref, lq1_ref, lk1_ref, lq2_ref, lk2_ref, g_ref, o_ref,
                 qpad_ref, m_ref, acc_ref, ot_ref, st_ref, *, tq, tkv, lam_init):
    qi = pl.program_id(1)
    nck = ((qi + 1) * tq + tkv - 1) // tkv
    nfull = (qi * tq + 1) // tkv
    lam = (jnp.exp(jnp.sum(lq1_ref[...] * lk1_ref[...], axis=-1, keepdims=True))
           - jnp.exp(jnp.sum(lq2_ref[...] * lk2_ref[...], axis=-1, keepdims=True)) + lam_init)
    _init_streams(q_ref, qpad_ref, m_ref, acc_ref)
    attend = functools.partial(_attend_chunk, k_ref, vt_ref, qpad_ref, m_ref, acc_ref,
                               st_ref, tkv=tkv, dv=DIFF_VDIM, v_head=lambda s: s // 2,
                               den_on_mxu=False)

    def full_body(c, _):
        attend(pl.multiple_of(c * tkv, tkv), None)
        return 0
    lax.fori_loop(0, nfull, full_body, 0)

    qpos = qi * tq + lax.broadcasted_iota(jnp.int32, (1, tq), 1)
    krow = lax.broadcasted_iota(jnp.int32, (tkv, tq), 0)

    def diag_body(c, _):
        off = pl.multiple_of(c * tkv, tkv)
        attend(off, jnp.where((off + krow) <= qpos, 0.0, MASKED))
        return 0
    lax.fori_loop(nfull, nck, diag_body, 0)

    for h in range(DIFF_HEADS):
        o1 = _normalised(acc_ref, 2 * h, DIFF_VDIM)
        o2 = _normalised(acc_ref, 2 * h + 1, DIFF_VDIM)
        ot_ref[h * DIFF_VDIM:(h + 1) * DIFF_VDIM, :] = o1 - lam * o2
    o = ot_ref[...].T
    for h in range(DIFF_HEADS):
        blk = _rmsnorm(o[:, h * DIFF_VDIM:(h + 1) * DIFF_VDIM], g_ref[...], DIFF_SUBLN_EPS)
        o_ref[0, :, h * DIFF_VDIM:(h + 1) * DIFF_VDIM] = (blk * (1.0 - lam_init)).astype(o_ref.dtype)


def _diff_branch(q, k, vt, lq1, lk1, lq2, lk2, g, lam_init, tq=512, tkv=512):
    B, S, W = q.shape
    kern = functools.partial(_diff_kernel, tq=tq, tkv=tkv, lam_init=lam_init)
    vec = lambda a: pl.BlockSpec(a.shape, lambda b, i: (0, 0))
    return pl.pallas_call(
        kern,
        grid=(B, S // tq),
        in_specs=[pl.BlockSpec((1, tq, W), lambda b, i: (b, i, 0)),
                  _resident((1, S, W)), _resident((1, W, S)),
                  vec(lq1), vec(lk1), vec(lq2), vec(lk2), vec(g)],
        out_specs=pl.BlockSpec((1, tq, W), lambda b, i: (b, i, 0)),
        out_shape=jax.ShapeDtypeStruct((B, S, W), BF16),
        scratch_shapes=[pltpu.VMEM((N_STREAMS, tq, LANES), BF16),
                        pltpu.VMEM((N_STREAMS, 1, tq), F32),
                        pltpu.VMEM((N_STREAMS, DIFF_VDIM + DEN_ROWS, tq), F32),
                        pltpu.VMEM((W, tq), F32),
                        pltpu.VMEM((N_STREAMS, tkv, tq), F32)],
        compiler_params=_cparams(("parallel", "arbitrary")),
        name="diff_branch",
    )(q, k, vt, lq1, lk1, lq2, lk2, g)


def _merge_kernel(x_ref, oa_ref, ob_ref, g_ref, wg_ref, gb_ref, wa_ref, wb_ref, wo_ref, y_ref):
    x = x_ref[...]
    h = _rmsnorm(x, g_ref[...], NORM_EPS).astype(BF16)
    gates = jax.nn.sigmoid(jnp.dot(h, wg_ref[...], preferred_element_type=F32) + gb_ref[...])
    pa = jnp.dot(oa_ref[...], wa_ref[...], preferred_element_type=F32)
    pb = jnp.dot(ob_ref[...], wb_ref[...], preferred_element_type=F32)
    merged = gates[:, :D_MODEL] * pa + gates[:, D_MODEL:] * pb
    y_ref[...] = x + jnp.dot(merged.astype(BF16), wo_ref[...], preferred_element_type=F32)


def _merge_project(x2, oa, ob, g, wg, gb, wa, wb, wo, tm=512):
    M = x2.shape[0]
    row = lambda w: pl.BlockSpec((tm, w), lambda i: (i, 0))
    full = lambda a: pl.BlockSpec(a.shape, lambda i: (0,) * a.ndim, pipeline_mode=pl.Buffered(1))
    return pl.pallas_call(
        _merge_kernel,
        grid=(M // tm,),
        in_specs=[row(D_MODEL), row(512), row(512), full(g), full(wg), full(gb),
                  full(wa), full(wb), full(wo)],
        out_specs=row(D_MODEL),
        out_shape=jax.ShapeDtypeStruct((M, D_MODEL), F32),
        compiler_params=_cparams(("parallel",)),
        name="merge_project",
    )(x2, oa, ob, g, wg, gb, wa, wb, wo)


def _ffn_kernel(x_ref, g_ref, wgate_ref, wup_ref, wdown_ref, gf_ref, y_ref, *, final_norm):
    x = x_ref[...]
    h = _rmsnorm(x, g_ref[...], NORM_EPS).astype(BF16)
    gate = jnp.dot(h, wgate_ref[...], preferred_element_type=F32)
    up = jnp.dot(h, wup_ref[...], preferred_element_type=F32)
    act = (gate * jax.nn.sigmoid(gate) * up).astype(BF16)
    y = x + jnp.dot(act, wdown_ref[...], preferred_element_type=F32)
    if final_norm:
        y = _rmsnorm(y, gf_ref[...], NORM_EPS)
    y_ref[...] = y


def _ffn(x2, g, wgate, wup, wdown, gf, final_norm, tm=512):
    M = x2.shape[0]
    row = lambda w: pl.BlockSpec((tm, w), lambda i: (i, 0))
    full = lambda a: pl.BlockSpec(a.shape, lambda i: (0,) * a.ndim, pipeline_mode=pl.Buffered(1))
    return pl.pallas_call(
        functools.partial(_ffn_kernel, final_norm=final_norm),
        grid=(M // tm,),
        in_specs=[row(D_MODEL), full(g), full(wgate), full(wup), full(wdown), full(gf)],
        out_specs=row(D_MODEL),
        out_shape=jax.ShapeDtypeStruct((M, D_MODEL), F32),
        compiler_params=_cparams(("parallel",)),
        name="swiglu_ffn",
    )(x2, g, wgate, wup, wdown, gf)


def _inv_freq(rot_dim):
    return jnp.power(jnp.float32(ROPE_THETA), -jnp.arange(0, rot_dim, 2, dtype=F32) / rot_dim)


def _inv_freq_column():
    inv = jnp.concatenate([_inv_freq(ROT_DIM_HEAD), _inv_freq(ROT_DIM_IDX)])
    return jnp.pad(inv, (0, TRIG_ROWS - inv.shape[0]))[:, None]


def _pad_lanes(v):
    return jnp.pad(v.astype(F32), (0, LANES - v.shape[0]))[None, :]


def kernel(x, positions, norm_mix_g, w_in, idx_k_norm_g, idx_k_norm_b, diff_lambda_q1,
           diff_lambda_k1, diff_lambda_q2, diff_lambda_k2, diff_subln_g, gate_b, w_branch_dsa,
           w_branch_diff, w_out, norm_ffn_g, w_ffn_in, w_ffn_out, norm_final_g):
    B, S, D = x.shape
    M = B * S
    depth = w_in.shape[0]
    if depth == 0:
        raise ValueError("depth must be positive")
    pos_t = positions.astype(F32)[:, None, :]
    inv_col = _inv_freq_column()
    paired = _paired_columns(2 * DSA_WIDTH)
    row = lambda v: v.astype(F32)[None, :]

    n_idx = IDX_QW + IDX_DIM + IDX_HEADS
    o_a = 0
    o_i = 3 * DSA_WIDTH
    o_b = o_i + n_idx
    o_g = o_b + 2 * DIFF_QK_WIDTH + DIFF_WIDTH

    x2 = x.reshape(M, D)
    for l in range(depth):
        lam_init = 0.8 - 0.6 * math.exp(-0.3 * l)
        wl = w_in[l]
        wa = wl[:, o_a:o_a + 2 * DSA_WIDTH][:, paired].astype(BF16)
        wavt = wl[:, o_a + 2 * DSA_WIDTH:o_a + 3 * DSA_WIDTH].T.astype(BF16)
        wi = jnp.pad(wl[:, o_i:o_i + n_idx], ((0, 0), (0, IDX_QW + LANES - n_idx))).astype(BF16)
        wb = wl[:, o_b:o_b + 2 * DIFF_QK_WIDTH][:, paired].astype(BF16)
        wbvt = wl[:, o_b + 2 * DIFF_QK_WIDTH:o_b + 2 * DIFF_QK_WIDTH + DIFF_WIDTH].T.astype(BF16)
        wg = wl[:, o_g:o_g + N_BRANCH * D_MODEL].astype(BF16)

        (q_a, k_a, vt_a, q_b, k_b, vt_b, q_i, k_i, w_it) = _in_projection(
            x2.reshape(B, S, D), pos_t, row(norm_mix_g[l]), wa, wavt, wb, wbvt, wi, inv_col,
            _pad_lanes(idx_k_norm_g[l]), _pad_lanes(idx_k_norm_b[l]))

        o_dsa = _dsa_branch(q_i, w_it, k_i, q_a, k_a, vt_a)
        o_diff = _diff_branch(q_b, k_b, vt_b,
                              row(diff_lambda_q1[l]), row(diff_lambda_k1[l]),
                              row(diff_lambda_q2[l]), row(diff_lambda_k2[l]),
                              row(diff_subln_g[l]), lam_init)

        x2 = _merge_project(x2, o_dsa.reshape(M, DSA_WIDTH), o_diff.reshape(M, DIFF_WIDTH),
                            row(norm_mix_g[l]), wg, row(gate_b[l]),
                            w_branch_dsa[l].astype(BF16), w_branch_diff[l].astype(BF16),
                            w_out[l].astype(BF16))
        wf = w_ffn_in[l]
        x2 = _ffn(x2, row(norm_ffn_g[l]), wf[:, :D_FF].astype(BF16), wf[:, D_FF:].astype(BF16),
                  w_ffn_out[l].astype(BF16), row(norm_final_g), final_norm=(l == depth - 1))

    return x2.reshape(B, S, D)
```

```python
import functools
import math

import jax
import jax.numpy as jnp
from jax import lax
from jax.experimental import pallas as pl
from jax.experimental.pallas import tpu as pltpu

F32 = jnp.float32
BF16 = jnp.bfloat16

D_MODEL = 1024
HEAD_DIM = 64
DSA_HEADS = 8
DSA_WIDTH = DSA_HEADS * HEAD_DIM
IDX_HEADS = 8
IDX_DIM = 32
IDX_QW = IDX_HEADS * IDX_DIM
TOPK_MAX = 256
DIFF_HEADS = 4
DIFF_DIM = 64
DIFF_VDIM = 2 * DIFF_DIM
DIFF_QK_WIDTH = DIFF_HEADS * 2 * DIFF_DIM
DIFF_WIDTH = DIFF_HEADS * DIFF_VDIM
DIFF_SUBLN_EPS = 1e-5
N_BRANCH = 2
ROPE_THETA = 500000.0
ROT_DIM_HEAD = HEAD_DIM // 4
ROT_DIM_IDX = IDX_DIM // 4
FFN_MULT = 256
D_FF = -(-8 * D_MODEL // (3 * FFN_MULT)) * FFN_MULT
NORM_EPS = 1e-6

LANES = 128
VMEM_LIMIT = 56 * 1024 * 1024
MASKED = -1e30
MAX_BISECT = 512
LOG2E = 1.4426950408889634
N_STREAMS = 8
ROWS = 64
RANK_ROWS = 256
DEN_ROWS = 16
TRIG_ROWS = 16
TRIG_ONE = 12
TRIG_ZERO = TRIG_ROWS + 12

NT_DIMS = (((1,), (1,)), ((), ()))


def _cparams(sem, flags=None):
    return pltpu.CompilerParams(dimension_semantics=sem, vmem_limit_bytes=VMEM_LIMIT, flags=flags)


def _rmsnorm(x, g, eps):
    return x * lax.rsqrt(jnp.mean(x * x, axis=-1, keepdims=True) + eps) * g


def _rope_block(y, cos, sin_lo, sin_hi, half):
    return (y * cos + pltpu.roll(y, half, 1) * sin_hi
            + pltpu.roll(y, LANES - half, 1) * sin_lo)


def _trig_table(pos_t, inv_col):
    ang_t = inv_col * pos_t
    pad = jnp.zeros((LANES - 2 * TRIG_ROWS, ang_t.shape[1]), F32)
    return jnp.concatenate([jnp.cos(ang_t), jnp.sin(ang_t), pad], axis=0).T


def _rope_tables(tbl, first, period, half):
    d = lax.broadcasted_iota(jnp.int32, tbl.shape, 1) % period
    rot = d < 2 * half
    take = lambda idx: jnp.take_along_axis(tbl, idx, axis=1)
    cos = take(jnp.where(rot, first + d % half, TRIG_ONE))
    sin_lo = -take(jnp.where(d < half, TRIG_ROWS + first + d, TRIG_ZERO))
    sin_hi = take(jnp.where(rot & (d >= half), TRIG_ROWS + first + d % half, TRIG_ZERO))
    return cos, sin_lo, sin_hi


def _rope_tables_paired(tbl):
    d = lax.broadcasted_iota(jnp.int32, tbl.shape, 1)
    x1 = d < ROT_DIM_HEAD
    rot = x1 | ((d >= LANES // 2) & (d < LANES // 2 + ROT_DIM_HEAD))
    f = d % (ROT_DIM_HEAD // 2)
    cos = jnp.take_along_axis(tbl, jnp.where(rot, f, TRIG_ONE), axis=1)
    sin = jnp.take_along_axis(tbl, jnp.where(rot, TRIG_ROWS + f, TRIG_ZERO), axis=1)
    return cos, jnp.where(x1, -sin, sin)


def _paired_head_lanes(second):
    lane = lax.broadcasted_iota(jnp.int32, (1, LANES), 1)
    half = ROT_DIM_HEAD // 2
    in_second = ((lane >= half) & (lane < 2 * half)) | (lane >= LANES // 2 + half)
    return in_second if second else jnp.logical_not(in_second)


def _paired_columns(width):
    half = ROT_DIM_HEAD // 2
    perm = list(range(LANES))
    perm[half:2 * half] = range(HEAD_DIM, HEAD_DIM + half)
    perm[LANES // 2:LANES // 2 + half] = range(half, 2 * half)
    return jnp.asarray([(c // LANES) * LANES + perm[c % LANES] for c in range(width)], jnp.int32)


def _inproj_kernel(x_ref, post_ref, g_ref, wa_ref, wavt_ref, wb_ref, wbvt_ref, wi_ref, invcol_ref,
                   lng_ref, lnb_ref,
                   qa_ref, ka_ref, vat_ref, qb_ref, kb_ref, vbt_ref, qi_ref, ki_ref, wit_ref):
    h = _rmsnorm(x_ref[0], g_ref[...], NORM_EPS).astype(BF16)
    tbl = _trig_table(post_ref[0], invcol_ref[...])
    cos_k, sin_k = _rope_tables_paired(tbl)
    cos_i, slo_i, shi_i = _rope_tables(tbl, ROT_DIM_HEAD // 2, IDX_DIM, ROT_DIM_IDX // 2)
    q_scale = HEAD_DIM ** -0.5 * LOG2E
    cos_q, sin_q = cos_k * q_scale, sin_k * q_scale

    def branch(w_ref, wvt_ref, q_ref, k_ref, vt_ref):
        for grp, (o_ref, cos, sin) in enumerate(((q_ref, cos_q, sin_q), (k_ref, cos_k, sin_k))):
            y = jnp.dot(h, w_ref[:, grp * 512:(grp + 1) * 512], preferred_element_type=F32)
            for j in range(512 // LANES):
                blk = y[:, j * LANES:(j + 1) * LANES]
                blk = blk * cos + pltpu.roll(blk, LANES // 2, 1) * sin
                o_ref[0, :, j * LANES:(j + 1) * LANES] = blk.astype(o_ref.dtype)
        vt_ref[0] = lax.dot_general(wvt_ref[...], h, NT_DIMS,
                                    preferred_element_type=F32).astype(vt_ref.dtype)

    branch(wa_ref, wavt_ref, qa_ref, ka_ref, vat_ref)
    branch(wb_ref, wbvt_ref, qb_ref, kb_ref, vbt_ref)

    r = jnp.dot(h, wi_ref[...], preferred_element_type=F32)
    for j in range(IDX_QW // LANES):
        blk = _rope_block(r[:, j * LANES:(j + 1) * LANES], cos_i, slo_i, shi_i, ROT_DIM_IDX // 2)
        qi_ref[0, :, j * LANES:(j + 1) * LANES] = blk.astype(qi_ref.dtype)
    kw = r[:, IDX_QW:IDX_QW + LANES]
    lane = lax.broadcasted_iota(jnp.int32, (1, LANES), 1)
    is_k = lane < IDX_DIM
    mu = jnp.sum(jnp.where(is_k, kw, 0.0), axis=-1, keepdims=True) * (1.0 / IDX_DIM)
    xc = jnp.where(is_k, kw - mu, 0.0)
    var = jnp.sum(xc * xc, axis=-1, keepdims=True) * (1.0 / IDX_DIM)
    kn = xc * lax.rsqrt(var + NORM_EPS) * lng_ref[...] + lnb_ref[...]
    kn = _rope_block(kn, cos_i, slo_i, shi_i, ROT_DIM_IDX // 2)
    ki_ref[0] = kn[:, :IDX_DIM].astype(ki_ref.dtype)
    w_scale = IDX_HEADS ** -0.5 * IDX_DIM ** -0.5
    wit_ref[0] = kw.T[IDX_DIM:IDX_DIM + IDX_HEADS, :] * w_scale


def _in_projection(x, pos_t, g, wa, wavt, wb, wbvt, wi, inv_col, lng, lnb, tm=256):
    B, S, _ = x.shape
    row = lambda w: pl.BlockSpec((1, tm, w), lambda b, i: (b, i, 0))
    col = lambda r: pl.BlockSpec((1, r, tm), lambda b, i: (b, 0, i))
    full = lambda a: pl.BlockSpec(a.shape, lambda b, i: (0,) * a.ndim,
                                  pipeline_mode=pl.Buffered(1))
    sds = jax.ShapeDtypeStruct
    out_shape = [sds((B, S, 512), BF16), sds((B, S, 512), BF16), sds((B, 512, S), BF16),
                 sds((B, S, 512), BF16), sds((B, S, 512), BF16), sds((B, 512, S), BF16),
                 sds((B, S, IDX_QW), BF16), sds((B, S, IDX_DIM), BF16), sds((B, IDX_HEADS, S), F32)]
    return pl.pallas_call(
        _inproj_kernel,
        grid=(B, S // tm),
        in_specs=[row(D_MODEL), col(1), full(g), full(wa), full(wavt), full(wb), full(wbvt),
                  full(wi), full(inv_col), full(lng), full(lnb)],
        out_specs=[row(512), row(512), col(512), row(512), row(512), col(512),
                   row(IDX_QW), row(IDX_DIM), col(IDX_HEADS)],
        out_shape=out_shape,
        compiler_params=_cparams(("parallel", "parallel")),
        name="in_projection",
    )(x, pos_t, g, wa, wavt, wb, wbvt, wi, inv_col, lng, lnb)


def _init_streams(q_ref, qpad_ref, m_ref, acc_ref):
    for s in range(N_STREAMS):
        blk = q_ref[0, :, (s // 2) * LANES:(s // 2 + 1) * LANES]
        qpad_ref[s] = jnp.where(_paired_head_lanes(s % 2 == 1), blk, jnp.zeros_like(blk))
    m_ref[...] = jnp.full(m_ref.shape, MASKED, F32)
    acc_ref[...] = jnp.zeros(acc_ref.shape, F32)


def _attend_chunk(k_ref, vt_ref, qpad_ref, m_ref, acc_ref, st_ref, off, bias,
                  *, tkv, dv, v_head, den_on_mxu, between=None):
    col_max = []
    for s in range(N_STREAMS):
        kp = k_ref[0, pl.ds(off, tkv), (s // 2) * LANES:(s // 2 + 1) * LANES]
        st = lax.dot_general(kp, qpad_ref[s], NT_DIMS, preferred_element_type=F32)
        if bias is not None:
            st = st + bias
        st_ref[s] = st
        col_max.append(jnp.max(st, axis=0, keepdims=True))
    extra = None if between is None else between()
    for s in range(N_STREAMS):
        m_old = m_ref[s]
        m_new = jnp.maximum(m_old, col_max[s])
        alpha = jnp.exp2(m_old - m_new)
        r0 = v_head(s) * dv
        v_c = vt_ref[0, r0:r0 + dv, pl.ds(off, tkv)]
        if den_on_mxu:
            p = jnp.exp2((st_ref[s] - m_new).astype(BF16))
            v_aug = jnp.concatenate([v_c, jnp.ones((DEN_ROWS, tkv), BF16)], axis=0)
            acc_ref[s] = alpha * acc_ref[s] + jnp.dot(v_aug, p, preferred_element_type=F32)
        else:
            p = jnp.exp2(st_ref[s] - m_new)
            acc_ref[s, 0:dv, :] = alpha * acc_ref[s, 0:dv, :] + jnp.dot(
                v_c, p.astype(BF16), preferred_element_type=F32)
            acc_ref[s, dv:dv + 1, :] = (alpha * acc_ref[s, dv:dv + 1, :]
                                        + jnp.sum(p, axis=0, keepdims=True))
        m_ref[s] = m_new
    return extra


def _normalised(acc_ref, s, dv):
    return acc_ref[s, 0:dv, :] / acc_ref[s, dv:dv + 1, :]


def _indexer_operands(qi_ref, wt_ref):
    qx = qi_ref[0]
    q2 = jnp.concatenate([qx[:, h * IDX_DIM:(h + 1) * IDX_DIM] for h in range(IDX_HEADS)], axis=0)
    return q2, wt_ref[0]


def _score_chunk(kidx_ref, score_ref, slot, q2, wt, qpos, off, carry, *, tq, tk):
    mx, mn, c0, cp = carry
    kc = kidx_ref[0, pl.ds(off, tk), :]
    logits = lax.dot_general(kc, q2, NT_DIMS, preferred_element_type=F32)
    sc = jnp.maximum(logits[:, :tq], 0.0) * wt[0:1, :]
    for h in range(1, IDX_HEADS):
        sc = sc + jnp.maximum(logits[:, h * tq:(h + 1) * tq], 0.0) * wt[h:h + 1, :]
    causal = (off + lax.broadcasted_iota(jnp.int32, (tk, tq), 0)) <= qpos
    s_hi = jnp.where(causal, sc, -jnp.inf)
    s_lo = jnp.where(causal, sc, jnp.inf)
    score_ref[slot, pl.ds(off, tk), :] = s_hi
    for j in range(tk // ROWS):
        b_hi = s_hi[j * ROWS:(j + 1) * ROWS]
        mx = jnp.maximum(mx, b_hi)
        mn = jnp.minimum(mn, s_lo[j * ROWS:(j + 1) * ROWS])
        c0 = c0 + jnp.where(b_hi >= 0.0, 1.0, 0.0)
        cp = cp + jnp.where(b_hi > 0.0, 1.0, 0.0)
    return mx, mn, c0, cp


def _score_carry(tq):
    zero = jnp.zeros((ROWS, tq), F32)
    return zero - jnp.inf, zero + jnp.inf, zero, zero


def _select_threshold(score_ref, thr_ref, slot, nck, qpos, carry, *, tq, tk, ktop):
    mx, mn, c0, cp = carry
    hi0 = jnp.max(mx, axis=0, keepdims=True)
    lo0 = jnp.min(mn, axis=0, keepdims=True)
    n_ge0 = jnp.sum(c0, axis=0, keepdims=True)
    n_gt0 = jnp.sum(cp, axis=0, keepdims=True)
    zero = jnp.zeros((ROWS, tq), F32)

    def count(pred):
        def body(c, acc):
            off = pl.multiple_of(c * tk, tk)
            for j in range(tk // ROWS):
                x = score_ref[slot, pl.ds(pl.multiple_of(off + j * ROWS, ROWS), ROWS), :]
                acc = acc + jnp.where(pred(x), 1.0, 0.0)
            return acc
        return jnp.sum(lax.fori_loop(0, nck, body, zero), axis=0, keepdims=True)

    n_valid = (qpos + 1).astype(F32)
    kk = jnp.minimum(n_valid, float(ktop))

    all_in = n_valid <= kk
    above = jnp.logical_and(jnp.logical_not(all_in), n_gt0 >= kk)
    below = jnp.logical_and(jnp.logical_not(all_in), n_ge0 < kk)
    lo = jnp.where(all_in, lo0, jnp.where(below, lo0, 0.0))
    clo = jnp.where(all_in, n_valid, jnp.where(below, n_valid, n_ge0))
    hi = jnp.where(below, 0.0, hi0)
    fin = jnp.where(jnp.logical_or(above, below), 0.0, 1.0)

    def bis_cond(st):
        return jnp.logical_and(st[0] < MAX_BISECT, jnp.min(st[-1]) < 0.5)

    def bis_body(st):
        it, lo, hi, clo, chi, fin = st
        mid0 = 0.5 * lo + 0.5 * hi
        collapsed = jnp.logical_or(mid0 <= lo, mid0 >= hi)
        mid = jnp.where(collapsed, hi, mid0)
        c = count(lambda x: x >= mid)
        ge = c >= kk
        live = fin < 0.5
        up = jnp.logical_and(live, ge)
        dn = jnp.logical_and(live, jnp.logical_not(ge))
        lo = jnp.where(up, mid, lo)
        clo = jnp.where(up, c, clo)
        hi = jnp.where(dn, mid, hi)
        chi = jnp.where(dn, c, chi)
        done = jnp.logical_or(c == kk, collapsed)
        fin = jnp.where(jnp.logical_and(live, done), 1.0, fin)
        return it + 1, lo, hi, clo, chi, fin

    _, thr, hi, clo, chi, _ = lax.while_loop(
        bis_cond, bis_body, (jnp.int32(0), lo, hi, clo, jnp.zeros_like(lo), fin))
    thr_ref[slot] = thr
    tied = clo > kk
    has_tie = jnp.max(jnp.where(tied, 1.0, 0.0)) > 0.5

    @pl.when(has_tie)
    def _():
        n_gt = jnp.where(jnp.logical_or(above, below),
                         jnp.where(thr >= hi, 0.0, chi), n_gt0)
        need = jnp.where(tied, kk - n_gt, float(2 ** 24))
        lower = (lax.broadcasted_iota(jnp.int32, (RANK_ROWS, RANK_ROWS), 0)
                 >= lax.broadcasted_iota(jnp.int32, (RANK_ROWS, RANK_ROWS), 1))
        lower = jnp.where(lower, 1.0, 0.0).astype(BF16)

        def body(c, need):
            rows = [pl.multiple_of(c * tk + j * RANK_ROWS, RANK_ROWS)
                    for j in range(tk // RANK_ROWS)]
            xs = [score_ref[slot, pl.ds(r0, RANK_ROWS), :] for r0 in rows]
            ranks = [jnp.dot(lower, jnp.where(x == thr, 1.0, 0.0).astype(BF16),
                             preferred_element_type=F32) for x in xs]
            for r0, x, rank in zip(rows, xs, ranks):
                kept = jnp.where(x > thr, 1.0,
                                 jnp.where(rank <= need, jnp.where(x == thr, 1.0, -1.0), -1.0))
                score_ref[slot, pl.ds(r0, RANK_ROWS), :] = kept
                need = need - rank[RANK_ROWS - 1:RANK_ROWS, :]
            return need
        lax.fori_loop(0, nck, body, need)
        thr_ref[slot] = jnp.zeros((1, tq), F32)


def _dsa_kernel(qi_ref, wt_ref, qin_ref, wtn_ref, kidx_ref, q_ref, k_ref, vt_ref, o_ref,
                score_ref, thr_ref, qpad_ref, m_ref, acc_ref, ot_ref, st_ref,
                *, tq, tk, ktop):
    qi = pl.program_id(1)
    slot = qi % 2
    nslot = 1 - slot
    chunks = lambda blk: ((blk + 1) * tq + tk - 1) // tk
    nck = chunks(qi)
    lanes_q = lax.broadcasted_iota(jnp.int32, (1, tq), 1)
    select = functools.partial(_select_threshold, score_ref, thr_ref, tq=tq, tk=tk, ktop=ktop)

    @pl.when(qi == 0)
    def _():
        q2, wt = _indexer_operands(qi_ref, wt_ref)

        def body(c, carry):
            return _score_chunk(kidx_ref, score_ref, 0, q2, wt, lanes_q,
                                pl.multiple_of(c * tk, tk), carry, tq=tq, tk=tk)
        select(0, nck, lanes_q, lax.fori_loop(0, nck, body, _score_carry(tq)))

    _init_streams(q_ref, qpad_ref, m_ref, acc_ref)
    thr_sel = thr_ref[slot]

    def attend(off, between=None):
        bias = jnp.where(score_ref[slot, pl.ds(off, tk), :] >= thr_sel, 0.0, MASKED)
        return _attend_chunk(k_ref, vt_ref, qpad_ref, m_ref, acc_ref, st_ref, off, bias,
                             tkv=tk, dv=HEAD_DIM, v_head=lambda s: s, den_on_mxu=True,
                             between=between)

    has_next = qi + 1 < pl.num_programs(1)

    @pl.when(has_next)
    def _():
        q2, wt = _indexer_operands(qin_ref, wtn_ref)
        qpos = (qi + 1) * tq + lanes_q
        score = functools.partial(_score_chunk, kidx_ref, score_ref, nslot, q2, wt, qpos,
                                  tq=tq, tk=tk)

        def paired(c, carry):
            off = pl.multiple_of(c * tk, tk)
            return attend(off, between=lambda: score(off, carry))
        carry = lax.fori_loop(0, nck, paired, _score_carry(tq))
        nck_next = chunks(qi + 1)
        carry = lax.fori_loop(nck, nck_next,
                              lambda c, carry: score(pl.multiple_of(c * tk, tk), carry), carry)
        select(nslot, nck_next, qpos, carry)

    @pl.when(jnp.logical_not(has_next))
    def _():
        def body(c, _):
            attend(pl.multiple_of(c * tk, tk))
            return 0
        lax.fori_loop(0, nck, body, 0)

    for s in range(N_STREAMS):
        ot_ref[s * HEAD_DIM:(s + 1) * HEAD_DIM, :] = _normalised(acc_ref, s, HEAD_DIM)
    o_ref[0] = ot_ref[...].T.astype(o_ref.dtype)


def _resident(shape):
    return pl.BlockSpec(shape, lambda b, i: (b, 0, 0), pipeline_mode=pl.Buffered(1))


def _dsa_branch(q_i, w_it, k_i, q, k, vt, tq=256, tk=512):
    B, S, W = q.shape
    nq = S // tq
    ktop = min(TOPK_MAX, S // 4)
    kern = functools.partial(_dsa_kernel, tq=tq, tk=tk, ktop=ktop)
    nxt = lambda i: jnp.minimum(i + 1, nq - 1)
    return pl.pallas_call(
        kern,
        grid=(B, nq),
        in_specs=[pl.BlockSpec((1, tq, IDX_QW), lambda b, i: (b, i, 0)),
                  pl.BlockSpec((1, IDX_HEADS, tq), lambda b, i: (b, 0, i)),
                  pl.BlockSpec((1, tq, IDX_QW), lambda b, i: (b, nxt(i), 0)),
                  pl.BlockSpec((1, IDX_HEADS, tq), lambda b, i: (b, 0, nxt(i))),
                  _resident((1, S, IDX_DIM)),
                  pl.BlockSpec((1, tq, W), lambda b, i: (b, i, 0)),
                  _resident((1, S, W)), _resident((1, W, S))],
        out_specs=pl.BlockSpec((1, tq, W), lambda b, i: (b, i, 0)),
        out_shape=jax.ShapeDtypeStruct((B, S, W), BF16),
        scratch_shapes=[pltpu.VMEM((2, S, tq), F32), pltpu.VMEM((2, 1, tq), F32),
                        pltpu.VMEM((N_STREAMS, tq, LANES), BF16),
                        pltpu.VMEM((N_STREAMS, 1, tq), F32),
                        pltpu.VMEM((N_STREAMS, HEAD_DIM + DEN_ROWS, tq), F32),
                        pltpu.VMEM((W, tq), F32),
                        pltpu.VMEM((N_STREAMS, tk, tq), F32)],
        compiler_params=_cparams(("arbitrary", "arbitrary")),
        name="dsa_branch",
    )(q_i, w_it, q_i, w_it, k_i, q, k, vt)


def _diff_kernel(q_ref, k_ref, vt_ref, lq1_ref, lk1_ref, lq2_ref, lk2_ref, g_ref, o_ref,
                 qpad_ref, m_ref, acc_ref, ot_ref, st_ref, *, tq, tkv, lam_init):
    qi = pl.program_id(1)
    nck = ((qi + 1) * tq + tkv - 1) // tkv
    nfull = (qi * tq + 1) // tkv
    lam = (jnp.exp(jnp.sum(lq1_ref[...] * lk1_ref[...], axis=-1, keepdims=True))
           - jnp.exp(jnp.sum(lq2_ref[...] * lk2_ref[...], axis=-1, keepdims=True)) + lam_init)
    _init_streams(q_ref, qpad_ref, m_ref, acc_ref)
    attend = functools.partial(_attend_chunk, k_ref, vt_ref, qpad_ref, m_ref, acc_ref,
                               st_ref, tkv=tkv, dv=DIFF_VDIM, v_head=lambda s: s // 2,
                               den_on_mxu=False)

    def full_body(c, _):
        attend(pl.multiple_of(c * tkv, tkv), None)
        return 0
    lax.fori_loop(0, nfull, full_body, 0)

    qpos = qi * tq + lax.broadcasted_iota(jnp.int32, (1, tq), 1)
    krow = lax.broadcasted_iota(jnp.int32, (tkv, tq), 0)

    def diag_body(c, _):
        off = pl.multiple_of(c * tkv, tkv)
        attend(off, jnp.where((off + krow) <= qpos, 0.0, MASKED))
        return 0
    lax.fori_loop(nfull, nck, diag_body, 0)

    for h in range(DIFF_HEADS):
        o1 = _normalised(acc_ref, 2 * h, DIFF_VDIM)
        o2 = _normalised(acc_ref, 2 * h + 1, DIFF_VDIM)
        ot_ref[h * DIFF_VDIM:(h + 1) * DIFF_VDIM, :] = o1 - lam * o2
    o = ot_ref[...].T
    for h in range(DIFF_HEADS):
        blk = _rmsnorm(o[:, h * DIFF_VDIM:(h + 1) * DIFF_VDIM], g_ref[...], DIFF_SUBLN_EPS)
        o_ref[0, :, h * DIFF_VDIM:(h + 1) * DIFF_VDIM] = (blk * (1.0 - lam_init)).astype(o_ref.dtype)


def _diff_branch(q, k, vt, lq1, lk1, lq2, lk2, g, lam_init, tq=512, tkv=512):
    B, S, W = q.shape
    kern = functools.partial(_diff_kernel, tq=tq, tkv=tkv, lam_init=lam_init)
    vec = lambda a: pl.BlockSpec(a.shape, lambda b, i: (0, 0))
    return pl.pallas_call(
        kern,
        grid=(B, S // tq),
        in_specs=[pl.BlockSpec((1, tq, W), lambda b, i: (b, i, 0)),
                  _resident((1, S, W)), _resident((1, W, S)),
                  vec(lq1), vec(lk1), vec(lq2), vec(lk2), vec(g)],
        out_specs=pl.BlockSpec((1, tq, W), lambda b, i: (b, i, 0)),
        out_shape=jax.ShapeDtypeStruct((B, S, W), BF16),
        scratch_shapes=[pltpu.VMEM((N_STREAMS, tq, LANES), BF16),
                        pltpu.VMEM((N_STREAMS, 1, tq), F32),
                        pltpu.VMEM((N_STREAMS, DIFF_VDIM + DEN_ROWS, tq), F32),
                        pltpu.VMEM((W, tq), F32),
                        pltpu.VMEM((N_STREAMS, tkv, tq), F32)],
        compiler_params=_cparams(("parallel", "arbitrary")),
        name="diff_branch",
    )(q, k, vt, lq1, lk1, lq2, lk2, g)


def _merge_kernel(x_ref, oa_ref, ob_ref, g_ref, wg_ref, gb_ref, wa_ref, wb_ref, wo_ref, y_ref):
    x = x_ref[...]
    h = _rmsnorm(x, g_ref[...], NORM_EPS).astype(BF16)
    gates = jax.nn.sigmoid(jnp.dot(h, wg_ref[...], preferred_element_type=F32) + gb_ref[...])
    pa = jnp.dot(oa_ref[...], wa_ref[...], preferred_element_type=F32)
    pb = jnp.dot(ob_ref[...], wb_ref[...], preferred_element_type=F32)
    merged = gates[:, :D_MODEL] * pa + gates[:, D_MODEL:] * pb
    y_ref[...] = x + jnp.dot(merged.astype(BF16), wo_ref[...], preferred_element_type=F32)


def _merge_project(x2, oa, ob, g, wg, gb, wa, wb, wo, tm=512):
    M = x2.shape[0]
    row = lambda w: pl.BlockSpec((tm, w), lambda i: (i, 0))
    full = lambda a: pl.BlockSpec(a.shape, lambda i: (0,) * a.ndim, pipeline_mode=pl.Buffered(1))
    return pl.pallas_call(
        _merge_kernel,
        grid=(M // tm,),
        in_specs=[row(D_MODEL), row(512), row(512), full(g), full(wg), full(gb),
                  full(wa), full(wb), full(wo)],
        out_specs=row(D_MODEL),
        out_shape=jax.ShapeDtypeStruct((M, D_MODEL), F32),
        compiler_params=_cparams(("parallel",)),
        name="merge_project",
    )(x2, oa, ob, g, wg, gb, wa, wb, wo)


def _ffn_kernel(x_ref, g_ref, wgate_ref, wup_ref, wdown_ref, gf_ref, y_ref, *, final_norm):
    x = x_ref[...]
    h = _rmsnorm(x, g_ref[...], NORM_EPS).astype(BF16)
    gate = jnp.dot(h, wgate_ref[...], preferred_element_type=F32)
    up = jnp.dot(h, wup_ref[...], preferred_element_type=F32)
    act = (gate * jax.nn.sigmoid(gate) * up).astype(BF16)
    y = x + jnp.dot(act, wdown_ref[...], preferred_element_type=F32)
    if final_norm:
        y = _rmsnorm(y, gf_ref[...], NORM_EPS)
    y_ref[...] = y


def _ffn(x2, g, wgate, wup, wdown, gf, final_norm, tm=512):
    M = x2.shape[0]
    row = lambda w: pl.BlockSpec((tm, w), lambda i: (i, 0))
    full = lambda a: pl.BlockSpec(a.shape, lambda i: (0,) * a.ndim, pipeline_mode=pl.Buffered(1))
    return pl.pallas_call(
        functools.partial(_ffn_kernel, final_norm=final_norm),
        grid=(M // tm,),
        in_specs=[row(D_MODEL), full(g), full(wgate), full(wup), full(wdown), full(gf)],
        out_specs=row(D_MODEL),
        out_shape=jax.ShapeDtypeStruct((M, D_MODEL), F32),
        compiler_params=_cparams(("parallel",)),
        name="swiglu_ffn",
    )(x2, g, wgate, wup, wdown, gf)


def _inv_freq(rot_dim):
    return jnp.power(jnp.float32(ROPE_THETA), -jnp.arange(0, rot_dim, 2, dtype=F32) / rot_dim)


def _inv_freq_column():
    inv = jnp.concatenate([_inv_freq(ROT_DIM_HEAD), _inv_freq(ROT_DIM_IDX)])
    return jnp.pad(inv, (0, TRIG_ROWS - inv.shape[0]))[:, None]


def _pad_lanes(v):
    return jnp.pad(v.astype(F32), (0, LANES - v.shape[0]))[None, :]


def kernel(x, positions, norm_mix_g, w_in, idx_k_norm_g, idx_k_norm_b, diff_lambda_q1,
           diff_lambda_k1, diff_lambda_q2, diff_lambda_k2, diff_subln_g, gate_b, w_branch_dsa,
           w_branch_diff, w_out, norm_ffn_g, w_ffn_in, w_ffn_out, norm_final_g):
    B, S, D = x.shape
    M = B * S
    depth = w_in.shape[0]
    if depth == 0:
        raise ValueError("depth must be positive")
    pos_t = positions.astype(F32)[:, None, :]
    inv_col = _inv_freq_column()
    paired = _paired_columns(2 * DSA_WIDTH)
    row = lambda v: v.astype(F32)[None, :]

    n_idx = IDX_QW + IDX_DIM + IDX_HEADS
    o_a = 0
    o_i = 3 * DSA_WIDTH
    o_b = o_i + n_idx
    o_g = o_b + 2 * DIFF_QK_WIDTH + DIFF_WIDTH

    x2 = x.reshape(M, D)
    for l in range(depth):
        lam_init = 0.8 - 0.6 * math.exp(-0.3 * l)
        wl = w_in[l]
        wa = wl[:, o_a:o_a + 2 * DSA_WIDTH][:, paired].astype(BF16)
        wavt = wl[:, o_a + 2 * DSA_WIDTH:o_a + 3 * DSA_WIDTH].T.astype(BF16)
        wi = jnp.pad(wl[:, o_i:o_i + n_idx], ((0, 0), (0, IDX_QW + LANES - n_idx))).astype(BF16)
        wb = wl[:, o_b:o_b + 2 * DIFF_QK_WIDTH][:, paired].astype(BF16)
        wbvt = wl[:, o_b + 2 * DIFF_QK_WIDTH:o_b + 2 * DIFF_QK_WIDTH + DIFF_WIDTH].T.astype(BF16)
        wg = wl[:, o_g:o_g + N_BRANCH * D_MODEL].astype(BF16)

        (q_a, k_a, vt_a, q_b, k_b, vt_b, q_i, k_i, w_it) = _in_projection(
            x2.reshape(B, S, D), pos_t, row(norm_mix_g[l]), wa, wavt, wb, wbvt, wi, inv_col,
            _pad_lanes(idx_k_norm_g[l]), _pad_lanes(idx_k_norm_b[l]))

        o_dsa = _dsa_branch(q_i, w_it, k_i, q_a, k_a, vt_a)
        o_diff = _diff_branch(q_b, k_b, vt_b,
                              row(diff_lambda_q1[l]), row(diff_lambda_k1[l]),
                              row(diff_lambda_q2[l]), row(diff_lambda_k2[l]),
                              row(diff_subln_g[l]), lam_init)

        x2 = _merge_project(x2, o_dsa.reshape(M, DSA_WIDTH), o_diff.reshape(M, DIFF_WIDTH),
                            row(norm_mix_g[l]), wg, row(gate_b[l]),
                            w_branch_dsa[l].astype(BF16), w_branch_diff[l].astype(BF16),
                            w_out[l].astype(BF16))
        wf = w_ffn_in[l]
        x2 = _ffn(x2, row(norm_ffn_g[l]), wf[:, :D_FF].astype(BF16), wf[:, D_FF:].astype(BF16),
                  w_ffn_out[l].astype(BF16), row(norm_final_g), final_norm=(l == depth - 1))

    return x2.reshape(B, S, D)
```

```python
import functools
import math

import jax
import jax.numpy as jnp
from jax import lax
from jax.experimental import pallas as pl
from jax.experimental.pallas import tpu as pltpu

F32 = jnp.float32
BF16 = jnp.bfloat16

D_MODEL = 1024
HEAD_DIM = 64
DSA_HEADS = 8
DSA_WIDTH = DSA_HEADS * HEAD_DIM
IDX_HEADS = 8
IDX_DIM = 32
IDX_QW = IDX_HEADS * IDX_DIM
TOPK_MAX = 256
DIFF_HEADS = 4
DIFF_DIM = 64
DIFF_VDIM = 2 * DIFF_DIM
DIFF_QK_WIDTH = DIFF_HEADS * 2 * DIFF_DIM
DIFF_WIDTH = DIFF_HEADS * DIFF_VDIM
DIFF_SUBLN_EPS = 1e-5
N_BRANCH = 2
ROPE_THETA = 500000.0
ROT_DIM_HEAD = HEAD_DIM // 4
ROT_DIM_IDX = IDX_DIM // 4
FFN_MULT = 256
D_FF = -(-8 * D_MODEL // (3 * FFN_MULT)) * FFN_MULT
NORM_EPS = 1e-6

LANES = 128
VMEM_LIMIT = 56 * 1024 * 1024
MASKED = -1e30
MAX_BISECT = 512
UNCHECKED_BISECT = 12
LOG2E = 1.4426950408889634
N_STREAMS = 8
ROWS = 64
RANK_ROWS = 256
DEN_ROWS = 16
TRIG_ROWS = 16
TRIG_ONE = 12
TRIG_ZERO = TRIG_ROWS + 12

NT_DIMS = (((1,), (1,)), ((), ()))


def _cparams(sem, flags=None):
    return pltpu.CompilerParams(dimension_semantics=sem, vmem_limit_bytes=VMEM_LIMIT, flags=flags)


def _rmsnorm(x, g, eps):
    return x * lax.rsqrt(jnp.mean(x * x, axis=-1, keepdims=True) + eps) * g


def _rope_block(y, cos, sin_lo, sin_hi, half):
    return (y * cos + pltpu.roll(y, half, 1) * sin_hi
            + pltpu.roll(y, LANES - half, 1) * sin_lo)


def _trig_table(pos_t, inv_col):
    ang_t = inv_col * pos_t
    pad = jnp.zeros((LANES - 2 * TRIG_ROWS, ang_t.shape[1]), F32)
    return jnp.concatenate([jnp.cos(ang_t), jnp.sin(ang_t), pad], axis=0).T


def _rope_tables(tbl, first, period, half):
    d = lax.broadcasted_iota(jnp.int32, tbl.shape, 1) % period
    rot = d < 2 * half
    take = lambda idx: jnp.take_along_axis(tbl, idx, axis=1)
    cos = take(jnp.where(rot, first + d % half, TRIG_ONE))
    sin_lo = -take(jnp.where(d < half, TRIG_ROWS + first + d, TRIG_ZERO))
    sin_hi = take(jnp.where(rot & (d >= half), TRIG_ROWS + first + d % half, TRIG_ZERO))
    return cos, sin_lo, sin_hi


def _rope_tables_paired(tbl):
    d = lax.broadcasted_iota(jnp.int32, tbl.shape, 1)
    x1 = d < ROT_DIM_HEAD
    rot = x1 | ((d >= LANES // 2) & (d < LANES // 2 + ROT_DIM_HEAD))
    f = d % (ROT_DIM_HEAD // 2)
    cos = jnp.take_along_axis(tbl, jnp.where(rot, f, TRIG_ONE), axis=1)
    sin = jnp.take_along_axis(tbl, jnp.where(rot, TRIG_ROWS + f, TRIG_ZERO), axis=1)
    return cos, jnp.where(x1, -sin, sin)


def _paired_head_lanes(second):
    lane = lax.broadcasted_iota(jnp.int32, (1, LANES), 1)
    half = ROT_DIM_HEAD // 2
    in_second = ((lane >= half) & (lane < 2 * half)) | (lane >= LANES // 2 + half)
    return in_second if second else jnp.logical_not(in_second)


def _paired_columns(width):
    half = ROT_DIM_HEAD // 2
    perm = list(range(LANES))
    perm[half:2 * half] = range(HEAD_DIM, HEAD_DIM + half)
    perm[LANES // 2:LANES // 2 + half] = range(half, 2 * half)
    return jnp.asarray([(c // LANES) * LANES + perm[c % LANES] for c in range(width)], jnp.int32)


def _inproj_kernel(x_ref, post_ref, g_ref, wa_ref, wavt_ref, wb_ref, wbvt_ref, wi_ref, invcol_ref,
                   lng_ref, lnb_ref,
                   qa_ref, ka_ref, vat_ref, qb_ref, kb_ref, vbt_ref, qi_ref, ki_ref, wit_ref):
    h = _rmsnorm(x_ref[0], g_ref[...], NORM_EPS).astype(BF16)
    tbl = _trig_table(post_ref[0], invcol_ref[...])
    cos_k, sin_k = _rope_tables_paired(tbl)
    cos_i, slo_i, shi_i = _rope_tables(tbl, ROT_DIM_HEAD // 2, IDX_DIM, ROT_DIM_IDX // 2)
    q_scale = HEAD_DIM ** -0.5 * LOG2E
    cos_q, sin_q = cos_k * q_scale, sin_k * q_scale

    def branch(w_ref, wvt_ref, q_ref, k_ref, vt_ref):
        for grp, (o_ref, cos, sin) in enumerate(((q_ref, cos_q, sin_q), (k_ref, cos_k, sin_k))):
            y = jnp.dot(h, w_ref[:, grp * 512:(grp + 1) * 512], preferred_element_type=F32)
            for j in range(512 // LANES):
                blk = y[:, j * LANES:(j + 1) * LANES]
                blk = blk * cos + pltpu.roll(blk, LANES // 2, 1) * sin
                o_ref[0, :, j * LANES:(j + 1) * LANES] = blk.astype(o_ref.dtype)
        vt_ref[0] = lax.dot_general(wvt_ref[...], h, NT_DIMS,
                                    preferred_element_type=F32).astype(vt_ref.dtype)

    branch(wa_ref, wavt_ref, qa_ref, ka_ref, vat_ref)
    branch(wb_ref, wbvt_ref, qb_ref, kb_ref, vbt_ref)

    r = jnp.dot(h, wi_ref[...], preferred_element_type=F32)
    for j in range(IDX_QW // LANES):
        blk = _rope_block(r[:, j * LANES:(j + 1) * LANES], cos_i, slo_i, shi_i, ROT_DIM_IDX // 2)
        qi_ref[0, :, j * LANES:(j + 1) * LANES] = blk.astype(qi_ref.dtype)
    kw = r[:, IDX_QW:IDX_QW + LANES]
    lane = lax.broadcasted_iota(jnp.int32, (1, LANES), 1)
    is_k = lane < IDX_DIM
    mu = jnp.sum(jnp.where(is_k, kw, 0.0), axis=-1, keepdims=True) * (1.0 / IDX_DIM)
    xc = jnp.where(is_k, kw - mu, 0.0)
    var = jnp.sum(xc * xc, axis=-1, keepdims=True) * (1.0 / IDX_DIM)
    kn = xc * lax.rsqrt(var + NORM_EPS) * lng_ref[...] + lnb_ref[...]
    kn = _rope_block(kn, cos_i, slo_i, shi_i, ROT_DIM_IDX // 2)
    ki_ref[0] = kn[:, :IDX_DIM].astype(ki_ref.dtype)
    w_scale = IDX_HEADS ** -0.5 * IDX_DIM ** -0.5
    wit_ref[0] = kw.T[IDX_DIM:IDX_DIM + IDX_HEADS, :] * w_scale


def _in_projection(x, pos_t, g, wa, wavt, wb, wbvt, wi, inv_col, lng, lnb, tm=256):
    B, S, _ = x.shape
    row = lambda w: pl.BlockSpec((1, tm, w), lambda b, i: (b, i, 0))
    col = lambda r: pl.BlockSpec((1, r, tm), lambda b, i: (b, 0, i))
    full = lambda a: pl.BlockSpec(a.shape, lambda b, i: (0,) * a.ndim,
                                  pipeline_mode=pl.Buffered(1))
    sds = jax.ShapeDtypeStruct
    out_shape = [sds((B, S, 512), BF16), sds((B, S, 512), BF16), sds((B, 512, S), BF16),
                 sds((B, S, 512), BF16), sds((B, S, 512), BF16), sds((B, 512, S), BF16),
                 sds((B, S, IDX_QW), BF16), sds((B, S, IDX_DIM), BF16), sds((B, IDX_HEADS, S), F32)]
    return pl.pallas_call(
        _inproj_kernel,
        grid=(B, S // tm),
        in_specs=[row(D_MODEL), col(1), full(g), full(wa), full(wavt), full(wb), full(wbvt),
                  full(wi), full(inv_col), full(lng), full(lnb)],
        out_specs=[row(512), row(512), col(512), row(512), row(512), col(512),
                   row(IDX_QW), row(IDX_DIM), col(IDX_HEADS)],
        out_shape=out_shape,
        compiler_params=_cparams(("parallel", "parallel")),
        name="in_projection",
    )(x, pos_t, g, wa, wavt, wb, wbvt, wi, inv_col, lng, lnb)


def _init_streams(q_ref, qpad_ref, m_ref, acc_ref):
    for s in range(N_STREAMS):
        blk = q_ref[0, :, (s // 2) * LANES:(s // 2 + 1) * LANES]
        qpad_ref[s] = jnp.where(_paired_head_lanes(s % 2 == 1), blk, jnp.zeros_like(blk))
    m_ref[...] = jnp.full(m_ref.shape, MASKED, F32)
    acc_ref[...] = jnp.zeros(acc_ref.shape, F32)


def _attend_chunk(k_ref, vt_ref, qpad_ref, m_ref, acc_ref, st_ref, off, bias,
                  *, tkv, dv, v_head, den_on_mxu, between=None):
    col_max = []
    for s in range(N_STREAMS):
        kp = k_ref[0, pl.ds(off, tkv), (s // 2) * LANES:(s // 2 + 1) * LANES]
        st = lax.dot_general(kp, qpad_ref[s], NT_DIMS, preferred_element_type=F32)
        if bias is not None:
            st = st + bias
        st_ref[s] = st
        col_max.append(jnp.max(st, axis=0, keepdims=True))
    extra = None if between is None else between()
    for s in range(N_STREAMS):
        m_old = m_ref[s]
        m_new = jnp.maximum(m_old, col_max[s])
        alpha = jnp.exp2(m_old - m_new)
        r0 = v_head(s) * dv
        v_c = vt_ref[0, r0:r0 + dv, pl.ds(off, tkv)]
        if den_on_mxu:
            p = jnp.exp2((st_ref[s] - m_new).astype(BF16))
            v_aug = jnp.concatenate([v_c, jnp.ones((DEN_ROWS, tkv), BF16)], axis=0)
            acc_ref[s] = alpha * acc_ref[s] + jnp.dot(v_aug, p, preferred_element_type=F32)
        else:
            p = jnp.exp2(st_ref[s] - m_new)
            acc_ref[s, 0:dv, :] = alpha * acc_ref[s, 0:dv, :] + jnp.dot(
                v_c, p.astype(BF16), preferred_element_type=F32)
            acc_ref[s, dv:dv + 1, :] = (alpha * acc_ref[s, dv:dv + 1, :]
                                        + jnp.sum(p, axis=0, keepdims=True))
        m_ref[s] = m_new
    return extra


def _normalised(acc_ref, s, dv):
    return acc_ref[s, 0:dv, :] / acc_ref[s, dv:dv + 1, :]


def _indexer_operands(qi_ref, wt_ref):
    qx = qi_ref[0]
    q2 = jnp.concatenate([qx[:, h * IDX_DIM:(h + 1) * IDX_DIM] for h in range(IDX_HEADS)], axis=0)
    return q2, wt_ref[0]


def _score_chunk(kidx_ref, score_ref, slot, q2, wt, qpos, off, carry, *, tq, tk):
    mx, mn, c0, cp = carry
    kc = kidx_ref[0, pl.ds(off, tk), :]
    logits = lax.dot_general(kc, q2, NT_DIMS, preferred_element_type=F32)
    sc = jnp.maximum(logits[:, :tq], 0.0) * wt[0:1, :]
    for h in range(1, IDX_HEADS):
        sc = sc + jnp.maximum(logits[:, h * tq:(h + 1) * tq], 0.0) * wt[h:h + 1, :]
    causal = (off + lax.broadcasted_iota(jnp.int32, (tk, tq), 0)) <= qpos
    s_hi = jnp.where(causal, sc, -jnp.inf)
    s_lo = jnp.where(causal, sc, jnp.inf)
    score_ref[slot, pl.ds(off, tk), :] = s_hi
    for j in range(tk // ROWS):
        b_hi = s_hi[j * ROWS:(j + 1) * ROWS]
        mx = jnp.maximum(mx, b_hi)
        mn = jnp.minimum(mn, s_lo[j * ROWS:(j + 1) * ROWS])
        c0 = c0 + jnp.where(b_hi >= 0.0, 1.0, 0.0)
        cp = cp + jnp.where(b_hi > 0.0, 1.0, 0.0)
    return mx, mn, c0, cp


def _score_carry(tq):
    zero = jnp.zeros((ROWS, tq), F32)
    return zero - jnp.inf, zero + jnp.inf, zero, zero


def _select_threshold(score_ref, thr_ref, slot, nck, qpos, carry, *, tq, tk, ktop):
    mx, mn, c0, cp = carry
    hi0 = jnp.max(mx, axis=0, keepdims=True)
    lo0 = jnp.min(mn, axis=0, keepdims=True)
    n_ge0 = jnp.sum(c0, axis=0, keepdims=True)
    n_gt0 = jnp.sum(cp, axis=0, keepdims=True)
    zero = jnp.zeros((ROWS, tq), F32)

    def count(pred):
        def body(c, acc):
            off = pl.multiple_of(c * tk, tk)
            for j in range(tk // ROWS):
                x = score_ref[slot, pl.ds(pl.multiple_of(off + j * ROWS, ROWS), ROWS), :]
                acc = acc + jnp.where(pred(x), 1.0, 0.0)
            return acc
        return jnp.sum(lax.fori_loop(0, nck, body, zero), axis=0, keepdims=True)

    n_valid = (qpos + 1).astype(F32)
    kk = jnp.minimum(n_valid, float(ktop))

    all_in = n_valid <= kk
    above = jnp.logical_and(jnp.logical_not(all_in), n_gt0 >= kk)
    below = jnp.logical_and(jnp.logical_not(all_in), n_ge0 < kk)
    lo = jnp.where(all_in, lo0, jnp.where(below, lo0, 0.0))
    clo = jnp.where(all_in, n_valid, jnp.where(below, n_valid, n_ge0))
    hi = jnp.where(below, 0.0, hi0)
    fin = jnp.where(jnp.logical_or(above, below), 0.0, 1.0)

    def bis_cond(st):
        return jnp.logical_and(st[0] < MAX_BISECT, jnp.min(st[-1]) < 0.5)

    def bis_body(st):
        it, lo, hi, clo, chi, fin = st
        mid0 = 0.5 * lo + 0.5 * hi
        collapsed = jnp.logical_or(mid0 <= lo, mid0 >= hi)
        mid = jnp.where(collapsed, hi, mid0)
        c = count(lambda x: x >= mid)
        ge = c >= kk
        live = fin < 0.5
        up = jnp.logical_and(live, ge)
        dn = jnp.logical_and(live, jnp.logical_not(ge))
        lo = jnp.where(up, mid, lo)
        clo = jnp.where(up, c, clo)
        hi = jnp.where(dn, mid, hi)
        chi = jnp.where(dn, c, chi)
        done = jnp.logical_or(c == kk, collapsed)
        fin = jnp.where(jnp.logical_and(live, done), 1.0, fin)
        return it + 1, lo, hi, clo, chi, fin

    state = lax.fori_loop(0, UNCHECKED_BISECT, lambda _, st: bis_body(st),
                          (jnp.int32(0), lo, hi, clo, jnp.zeros_like(lo), fin))
    _, thr, hi, clo, chi, _ = lax.while_loop(bis_cond, bis_body, state)
    thr_ref[slot] = thr
    tied = clo > kk
    has_tie = jnp.max(jnp.where(tied, 1.0, 0.0)) > 0.5

    @pl.when(has_tie)
    def _():
        n_gt = jnp.where(jnp.logical_or(above, below),
                         jnp.where(thr >= hi, 0.0, chi), n_gt0)
        need = jnp.where(tied, kk - n_gt, float(2 ** 24))
        lower = (lax.broadcasted_iota(jnp.int32, (RANK_ROWS, RANK_ROWS), 0)
                 >= lax.broadcasted_iota(jnp.int32, (RANK_ROWS, RANK_ROWS), 1))
        lower = jnp.where(lower, 1.0, 0.0).astype(BF16)

        def body(c, need):
            rows = [pl.multiple_of(c * tk + j * RANK_ROWS, RANK_ROWS)
                    for j in range(tk // RANK_ROWS)]
            xs = [score_ref[slot, pl.ds(r0, RANK_ROWS), :] for r0 in rows]
            ranks = [jnp.dot(lower, jnp.where(x == thr, 1.0, 0.0).astype(BF16),
                             preferred_element_type=F32) for x in xs]
            for r0, x, rank in zip(rows, xs, ranks):
                kept = jnp.where(x > thr, 1.0,
                                 jnp.where(rank <= need, jnp.where(x == thr, 1.0, -1.0), -1.0))
                score_ref[slot, pl.ds(r0, RANK_ROWS), :] = kept
                need = need - rank[RANK_ROWS - 1:RANK_ROWS, :]
            return need
        lax.fori_loop(0, nck, body, need)
        thr_ref[slot] = jnp.zeros((1, tq), F32)


def _dsa_kernel(qi_ref, wt_ref, qin_ref, wtn_ref, kidx_ref, q_ref, k_ref, vt_ref, o_ref,
                score_ref, thr_ref, qpad_ref, m_ref, acc_ref, ot_ref, st_ref,
                *, tq, tk, ktop):
    qi = pl.program_id(1)
    slot = qi % 2
    nslot = 1 - slot
    chunks = lambda blk: ((blk + 1) * tq + tk - 1) // tk
    nck = chunks(qi)
    lanes_q = lax.broadcasted_iota(jnp.int32, (1, tq), 1)
    select = functools.partial(_select_threshold, score_ref, thr_ref, tq=tq, tk=tk, ktop=ktop)

    @pl.when(qi == 0)
    def _():
        q2, wt = _indexer_operands(qi_ref, wt_ref)

        def body(c, carry):
            return _score_chunk(kidx_ref, score_ref, 0, q2, wt, lanes_q,
                                pl.multiple_of(c * tk, tk), carry, tq=tq, tk=tk)
        select(0, nck, lanes_q, lax.fori_loop(0, nck, body, _score_carry(tq)))

    _init_streams(q_ref, qpad_ref, m_ref, acc_ref)
    thr_sel = thr_ref[slot]

    def attend(off, between=None):
        bias = jnp.where(score_ref[slot, pl.ds(off, tk), :] >= thr_sel, 0.0, MASKED)
        return _attend_chunk(k_ref, vt_ref, qpad_ref, m_ref, acc_ref, st_ref, off, bias,
                             tkv=tk, dv=HEAD_DIM, v_head=lambda s: s, den_on_mxu=True,
                             between=between)

    has_next = qi + 1 < pl.num_programs(1)

    @pl.when(has_next)
    def _():
        q2, wt = _indexer_operands(qin_ref, wtn_ref)
        qpos = (qi + 1) * tq + lanes_q
        score = functools.partial(_score_chunk, kidx_ref, score_ref, nslot, q2, wt, qpos,
                                  tq=tq, tk=tk)

        def paired(c, carry):
            off = pl.multiple_of(c * tk, tk)
            return attend(off, between=lambda: score(off, carry))
        carry = lax.fori_loop(0, nck, paired, _score_carry(tq))
        nck_next = chunks(qi + 1)
        carry = lax.fori_loop(nck, nck_next,
                              lambda c, carry: score(pl.multiple_of(c * tk, tk), carry), carry)
        select(nslot, nck_next, qpos, carry)

    @pl.when(jnp.logical_not(has_next))
    def _():
        def body(c, _):
            attend(pl.multiple_of(c * tk, tk))
            return 0
        lax.fori_loop(0, nck, body, 0)

    for s in range(N_STREAMS):
        ot_ref[s * HEAD_DIM:(s + 1) * HEAD_DIM, :] = _normalised(acc_ref, s, HEAD_DIM)
    o_ref[0] = ot_ref[...].T.astype(o_ref.dtype)


def _resident(shape):
    return pl.BlockSpec(shape, lambda b, i: (b, 0, 0), pipeline_mode=pl.Buffered(1))


def _dsa_branch(q_i, w_it, k_i, q, k, vt, tq=256, tk=512):
    B, S, W = q.shape
    nq = S // tq
    ktop = min(TOPK_MAX, S // 4)
    kern = functools.partial(_dsa_kernel, tq=tq, tk=tk, ktop=ktop)
    nxt = lambda i: jnp.minimum(i + 1, nq - 1)
    return pl.pallas_call(
        kern,
        grid=(B, nq),
        in_specs=[pl.BlockSpec((1, tq, IDX_QW), lambda b, i: (b, i, 0)),
                  pl.BlockSpec((1, IDX_HEADS, tq), lambda b, i: (b, 0, i)),
                  pl.BlockSpec((1, tq, IDX_QW), lambda b, i: (b, nxt(i), 0)),
                  pl.BlockSpec((1, IDX_HEADS, tq), lambda b, i: (b, 0, nxt(i))),
                  _resident((1, S, IDX_DIM)),
                  pl.BlockSpec((1, tq, W), lambda b, i: (b, i, 0)),
                  _resident((1, S, W)), _resident((1, W, S))],
        out_specs=pl.BlockSpec((1, tq, W), lambda b, i: (b, i, 0)),
        out_shape=jax.ShapeDtypeStruct((B, S, W), BF16),
        scratch_shapes=[pltpu.VMEM((2, S, tq), F32), pltpu.VMEM((2, 1, tq), F32),
                        pltpu.VMEM((N_STREAMS, tq, LANES), BF16),
                        pltpu.VMEM((N_STREAMS, 1, tq), F32),
                        pltpu.VMEM((N_STREAMS, HEAD_DIM + DEN_ROWS, tq), F32),
                        pltpu.VMEM((W, tq), F32),
                        pltpu.VMEM((N_STREAMS, tk, tq), F32)],
        compiler_params=_cparams(("arbitrary", "arbitrary")),
        name="dsa_branch",
    )(q_i, w_it, q_i, w_it, k_i, q, k, vt)


def _diff_kernel(q_ref, k_ref, vt_ref, lq1_ref, lk1_ref, lq2_ref, lk2_ref, g_ref, o_ref,
                 qpad_ref, m_ref, acc_ref, ot_ref, st_ref, *, tq, tkv, lam_init):
    qi = pl.program_id(1)
    nck = ((qi + 1) * tq + tkv - 1) // tkv
    nfull = (qi * tq + 1) // tkv
    lam = (jnp.exp(jnp.sum(lq1_ref[...] * lk1_ref[...], axis=-1, keepdims=True))
           - jnp.exp(jnp.sum(lq2_ref[...] * lk2_ref[...], axis=-1, keepdims=True)) + lam_init)
    _init_streams(q_ref, qpad_ref, m_ref, acc_ref)
    attend = functools.partial(_attend_chunk, k_ref, vt_ref, qpad_ref, m_ref, acc_ref,
                               st_ref, tkv=tkv, dv=DIFF_VDIM, v_head=lambda s: s // 2,
                               den_on_mxu=False)

    def full_body(c, _):
        attend(pl.multiple_of(c * tkv, tkv), None)
        return 0
    lax.fori_loop(0, nfull, full_body, 0)

    qpos = qi * tq + lax.broadcasted_iota(jnp.int32, (1, tq), 1)
    krow = lax.broadcasted_iota(jnp.int32, (tkv, tq), 0)

    def diag_body(c, _):
        off = pl.multiple_of(c * tkv, tkv)
        attend(off, jnp.where((off + krow) <= qpos, 0.0, MASKED))
        return 0
    lax.fori_loop(nfull, nck, diag_body, 0)

    for h in range(DIFF_HEADS):
        o1 = _normalised(acc_ref, 2 * h, DIFF_VDIM)
        o2 = _normalised(acc_ref, 2 * h + 1, DIFF_VDIM)
        ot_ref[h * DIFF_VDIM:(h + 1) * DIFF_VDIM, :] = o1 - lam * o2
    o = ot_ref[...].T
    for h in range(DIFF_HEADS):
        blk = _rmsnorm(o[:, h * DIFF_VDIM:(h + 1) * DIFF_VDIM], g_ref[...], DIFF_SUBLN_EPS)
        o_ref[0, :, h * DIFF_VDIM:(h + 1) * DIFF_VDIM] = (blk * (1.0 - lam_init)).astype(o_ref.dtype)


def _diff_branch(q, k, vt, lq1, lk1, lq2, lk2, g, lam_init, tq=512, tkv=512):
    B, S, W = q.shape
    kern = functools.partial(_diff_kernel, tq=tq, tkv=tkv, lam_init=lam_init)
    vec = lambda a: pl.BlockSpec(a.shape, lambda b, i: (0, 0))
    return pl.pallas_call(
        kern,
        grid=(B, S // tq),
        in_specs=[pl.BlockSpec((1, tq, W), lambda b, i: (b, i, 0)),
                  _resident((1, S, W)), _resident((1, W, S)),
                  vec(lq1), vec(lk1), vec(lq2), vec(lk2), vec(g)],
        out_specs=pl.BlockSpec((1, tq, W), lambda b, i: (b, i, 0)),
        out_shape=jax.ShapeDtypeStruct((B, S, W), BF16),
        scratch_shapes=[pltpu.VMEM((N_STREAMS, tq, LANES), BF16),
                        pltpu.VMEM((N_STREAMS, 1, tq), F32),
                        pltpu.VMEM((N_STREAMS, DIFF_VDIM + DEN_ROWS, tq), F32),
                        pltpu.VMEM((W, tq), F32),
                        pltpu.VMEM((N_STREAMS, tkv, tq), F32)],
        compiler_params=_cparams(("parallel", "arbitrary")),
        name="diff_branch",
    )(q, k, vt, lq1, lk1, lq2, lk2, g)


def _merge_kernel(x_ref, oa_ref, ob_ref, g_ref, wg_ref, gb_ref, wa_ref, wb_ref, wo_ref, y_ref):
    x = x_ref[...]
    h = _rmsnorm(x, g_ref[...], NORM_EPS).astype(BF16)
    gates = jax.nn.sigmoid(jnp.dot(h, wg_ref[...], preferred_element_type=F32) + gb_ref[...])
    pa = jnp.dot(oa_ref[...], wa_ref[...], preferred_element_type=F32)
    pb = jnp.dot(ob_ref[...], wb_ref[...], preferred_element_type=F32)
    merged = gates[:, :D_MODEL] * pa + gates[:, D_MODEL:] * pb
    y_ref[...] = x + jnp.dot(merged.astype(BF16), wo_ref[...], preferred_element_type=F32)


def _merge_project(x2, oa, ob, g, wg, gb, wa, wb, wo, tm=512):
    M = x2.shape[0]
    row = lambda w: pl.BlockSpec((tm, w), lambda i: (i, 0))
    full = lambda a: pl.BlockSpec(a.shape, lambda i: (0,) * a.ndim, pipeline_mode=pl.Buffered(1))
    return pl.pallas_call(
        _merge_kernel,
        grid=(M // tm,),
        in_specs=[row(D_MODEL), row(512), row(512), full(g), full(wg), full(gb),
                  full(wa), full(wb), full(wo)],
        out_specs=row(D_MODEL),
        out_shape=jax.ShapeDtypeStruct((M, D_MODEL), F32),
        compiler_params=_cparams(("parallel",)),
        name="merge_project",
    )(x2, oa, ob, g, wg, gb, wa, wb, wo)


def _ffn_kernel(x_ref, g_ref, win_ref, wdown_ref, gf_ref, y_ref, *, final_norm):
    x = x_ref[...]
    h = _rmsnorm(x, g_ref[...], NORM_EPS).astype(BF16)
    gate = jnp.dot(h, win_ref[:, :D_FF], preferred_element_type=F32)
    up = jnp.dot(h, win_ref[:, D_FF:], preferred_element_type=F32)
    act = (gate * jax.nn.sigmoid(gate) * up).astype(BF16)
    y = x + jnp.dot(act, wdown_ref[...], preferred_element_type=F32)
    if final_norm:
        y = _rmsnorm(y, gf_ref[...], NORM_EPS)
    y_ref[...] = y


def _ffn(x2, g, win, wdown, gf, final_norm, tm=512):
    M = x2.shape[0]
    row = lambda w: pl.BlockSpec((tm, w), lambda i: (i, 0))
    full = lambda a: pl.BlockSpec(a.shape, lambda i: (0,) * a.ndim, pipeline_mode=pl.Buffered(1))
    return pl.pallas_call(
        functools.partial(_ffn_kernel, final_norm=final_norm),
        grid=(M // tm,),
        in_specs=[row(D_MODEL), full(g), full(win), full(wdown), full(gf)],
        out_specs=row(D_MODEL),
        out_shape=jax.ShapeDtypeStruct((M, D_MODEL), F32),
        compiler_params=_cparams(("parallel",)),
        name="swiglu_ffn",
    )(x2, g, win, wdown, gf)


def _inv_freq(rot_dim):
    return jnp.power(jnp.float32(ROPE_THETA), -jnp.arange(0, rot_dim, 2, dtype=F32) / rot_dim)


def _inv_freq_column():
    inv = jnp.concatenate([_inv_freq(ROT_DIM_HEAD), _inv_freq(ROT_DIM_IDX)])
    return jnp.pad(inv, (0, TRIG_ROWS - inv.shape[0]))[:, None]


def _pad_lanes(v):
    return jnp.pad(v.astype(F32), (0, LANES - v.shape[0]))[None, :]


def kernel(x, positions, norm_mix_g, w_in, idx_k_norm_g, idx_k_norm_b, diff_lambda_q1,
           diff_lambda_k1, diff_lambda_q2, diff_lambda_k2, diff_subln_g, gate_b, w_branch_dsa,
           w_branch_diff, w_out, norm_ffn_g, w_ffn_in, w_ffn_out, norm_final_g):
    B, S, D = x.shape
    M = B * S
    depth = w_in.shape[0]
    if depth == 0:
        raise ValueError("depth must be positive")
    pos_t = positions.astype(F32)[:, None, :]
    inv_col = _inv_freq_column()
    paired = _paired_columns(2 * DSA_WIDTH)
    row = lambda v: v.astype(F32)[None, :]

    n_idx = IDX_QW + IDX_DIM + IDX_HEADS
    o_a = 0
    o_i = 3 * DSA_WIDTH
    o_b = o_i + n_idx
    o_g = o_b + 2 * DIFF_QK_WIDTH + DIFF_WIDTH

    x2 = x.reshape(M, D)
    for l in range(depth):
        lam_init = 0.8 - 0.6 * math.exp(-0.3 * l)
        wl = w_in[l]
        wa = wl[:, o_a:o_a + 2 * DSA_WIDTH][:, paired].astype(BF16)
        wavt = wl[:, o_a + 2 * DSA_WIDTH:o_a + 3 * DSA_WIDTH].T.astype(BF16)
        wi = jnp.pad(wl[:, o_i:o_i + n_idx], ((0, 0), (0, IDX_QW + LANES - n_idx))).astype(BF16)
        wb = wl[:, o_b:o_b + 2 * DIFF_QK_WIDTH][:, paired].astype(BF16)
        wbvt = wl[:, o_b + 2 * DIFF_QK_WIDTH:o_b + 2 * DIFF_QK_WIDTH + DIFF_WIDTH].T.astype(BF16)
        wg = wl[:, o_g:o_g + N_BRANCH * D_MODEL].astype(BF16)

        (q_a, k_a, vt_a, q_b, k_b, vt_b, q_i, k_i, w_it) = _in_projection(
            x2.reshape(B, S, D), pos_t, row(norm_mix_g[l]), wa, wavt, wb, wbvt, wi, inv_col,
            _pad_lanes(idx_k_norm_g[l]), _pad_lanes(idx_k_norm_b[l]))

        o_dsa = _dsa_branch(q_i, w_it, k_i, q_a, k_a, vt_a)
        o_diff = _diff_branch(q_b, k_b, vt_b,
                              row(diff_lambda_q1[l]), row(diff_lambda_k1[l]),
                              row(diff_lambda_q2[l]), row(diff_lambda_k2[l]),
                              row(diff_subln_g[l]), lam_init)

        x2 = _merge_project(x2, o_dsa.reshape(M, DSA_WIDTH), o_diff.reshape(M, DIFF_WIDTH),
                            row(norm_mix_g[l]), wg, row(gate_b[l]),
                            w_branch_dsa[l].astype(BF16), w_branch_diff[l].astype(BF16),
                            w_out[l].astype(BF16))
        x2 = _ffn(x2, row(norm_ffn_g[l]), w_ffn_in[l].astype(BF16), w_ffn_out[l].astype(BF16),
                  row(norm_final_g), final_norm=(l == depth - 1))

    return x2.reshape(B, S, D)
```

```python
import functools
import math

import jax
import jax.numpy as jnp
from jax import lax
from jax.experimental import pallas as pl
from jax.experimental.pallas import tpu as pltpu

F32 = jnp.float32
BF16 = jnp.bfloat16

D_MODEL = 1024
HEAD_DIM = 64
DSA_HEADS = 8
DSA_WIDTH = DSA_HEADS * HEAD_DIM
IDX_HEADS = 8
IDX_DIM = 32
IDX_QW = IDX_HEADS * IDX_DIM
TOPK_MAX = 256
DIFF_HEADS = 4
DIFF_DIM = 64
DIFF_VDIM = 2 * DIFF_DIM
DIFF_QK_WIDTH = DIFF_HEADS * 2 * DIFF_DIM
DIFF_WIDTH = DIFF_HEADS * DIFF_VDIM
DIFF_SUBLN_EPS = 1e-5
N_BRANCH = 2
ROPE_THETA = 500000.0
ROT_DIM_HEAD = HEAD_DIM // 4
ROT_DIM_IDX = IDX_DIM // 4
FFN_MULT = 256
D_FF = -(-8 * D_MODEL // (3 * FFN_MULT)) * FFN_MULT
NORM_EPS = 1e-6

LANES = 128
VMEM_LIMIT = 56 * 1024 * 1024
MASKED = -1e30
MAX_BISECT = 512
UNCHECKED_BISECT = 16
LOG2E = 1.4426950408889634
N_STREAMS = 8
ROWS = 64
RANK_ROWS = 256
DEN_ROWS = 16
TRIG_ROWS = 16
TRIG_ONE = 12
TRIG_ZERO = TRIG_ROWS + 12

NT_DIMS = (((1,), (1,)), ((), ()))


def _cparams(sem, flags=None):
    return pltpu.CompilerParams(dimension_semantics=sem, vmem_limit_bytes=VMEM_LIMIT, flags=flags)


def _rmsnorm(x, g, eps):
    return x * lax.rsqrt(jnp.mean(x * x, axis=-1, keepdims=True) + eps) * g


def _rope_block(y, cos, sin_lo, sin_hi, half):
    return (y * cos + pltpu.roll(y, half, 1) * sin_hi
            + pltpu.roll(y, LANES - half, 1) * sin_lo)


def _trig_table(pos_t, inv_col):
    ang_t = inv_col * pos_t
    pad = jnp.zeros((LANES - 2 * TRIG_ROWS, ang_t.shape[1]), F32)
    return jnp.concatenate([jnp.cos(ang_t), jnp.sin(ang_t), pad], axis=0).T


def _rope_tables(tbl, first, period, half):
    d = lax.broadcasted_iota(jnp.int32, tbl.shape, 1) % period
    rot = d < 2 * half
    take = lambda idx: jnp.take_along_axis(tbl, idx, axis=1)
    cos = take(jnp.where(rot, first + d % half, TRIG_ONE))
    sin_lo = -take(jnp.where(d < half, TRIG_ROWS + first + d, TRIG_ZERO))
    sin_hi = take(jnp.where(rot & (d >= half), TRIG_ROWS + first + d % half, TRIG_ZERO))
    return cos, sin_lo, sin_hi


def _rope_tables_paired(tbl):
    d = lax.broadcasted_iota(jnp.int32, tbl.shape, 1)
    x1 = d < ROT_DIM_HEAD
    rot = x1 | ((d >= LANES // 2) & (d < LANES // 2 + ROT_DIM_HEAD))
    f = d % (ROT_DIM_HEAD // 2)
    cos = jnp.take_along_axis(tbl, jnp.where(rot, f, TRIG_ONE), axis=1)
    sin = jnp.take_along_axis(tbl, jnp.where(rot, TRIG_ROWS + f, TRIG_ZERO), axis=1)
    return cos, jnp.where(x1, -sin, sin)


def _paired_head_lanes(second):
    lane = lax.broadcasted_iota(jnp.int32, (1, LANES), 1)
    half = ROT_DIM_HEAD // 2
    in_second = ((lane >= half) & (lane < 2 * half)) | (lane >= LANES // 2 + half)
    return in_second if second else jnp.logical_not(in_second)


def _paired_columns(width):
    half = ROT_DIM_HEAD // 2
    perm = list(range(LANES))
    perm[half:2 * half] = range(HEAD_DIM, HEAD_DIM + half)
    perm[LANES // 2:LANES // 2 + half] = range(half, 2 * half)
    return jnp.asarray([(c // LANES) * LANES + perm[c % LANES] for c in range(width)], jnp.int32)


def _inproj_kernel(x_ref, post_ref, g_ref, wa_ref, wavt_ref, wb_ref, wbvt_ref, wi_ref, invcol_ref,
                   lng_ref, lnb_ref,
                   qa_ref, ka_ref, vat_ref, qb_ref, kb_ref, vbt_ref, qi_ref, ki_ref, wit_ref):
    h = _rmsnorm(x_ref[0], g_ref[...], NORM_EPS).astype(BF16)
    tbl = _trig_table(post_ref[0], invcol_ref[...])
    cos_k, sin_k = _rope_tables_paired(tbl)
    cos_i, slo_i, shi_i = _rope_tables(tbl, ROT_DIM_HEAD // 2, IDX_DIM, ROT_DIM_IDX // 2)
    q_scale = HEAD_DIM ** -0.5 * LOG2E
    cos_q, sin_q = cos_k * q_scale, sin_k * q_scale

    def branch(w_ref, wvt_ref, q_ref, k_ref, vt_ref):
        for grp, (o_ref, cos, sin) in enumerate(((q_ref, cos_q, sin_q), (k_ref, cos_k, sin_k))):
            y = jnp.dot(h, w_ref[:, grp * 512:(grp + 1) * 512], preferred_element_type=F32)
            for j in range(512 // LANES):
                blk = y[:, j * LANES:(j + 1) * LANES]
                blk = blk * cos + pltpu.roll(blk, LANES // 2, 1) * sin
                o_ref[0, :, j * LANES:(j + 1) * LANES] = blk.astype(o_ref.dtype)
        vt_ref[0] = lax.dot_general(wvt_ref[...], h, NT_DIMS,
                                    preferred_element_type=F32).astype(vt_ref.dtype)

    branch(wa_ref, wavt_ref, qa_ref, ka_ref, vat_ref)
    branch(wb_ref, wbvt_ref, qb_ref, kb_ref, vbt_ref)

    r = jnp.dot(h, wi_ref[...], preferred_element_type=F32)
    for j in range(IDX_QW // LANES):
        blk = _rope_block(r[:, j * LANES:(j + 1) * LANES], cos_i, slo_i, shi_i, ROT_DIM_IDX // 2)
        qi_ref[0, :, j * LANES:(j + 1) * LANES] = blk.astype(qi_ref.dtype)
    kw = r[:, IDX_QW:IDX_QW + LANES]
    lane = lax.broadcasted_iota(jnp.int32, (1, LANES), 1)
    is_k = lane < IDX_DIM
    mu = jnp.sum(jnp.where(is_k, kw, 0.0), axis=-1, keepdims=True) * (1.0 / IDX_DIM)
    xc = jnp.where(is_k, kw - mu, 0.0)
    var = jnp.sum(xc * xc, axis=-1, keepdims=True) * (1.0 / IDX_DIM)
    kn = xc * lax.rsqrt(var + NORM_EPS) * lng_ref[...] + lnb_ref[...]
    kn = _rope_block(kn, cos_i, slo_i, shi_i, ROT_DIM_IDX // 2)
    ki_ref[0] = kn[:, :IDX_DIM].astype(ki_ref.dtype)
    w_scale = IDX_HEADS ** -0.5 * IDX_DIM ** -0.5
    wit_ref[0] = kw.T[IDX_DIM:IDX_DIM + IDX_HEADS, :] * w_scale


def _in_projection(x, pos_t, g, wa, wavt, wb, wbvt, wi, inv_col, lng, lnb, tm=256):
    B, S, _ = x.shape
    row = lambda w: pl.BlockSpec((1, tm, w), lambda b, i: (b, i, 0))
    col = lambda r: pl.BlockSpec((1, r, tm), lambda b, i: (b, 0, i))
    full = lambda a: pl.BlockSpec(a.shape, lambda b, i: (0,) * a.ndim,
                                  pipeline_mode=pl.Buffered(1))
    sds = jax.ShapeDtypeStruct
    out_shape = [sds((B, S, 512), BF16), sds((B, S, 512), BF16), sds((B, 512, S), BF16),
                 sds((B, S, 512), BF16), sds((B, S, 512), BF16), sds((B, 512, S), BF16),
                 sds((B, S, IDX_QW), BF16), sds((B, S, IDX_DIM), BF16), sds((B, IDX_HEADS, S), F32)]
    return pl.pallas_call(
        _inproj_kernel,
        grid=(B, S // tm),
        in_specs=[row(D_MODEL), col(1), full(g), full(wa), full(wavt), full(wb), full(wbvt),
                  full(wi), full(inv_col), full(lng), full(lnb)],
        out_specs=[row(512), row(512), col(512), row(512), row(512), col(512),
                   row(IDX_QW), row(IDX_DIM), col(IDX_HEADS)],
        out_shape=out_shape,
        compiler_params=_cparams(("parallel", "parallel")),
        name="in_projection",
    )(x, pos_t, g, wa, wavt, wb, wbvt, wi, inv_col, lng, lnb)


def _init_streams(q_ref, qpad_ref, m_ref, acc_ref):
    for s in range(N_STREAMS):
        blk = q_ref[0, :, (s // 2) * LANES:(s // 2 + 1) * LANES]
        qpad_ref[s] = jnp.where(_paired_head_lanes(s % 2 == 1), blk, jnp.zeros_like(blk))
    m_ref[...] = jnp.full(m_ref.shape, MASKED, F32)
    acc_ref[...] = jnp.zeros(acc_ref.shape, F32)


def _attend_chunk(k_ref, vt_ref, qpad_ref, m_ref, acc_ref, st_ref, off, bias,
                  *, tkv, dv, v_head, den_on_mxu, between=None):
    col_max = []
    for s in range(N_STREAMS):
        kp = k_ref[0, pl.ds(off, tkv), (s // 2) * LANES:(s // 2 + 1) * LANES]
        st = lax.dot_general(kp, qpad_ref[s], NT_DIMS, preferred_element_type=F32)
        if bias is not None:
            st = st + bias
        st_ref[s] = st
        col_max.append(jnp.max(st, axis=0, keepdims=True))
    extra = None if between is None else between()
    for s in range(N_STREAMS):
        m_old = m_ref[s]
        m_new = jnp.maximum(m_old, col_max[s])
        alpha = jnp.exp2(m_old - m_new)
        r0 = v_head(s) * dv
        v_c = vt_ref[0, r0:r0 + dv, pl.ds(off, tkv)]
        if den_on_mxu:
            p = jnp.exp2((st_ref[s] - m_new).astype(BF16))
            v_aug = jnp.concatenate([v_c, jnp.ones((DEN_ROWS, tkv), BF16)], axis=0)
            acc_ref[s] = alpha * acc_ref[s] + jnp.dot(v_aug, p, preferred_element_type=F32)
        else:
            p = jnp.exp2(st_ref[s] - m_new)
            acc_ref[s, 0:dv, :] = alpha * acc_ref[s, 0:dv, :] + jnp.dot(
                v_c, p.astype(BF16), preferred_element_type=F32)
            acc_ref[s, dv:dv + 1, :] = (alpha * acc_ref[s, dv:dv + 1, :]
                                        + jnp.sum(p, axis=0, keepdims=True))
        m_ref[s] = m_new
    return extra


def _normalised(acc_ref, s, dv):
    return acc_ref[s, 0:dv, :] / acc_ref[s, dv:dv + 1, :]


def _indexer_operands(qi_ref, wt_ref):
    qx = qi_ref[0]
    q2 = jnp.concatenate([qx[:, h * IDX_DIM:(h + 1) * IDX_DIM] for h in range(IDX_HEADS)], axis=0)
    return q2, wt_ref[0]


def _score_chunk(kidx_ref, score_ref, slot, q2, wt, qpos, off, carry, *, tq, tk):
    mx, mn, c0, cp = carry
    kc = kidx_ref[0, pl.ds(off, tk), :]
    logits = lax.dot_general(kc, q2, NT_DIMS, preferred_element_type=F32)
    sc = jnp.maximum(logits[:, :tq], 0.0) * wt[0:1, :]
    for h in range(1, IDX_HEADS):
        sc = sc + jnp.maximum(logits[:, h * tq:(h + 1) * tq], 0.0) * wt[h:h + 1, :]
    causal = (off + lax.broadcasted_iota(jnp.int32, (tk, tq), 0)) <= qpos
    s_hi = jnp.where(causal, sc, -jnp.inf)
    s_lo = jnp.where(causal, sc, jnp.inf)
    score_ref[slot, pl.ds(off, tk), :] = s_hi
    for j in range(tk // ROWS):
        b_hi = s_hi[j * ROWS:(j + 1) * ROWS]
        mx = jnp.maximum(mx, b_hi)
        mn = jnp.minimum(mn, s_lo[j * ROWS:(j + 1) * ROWS])
        c0 = c0 + jnp.where(b_hi >= 0.0, 1.0, 0.0)
        cp = cp + jnp.where(b_hi > 0.0, 1.0, 0.0)
    return mx, mn, c0, cp


def _score_carry(tq):
    zero = jnp.zeros((ROWS, tq), F32)
    return zero - jnp.inf, zero + jnp.inf, zero, zero


def _select_threshold(score_ref, thr_ref, slot, nck, qpos, carry, *, tq, tk, ktop):
    mx, mn, c0, cp = carry
    hi0 = jnp.max(mx, axis=0, keepdims=True)
    lo0 = jnp.min(mn, axis=0, keepdims=True)
    n_ge0 = jnp.sum(c0, axis=0, keepdims=True)
    n_gt0 = jnp.sum(cp, axis=0, keepdims=True)
    zero = jnp.zeros((ROWS, tq), F32)

    def count(pred):
        def body(c, acc):
            off = pl.multiple_of(c * tk, tk)
            for j in range(tk // ROWS):
                x = score_ref[slot, pl.ds(pl.multiple_of(off + j * ROWS, ROWS), ROWS), :]
                acc = acc + jnp.where(pred(x), 1.0, 0.0)
            return acc
        return jnp.sum(lax.fori_loop(0, nck, body, zero), axis=0, keepdims=True)

    n_valid = (qpos + 1).astype(F32)
    kk = jnp.minimum(n_valid, float(ktop))

    all_in = n_valid <= kk
    above = jnp.logical_and(jnp.logical_not(all_in), n_gt0 >= kk)
    below = jnp.logical_and(jnp.logical_not(all_in), n_ge0 < kk)
    lo = jnp.where(all_in, lo0, jnp.where(below, lo0, 0.0))
    clo = jnp.where(all_in, n_valid, jnp.where(below, n_valid, n_ge0))
    hi = jnp.where(below, 0.0, hi0)
    fin = jnp.where(jnp.logical_or(above, below), 0.0, 1.0)

    def bis_cond(st):
        return jnp.logical_and(st[0] < MAX_BISECT, jnp.min(st[-1]) < 0.5)

    def bis_body(st):
        it, lo, hi, clo, chi, fin = st
        mid0 = 0.5 * lo + 0.5 * hi
        collapsed = jnp.logical_or(mid0 <= lo, mid0 >= hi)
        mid = jnp.where(collapsed, hi, mid0)
        c = count(lambda x: x >= mid)
        ge = c >= kk
        live = fin < 0.5
        up = jnp.logical_and(live, ge)
        dn = jnp.logical_and(live, jnp.logical_not(ge))
        lo = jnp.where(up, mid, lo)
        clo = jnp.where(up, c, clo)
        hi = jnp.where(dn, mid, hi)
        chi = jnp.where(dn, c, chi)
        done = jnp.logical_or(c == kk, collapsed)
        fin = jnp.where(jnp.logical_and(live, done), 1.0, fin)
        return it + 1, lo, hi, clo, chi, fin

    state = lax.fori_loop(0, UNCHECKED_BISECT, lambda _, st: bis_body(st),
                          (jnp.int32(0), lo, hi, clo, jnp.zeros_like(lo), fin))
    _, thr, hi, clo, chi, _ = lax.while_loop(bis_cond, bis_body, state)
    thr_ref[slot] = thr
    tied = clo > kk
    has_tie = jnp.max(jnp.where(tied, 1.0, 0.0)) > 0.5

    @pl.when(has_tie)
    def _():
        n_gt = jnp.where(jnp.logical_or(above, below),
                         jnp.where(thr >= hi, 0.0, chi), n_gt0)
        need = jnp.where(tied, kk - n_gt, float(2 ** 24))
        lower = (lax.broadcasted_iota(jnp.int32, (RANK_ROWS, RANK_ROWS), 0)
                 >= lax.broadcasted_iota(jnp.int32, (RANK_ROWS, RANK_ROWS), 1))
        lower = jnp.where(lower, 1.0, 0.0).astype(BF16)

        def body(c, need):
            rows = [pl.multiple_of(c * tk + j * RANK_ROWS, RANK_ROWS)
                    for j in range(tk // RANK_ROWS)]
            xs = [score_ref[slot, pl.ds(r0, RANK_ROWS), :] for r0 in rows]
            ranks = [jnp.dot(lower, jnp.where(x == thr, 1.0, 0.0).astype(BF16),
                             preferred_element_type=F32) for x in xs]
            for r0, x, rank in zip(rows, xs, ranks):
                kept = jnp.where(x > thr, 1.0,
                                 jnp.where(rank <= need, jnp.where(x == thr, 1.0, -1.0), -1.0))
                score_ref[slot, pl.ds(r0, RANK_ROWS), :] = kept
                need = need - rank[RANK_ROWS - 1:RANK_ROWS, :]
            return need
        lax.fori_loop(0, nck, body, need)
        thr_ref[slot] = jnp.zeros((1, tq), F32)


def _dsa_kernel(qi_ref, wt_ref, qin_ref, wtn_ref, kidx_ref, q_ref, k_ref, vt_ref, o_ref,
                score_ref, thr_ref, qpad_ref, m_ref, acc_ref, st_ref,
                *, tq, tk, ktop):
    qi = pl.program_id(1)
    slot = qi % 2
    nslot = 1 - slot
    chunks = lambda blk: ((blk + 1) * tq + tk - 1) // tk
    nck = chunks(qi)
    lanes_q = lax.broadcasted_iota(jnp.int32, (1, tq), 1)
    select = functools.partial(_select_threshold, score_ref, thr_ref, tq=tq, tk=tk, ktop=ktop)

    @pl.when(qi == 0)
    def _():
        q2, wt = _indexer_operands(qi_ref, wt_ref)

        def body(c, carry):
            return _score_chunk(kidx_ref, score_ref, 0, q2, wt, lanes_q,
                                pl.multiple_of(c * tk, tk), carry, tq=tq, tk=tk)
        select(0, nck, lanes_q, lax.fori_loop(0, nck, body, _score_carry(tq)))

    _init_streams(q_ref, qpad_ref, m_ref, acc_ref)
    thr_sel = thr_ref[slot]

    def attend(off, between=None):
        bias = jnp.where(score_ref[slot, pl.ds(off, tk), :] >= thr_sel, 0.0, MASKED)
        return _attend_chunk(k_ref, vt_ref, qpad_ref, m_ref, acc_ref, st_ref, off, bias,
                             tkv=tk, dv=HEAD_DIM, v_head=lambda s: s, den_on_mxu=True,
                             between=between)

    has_next = qi + 1 < pl.num_programs(1)

    @pl.when(has_next)
    def _():
        q2, wt = _indexer_operands(qin_ref, wtn_ref)
        qpos = (qi + 1) * tq + lanes_q
        score = functools.partial(_score_chunk, kidx_ref, score_ref, nslot, q2, wt, qpos,
                                  tq=tq, tk=tk)

        def paired(c, carry):
            off = pl.multiple_of(c * tk, tk)
            return attend(off, between=lambda: score(off, carry))
        carry = lax.fori_loop(0, nck, paired, _score_carry(tq))
        nck_next = chunks(qi + 1)
        carry = lax.fori_loop(nck, nck_next,
                              lambda c, carry: score(pl.multiple_of(c * tk, tk), carry), carry)
        select(nslot, nck_next, qpos, carry)

    @pl.when(jnp.logical_not(has_next))
    def _():
        def body(c, _):
            attend(pl.multiple_of(c * tk, tk))
            return 0
        lax.fori_loop(0, nck, body, 0)

    for s in range(N_STREAMS):
        o_ref[0, s * HEAD_DIM:(s + 1) * HEAD_DIM, :] = _normalised(
            acc_ref, s, HEAD_DIM).astype(o_ref.dtype)


def _resident(shape):
    return pl.BlockSpec(shape, lambda b, i: (b, 0, 0), pipeline_mode=pl.Buffered(1))


def _dsa_branch(q_i, w_it, k_i, q, k, vt, tq=256, tk=512):
    B, S, W = q.shape
    nq = S // tq
    ktop = min(TOPK_MAX, S // 4)
    kern = functools.partial(_dsa_kernel, tq=tq, tk=tk, ktop=ktop)
    nxt = lambda i: jnp.minimum(i + 1, nq - 1)
    return pl.pallas_call(
        kern,
        grid=(B, nq),
        in_specs=[pl.BlockSpec((1, tq, IDX_QW), lambda b, i: (b, i, 0)),
                  pl.BlockSpec((1, IDX_HEADS, tq), lambda b, i: (b, 0, i)),
                  pl.BlockSpec((1, tq, IDX_QW), lambda b, i: (b, nxt(i), 0)),
                  pl.BlockSpec((1, IDX_HEADS, tq), lambda b, i: (b, 0, nxt(i))),
                  _resident((1, S, IDX_DIM)),
                  pl.BlockSpec((1, tq, W), lambda b, i: (b, i, 0)),
                  _resident((1, S, W)), _resident((1, W, S))],
        out_specs=pl.BlockSpec((1, W, tq), lambda b, i: (b, 0, i)),
        out_shape=jax.ShapeDtypeStruct((B, W, S), BF16),
        scratch_shapes=[pltpu.VMEM((2, S, tq), F32), pltpu.VMEM((2, 1, tq), F32),
                        pltpu.VMEM((N_STREAMS, tq, LANES), BF16),
                        pltpu.VMEM((N_STREAMS, 1, tq), F32),
                        pltpu.VMEM((N_STREAMS, HEAD_DIM + DEN_ROWS, tq), F32),
                        pltpu.VMEM((N_STREAMS, tk, tq), F32)],
        compiler_params=_cparams(("arbitrary", "arbitrary")),
        name="dsa_branch",
    )(q_i, w_it, q_i, w_it, k_i, q, k, vt)


def _diff_kernel(q_ref, k_ref, vt_ref, lq1_ref, lk1_ref, lq2_ref, lk2_ref, g_ref, o_ref,
                 qpad_ref, m_ref, acc_ref, ot_ref, st_ref, *, tq, tkv, lam_init):
    qi = pl.program_id(1)
    nck = ((qi + 1) * tq + tkv - 1) // tkv
    nfull = (qi * tq + 1) // tkv
    lam = (jnp.exp(jnp.sum(lq1_ref[...] * lk1_ref[...], axis=-1, keepdims=True))
           - jnp.exp(jnp.sum(lq2_ref[...] * lk2_ref[...], axis=-1, keepdims=True)) + lam_init)
    _init_streams(q_ref, qpad_ref, m_ref, acc_ref)
    attend = functools.partial(_attend_chunk, k_ref, vt_ref, qpad_ref, m_ref, acc_ref,
                               st_ref, tkv=tkv, dv=DIFF_VDIM, v_head=lambda s: s // 2,
                               den_on_mxu=False)

    def full_body(c, _):
        attend(pl.multiple_of(c * tkv, tkv), None)
        return 0
    lax.fori_loop(0, nfull, full_body, 0)

    qpos = qi * tq + lax.broadcasted_iota(jnp.int32, (1, tq), 1)
    krow = lax.broadcasted_iota(jnp.int32, (tkv, tq), 0)

    def diag_body(c, _):
        off = pl.multiple_of(c * tkv, tkv)
        attend(off, jnp.where((off + krow) <= qpos, 0.0, MASKED))
        return 0
    lax.fori_loop(nfull, nck, diag_body, 0)

    for h in range(DIFF_HEADS):
        o1 = _normalised(acc_ref, 2 * h, DIFF_VDIM)
        o2 = _normalised(acc_ref, 2 * h + 1, DIFF_VDIM)
        ot_ref[h * DIFF_VDIM:(h + 1) * DIFF_VDIM, :] = o1 - lam * o2
    o = ot_ref[...].T
    for h in range(DIFF_HEADS):
        blk = _rmsnorm(o[:, h * DIFF_VDIM:(h + 1) * DIFF_VDIM], g_ref[...], DIFF_SUBLN_EPS)
        o_ref[0, :, h * DIFF_VDIM:(h + 1) * DIFF_VDIM] = (blk * (1.0 - lam_init)).astype(o_ref.dtype)


def _diff_branch(q, k, vt, lq1, lk1, lq2, lk2, g, lam_init, tq=512, tkv=512):
    B, S, W = q.shape
    kern = functools.partial(_diff_kernel, tq=tq, tkv=tkv, lam_init=lam_init)
    vec = lambda a: pl.BlockSpec(a.shape, lambda b, i: (0, 0))
    return pl.pallas_call(
        kern,
        grid=(B, S // tq),
        in_specs=[pl.BlockSpec((1, tq, W), lambda b, i: (b, i, 0)),
                  _resident((1, S, W)), _resident((1, W, S)),
                  vec(lq1), vec(lk1), vec(lq2), vec(lk2), vec(g)],
        out_specs=pl.BlockSpec((1, tq, W), lambda b, i: (b, i, 0)),
        out_shape=jax.ShapeDtypeStruct((B, S, W), BF16),
        scratch_shapes=[pltpu.VMEM((N_STREAMS, tq, LANES), BF16),
                        pltpu.VMEM((N_STREAMS, 1, tq), F32),
                        pltpu.VMEM((N_STREAMS, DIFF_VDIM + DEN_ROWS, tq), F32),
                        pltpu.VMEM((W, tq), F32),
                        pltpu.VMEM((N_STREAMS, tkv, tq), F32)],
        compiler_params=_cparams(("parallel", "arbitrary")),
        name="diff_branch",
    )(q, k, vt, lq1, lk1, lq2, lk2, g)


def _merge_kernel(x_ref, oat_ref, ob_ref, g_ref, wg_ref, gb_ref, wa_ref, wb_ref, wo_ref, y_ref):
    x = x_ref[...]
    h = _rmsnorm(x, g_ref[...], NORM_EPS).astype(BF16)
    gates = jax.nn.sigmoid(jnp.dot(h, wg_ref[...], preferred_element_type=F32) + gb_ref[...])
    pa = lax.dot_general(oat_ref[0], wa_ref[...], (((0,), (0,)), ((), ())),
                         preferred_element_type=F32)
    pb = jnp.dot(ob_ref[...], wb_ref[...], preferred_element_type=F32)
    merged = gates[:, :D_MODEL] * pa + gates[:, D_MODEL:] * pb
    y_ref[...] = x + jnp.dot(merged.astype(BF16), wo_ref[...], preferred_element_type=F32)


def _merge_project(x2, oat, ob, g, wg, gb, wa, wb, wo, tm=512):
    M = x2.shape[0]
    per_batch = oat.shape[2] // tm
    row = lambda w: pl.BlockSpec((tm, w), lambda i: (i, 0))
    full = lambda a: pl.BlockSpec(a.shape, lambda i: (0,) * a.ndim, pipeline_mode=pl.Buffered(1))
    col = pl.BlockSpec((1, oat.shape[1], tm), lambda i: (i // per_batch, 0, i % per_batch))
    return pl.pallas_call(
        _merge_kernel,
        grid=(M // tm,),
        in_specs=[row(D_MODEL), col, row(512), full(g), full(wg), full(gb),
                  full(wa), full(wb), full(wo)],
        out_specs=row(D_MODEL),
        out_shape=jax.ShapeDtypeStruct((M, D_MODEL), F32),
        compiler_params=_cparams(("parallel",)),
        name="merge_project",
    )(x2, oat, ob, g, wg, gb, wa, wb, wo)


def _ffn_kernel(x_ref, g_ref, win_ref, wdown_ref, gf_ref, y_ref, *, final_norm):
    x = x_ref[...]
    h = _rmsnorm(x, g_ref[...], NORM_EPS).astype(BF16)
    gate = jnp.dot(h, win_ref[:, :D_FF], preferred_element_type=F32)
    up = jnp.dot(h, win_ref[:, D_FF:], preferred_element_type=F32)
    act = (gate * jax.nn.sigmoid(gate) * up).astype(BF16)
    y = x + jnp.dot(act, wdown_ref[...], preferred_element_type=F32)
    if final_norm:
        y = _rmsnorm(y, gf_ref[...], NORM_EPS)
    y_ref[...] = y


def _ffn(x2, g, win, wdown, gf, final_norm, tm=512):
    M = x2.shape[0]
    row = lambda w: pl.BlockSpec((tm, w), lambda i: (i, 0))
    full = lambda a: pl.BlockSpec(a.shape, lambda i: (0,) * a.ndim, pipeline_mode=pl.Buffered(1))
    return pl.pallas_call(
        functools.partial(_ffn_kernel, final_norm=final_norm),
        grid=(M // tm,),
        in_specs=[row(D_MODEL), full(g), full(win), full(wdown), full(gf)],
        out_specs=row(D_MODEL),
        out_shape=jax.ShapeDtypeStruct((M, D_MODEL), F32),
        compiler_params=_cparams(("parallel",)),
        name="swiglu_ffn",
    )(x2, g, win, wdown, gf)


def _inv_freq(rot_dim):
    return jnp.power(jnp.float32(ROPE_THETA), -jnp.arange(0, rot_dim, 2, dtype=F32) / rot_dim)


def _inv_freq_column():
    inv = jnp.concatenate([_inv_freq(ROT_DIM_HEAD), _inv_freq(ROT_DIM_IDX)])
    return jnp.pad(inv, (0, TRIG_ROWS - inv.shape[0]))[:, None]


def _pad_lanes(v):
    return jnp.pad(v.astype(F32), (0, LANES - v.shape[0]))[None, :]


def kernel(x, positions, norm_mix_g, w_in, idx_k_norm_g, idx_k_norm_b, diff_lambda_q1,
           diff_lambda_k1, diff_lambda_q2, diff_lambda_k2, diff_subln_g, gate_b, w_branch_dsa,
           w_branch_diff, w_out, norm_ffn_g, w_ffn_in, w_ffn_out, norm_final_g):
    B, S, D = x.shape
    M = B * S
    depth = w_in.shape[0]
    if depth == 0:
        raise ValueError("depth must be positive")
    pos_t = positions.astype(F32)[:, None, :]
    inv_col = _inv_freq_column()
    paired = _paired_columns(2 * DSA_WIDTH)
    row = lambda v: v.astype(F32)[None, :]

    n_idx = IDX_QW + IDX_DIM + IDX_HEADS
    o_a = 0
    o_i = 3 * DSA_WIDTH
    o_b = o_i + n_idx
    o_g = o_b + 2 * DIFF_QK_WIDTH + DIFF_WIDTH

    x2 = x.reshape(M, D)
    for l in range(depth):
        lam_init = 0.8 - 0.6 * math.exp(-0.3 * l)
        wl = w_in[l]
        wa = wl[:, o_a:o_a + 2 * DSA_WIDTH][:, paired].astype(BF16)
        wavt = wl[:, o_a + 2 * DSA_WIDTH:o_a + 3 * DSA_WIDTH].T.astype(BF16)
        wi = jnp.pad(wl[:, o_i:o_i + n_idx], ((0, 0), (0, IDX_QW + LANES - n_idx))).astype(BF16)
        wb = wl[:, o_b:o_b + 2 * DIFF_QK_WIDTH][:, paired].astype(BF16)
        wbvt = wl[:, o_b + 2 * DIFF_QK_WIDTH:o_b + 2 * DIFF_QK_WIDTH + DIFF_WIDTH].T.astype(BF16)
        wg = wl[:, o_g:o_g + N_BRANCH * D_MODEL].astype(BF16)

        (q_a, k_a, vt_a, q_b, k_b, vt_b, q_i, k_i, w_it) = _in_projection(
            x2.reshape(B, S, D), pos_t, row(norm_mix_g[l]), wa, wavt, wb, wbvt, wi, inv_col,
            _pad_lanes(idx_k_norm_g[l]), _pad_lanes(idx_k_norm_b[l]))

        o_dsa = _dsa_branch(q_i, w_it, k_i, q_a, k_a, vt_a)
        o_diff = _diff_branch(q_b, k_b, vt_b,
                              row(diff_lambda_q1[l]), row(diff_lambda_k1[l]),
                              row(diff_lambda_q2[l]), row(diff_lambda_k2[l]),
                              row(diff_subln_g[l]), lam_init)

        x2 = _merge_project(x2, o_dsa, o_diff.reshape(M, DIFF_WIDTH),
                            row(norm_mix_g[l]), wg, row(gate_b[l]),
                            w_branch_dsa[l].astype(BF16), w_branch_diff[l].astype(BF16),
                            w_out[l].astype(BF16))
        x2 = _ffn(x2, row(norm_ffn_g[l]), w_ffn_in[l].astype(BF16), w_ffn_out[l].astype(BF16),
                  row(norm_final_g), final_norm=(l == depth - 1))

    return x2.reshape(B, S, D)
```

```python
import functools
import math

import jax
import jax.numpy as jnp
from jax import lax
from jax.experimental import pallas as pl
from jax.experimental.pallas import tpu as pltpu

F32 = jnp.float32
BF16 = jnp.bfloat16

D_MODEL = 1024
HEAD_DIM = 64
DSA_HEADS = 8
DSA_WIDTH = DSA_HEADS * HEAD_DIM
IDX_HEADS = 8
IDX_DIM = 32
IDX_QW = IDX_HEADS * IDX_DIM
TOPK_MAX = 256
DIFF_HEADS = 4
DIFF_DIM = 64
DIFF_VDIM = 2 * DIFF_DIM
DIFF_QK_WIDTH = DIFF_HEADS * 2 * DIFF_DIM
DIFF_WIDTH = DIFF_HEADS * DIFF_VDIM
DIFF_SUBLN_EPS = 1e-5
N_BRANCH = 2
ROPE_THETA = 500000.0
ROT_DIM_HEAD = HEAD_DIM // 4
ROT_DIM_IDX = IDX_DIM // 4
FFN_MULT = 256
D_FF = -(-8 * D_MODEL // (3 * FFN_MULT)) * FFN_MULT
NORM_EPS = 1e-6

LANES = 128
VMEM_LIMIT = 56 * 1024 * 1024
MASKED = -1e30
MAX_BISECT = 512
UNCHECKED_BISECT = 16
LOG2E = 1.4426950408889634
N_STREAMS = 8
ROWS = 64
RANK_ROWS = 256
DEN_ROWS = 16
TRIG_ROWS = 16
TRIG_ONE = 12
TRIG_ZERO = TRIG_ROWS + 12

NT_DIMS = (((1,), (1,)), ((), ()))


def _cparams(sem):
    return pltpu.CompilerParams(dimension_semantics=sem, vmem_limit_bytes=VMEM_LIMIT)


def _rmsnorm(x, g, eps):
    return x * lax.rsqrt(jnp.mean(x * x, axis=-1, keepdims=True) + eps) * g


def _rope_block(y, cos, sin_lo, sin_hi, half):
    return (y * cos + pltpu.roll(y, half, 1) * sin_hi
            + pltpu.roll(y, LANES - half, 1) * sin_lo)


def _trig_table(pos_t, inv_col):
    ang_t = inv_col * pos_t
    pad = jnp.zeros((LANES - 2 * TRIG_ROWS, ang_t.shape[1]), F32)
    return jnp.concatenate([jnp.cos(ang_t), jnp.sin(ang_t), pad], axis=0).T


def _rope_tables(tbl, first, period, half):
    d = lax.broadcasted_iota(jnp.int32, tbl.shape, 1) % period
    rot = d < 2 * half
    take = lambda idx: jnp.take_along_axis(tbl, idx, axis=1)
    cos = take(jnp.where(rot, first + d % half, TRIG_ONE))
    sin_lo = -take(jnp.where(d < half, TRIG_ROWS + first + d, TRIG_ZERO))
    sin_hi = take(jnp.where(rot & (d >= half), TRIG_ROWS + first + d % half, TRIG_ZERO))
    return cos, sin_lo, sin_hi


def _rope_tables_paired(tbl):
    d = lax.broadcasted_iota(jnp.int32, tbl.shape, 1)
    x1 = d < ROT_DIM_HEAD
    rot = x1 | ((d >= LANES // 2) & (d < LANES // 2 + ROT_DIM_HEAD))
    f = d % (ROT_DIM_HEAD // 2)
    cos = jnp.take_along_axis(tbl, jnp.where(rot, f, TRIG_ONE), axis=1)
    sin = jnp.take_along_axis(tbl, jnp.where(rot, TRIG_ROWS + f, TRIG_ZERO), axis=1)
    return cos, jnp.where(x1, -sin, sin)


def _paired_head_lanes(second):
    lane = lax.broadcasted_iota(jnp.int32, (1, LANES), 1)
    half = ROT_DIM_HEAD // 2
    in_second = ((lane >= half) & (lane < 2 * half)) | (lane >= LANES // 2 + half)
    return in_second if second else jnp.logical_not(in_second)


def _paired_columns(width):
    half = ROT_DIM_HEAD // 2
    perm = list(range(LANES))
    perm[half:2 * half] = range(HEAD_DIM, HEAD_DIM + half)
    perm[LANES // 2:LANES // 2 + half] = range(half, 2 * half)
    return jnp.asarray([(c // LANES) * LANES + perm[c % LANES] for c in range(width)], jnp.int32)


def _inproj_kernel(x_ref, post_ref, g_ref, wa_ref, wavt_ref, wb_ref, wbvt_ref, wi_ref, invcol_ref,
                   lng_ref, lnb_ref,
                   qa_ref, ka_ref, vat_ref, qb_ref, kb_ref, vbt_ref, qi_ref, ki_ref, wit_ref):
    h = _rmsnorm(x_ref[0], g_ref[...], NORM_EPS).astype(BF16)
    tbl = _trig_table(post_ref[0], invcol_ref[...])
    cos_k, sin_k = _rope_tables_paired(tbl)
    cos_i, slo_i, shi_i = _rope_tables(tbl, ROT_DIM_HEAD // 2, IDX_DIM, ROT_DIM_IDX // 2)
    q_scale = HEAD_DIM ** -0.5 * LOG2E
    cos_q, sin_q = cos_k * q_scale, sin_k * q_scale

    def branch(w_ref, wvt_ref, q_ref, k_ref, vt_ref):
        for grp, (o_ref, cos, sin) in enumerate(((q_ref, cos_q, sin_q), (k_ref, cos_k, sin_k))):
            y = jnp.dot(h, w_ref[:, grp * 512:(grp + 1) * 512], preferred_element_type=F32)
            for j in range(512 // LANES):
                blk = y[:, j * LANES:(j + 1) * LANES]
                blk = blk * cos + pltpu.roll(blk, LANES // 2, 1) * sin
                o_ref[0, :, j * LANES:(j + 1) * LANES] = blk.astype(o_ref.dtype)
        vt_ref[0] = lax.dot_general(wvt_ref[...], h, NT_DIMS,
                                    preferred_element_type=F32).astype(vt_ref.dtype)

    branch(wa_ref, wavt_ref, qa_ref, ka_ref, vat_ref)
    branch(wb_ref, wbvt_ref, qb_ref, kb_ref, vbt_ref)

    r = jnp.dot(h, wi_ref[...], preferred_element_type=F32)
    for j in range(IDX_QW // LANES):
        blk = _rope_block(r[:, j * LANES:(j + 1) * LANES], cos_i, slo_i, shi_i, ROT_DIM_IDX // 2)
        qi_ref[0, :, j * LANES:(j + 1) * LANES] = blk.astype(qi_ref.dtype)
    kw = r[:, IDX_QW:IDX_QW + LANES]
    lane = lax.broadcasted_iota(jnp.int32, (1, LANES), 1)
    is_k = lane < IDX_DIM
    mu = jnp.sum(jnp.where(is_k, kw, 0.0), axis=-1, keepdims=True) * (1.0 / IDX_DIM)
    xc = jnp.where(is_k, kw - mu, 0.0)
    var = jnp.sum(xc * xc, axis=-1, keepdims=True) * (1.0 / IDX_DIM)
    kn = xc * lax.rsqrt(var + NORM_EPS) * lng_ref[...] + lnb_ref[...]
    kn = _rope_block(kn, cos_i, slo_i, shi_i, ROT_DIM_IDX // 2)
    ki_ref[0] = kn[:, :IDX_DIM].astype(ki_ref.dtype)
    w_scale = IDX_HEADS ** -0.5 * IDX_DIM ** -0.5
    wit_ref[0] = kw.T[IDX_DIM:IDX_DIM + IDX_HEADS, :] * w_scale


def _in_projection(x, pos_t, g, wa, wavt, wb, wbvt, wi, inv_col, lng, lnb, tm=256):
    B, S, _ = x.shape
    row = lambda w: pl.BlockSpec((1, tm, w), lambda b, i: (b, i, 0))
    col = lambda r: pl.BlockSpec((1, r, tm), lambda b, i: (b, 0, i))
    full = lambda a: pl.BlockSpec(a.shape, lambda b, i: (0,) * a.ndim,
                                  pipeline_mode=pl.Buffered(1))
    sds = jax.ShapeDtypeStruct
    out_shape = [sds((B, S, 512), BF16), sds((B, S, 512), BF16), sds((B, 512, S), BF16),
                 sds((B, S, 512), BF16), sds((B, S, 512), BF16), sds((B, 512, S), BF16),
                 sds((B, S, IDX_QW), BF16), sds((B, S, IDX_DIM), BF16), sds((B, IDX_HEADS, S), F32)]
    return pl.pallas_call(
        _inproj_kernel,
        grid=(B, S // tm),
        in_specs=[row(D_MODEL), col(1), full(g), full(wa), full(wavt), full(wb), full(wbvt),
                  full(wi), full(inv_col), full(lng), full(lnb)],
        out_specs=[row(512), row(512), col(512), row(512), row(512), col(512),
                   row(IDX_QW), row(IDX_DIM), col(IDX_HEADS)],
        out_shape=out_shape,
        compiler_params=_cparams(("parallel", "parallel")),
        name="in_projection",
    )(x, pos_t, g, wa, wavt, wb, wbvt, wi, inv_col, lng, lnb)


def _init_streams(q_ref, qpad_ref, m_ref, acc_ref):
    for s in range(N_STREAMS):
        blk = q_ref[0, :, (s // 2) * LANES:(s // 2 + 1) * LANES]
        qpad_ref[s] = jnp.where(_paired_head_lanes(s % 2 == 1), blk, jnp.zeros_like(blk))
    m_ref[...] = jnp.full(m_ref.shape, MASKED, F32)
    acc_ref[...] = jnp.zeros(acc_ref.shape, F32)


def _attend_chunk(k_ref, vt_ref, qpad_ref, m_ref, acc_ref, st_ref, off, bias,
                  *, tkv, dv, v_head, den_on_mxu, between=None):
    col_max = []
    for s in range(N_STREAMS):
        kp = k_ref[0, pl.ds(off, tkv), (s // 2) * LANES:(s // 2 + 1) * LANES]
        st = lax.dot_general(kp, qpad_ref[s], NT_DIMS, preferred_element_type=F32)
        if bias is not None:
            st = st + bias
        st_ref[s] = st
        col_max.append(jnp.max(st, axis=0, keepdims=True))
    extra = None if between is None else between()
    for s in range(N_STREAMS):
        m_old = m_ref[s]
        m_new = jnp.maximum(m_old, col_max[s])
        alpha = jnp.exp2(m_old - m_new)
        r0 = v_head(s) * dv
        v_c = vt_ref[0, r0:r0 + dv, pl.ds(off, tkv)]
        if den_on_mxu:
            p = jnp.exp2((st_ref[s] - m_new).astype(BF16))
            v_aug = jnp.concatenate([v_c, jnp.ones((DEN_ROWS, tkv), BF16)], axis=0)
            acc_ref[s] = alpha * acc_ref[s] + jnp.dot(v_aug, p, preferred_element_type=F32)
        else:
            p = jnp.exp2(st_ref[s] - m_new)
            acc_ref[s, 0:dv, :] = alpha * acc_ref[s, 0:dv, :] + jnp.dot(
                v_c, p.astype(BF16), preferred_element_type=F32)
            acc_ref[s, dv:dv + 1, :] = (alpha * acc_ref[s, dv:dv + 1, :]
                                        + jnp.sum(p, axis=0, keepdims=True))
        m_ref[s] = m_new
    return extra


def _normalised(acc_ref, s, dv):
    return acc_ref[s, 0:dv, :] / acc_ref[s, dv:dv + 1, :]


def _indexer_operands(qi_ref, wt_ref):
    qx = qi_ref[0]
    q2 = jnp.concatenate([qx[:, h * IDX_DIM:(h + 1) * IDX_DIM] for h in range(IDX_HEADS)], axis=0)
    return q2, wt_ref[0]


def _score_chunk(kidx_ref, score_ref, slot, q2, wt, qpos, off, carry, *, tq, tk):
    mx, mn, c0, cp = carry
    kc = kidx_ref[0, pl.ds(off, tk), :]
    logits = lax.dot_general(kc, q2, NT_DIMS, preferred_element_type=F32)
    sc = jnp.maximum(logits[:, :tq], 0.0) * wt[0:1, :]
    for h in range(1, IDX_HEADS):
        sc = sc + jnp.maximum(logits[:, h * tq:(h + 1) * tq], 0.0) * wt[h:h + 1, :]
    causal = (off + lax.broadcasted_iota(jnp.int32, (tk, tq), 0)) <= qpos
    s_hi = jnp.where(causal, sc, -jnp.inf)
    s_lo = jnp.where(causal, sc, jnp.inf)
    score_ref[slot, pl.ds(off, tk), :] = s_hi
    for j in range(tk // ROWS):
        b_hi = s_hi[j * ROWS:(j + 1) * ROWS]
        mx = jnp.maximum(mx, b_hi)
        mn = jnp.minimum(mn, s_lo[j * ROWS:(j + 1) * ROWS])
        c0 = c0 + jnp.where(b_hi >= 0.0, 1.0, 0.0)
        cp = cp + jnp.where(b_hi > 0.0, 1.0, 0.0)
    return mx, mn, c0, cp


def _score_carry(tq):
    zero = jnp.zeros((ROWS, tq), F32)
    return zero - jnp.inf, zero + jnp.inf, zero, zero


def _select_threshold(score_ref, thr_ref, slot, nck, qpos, carry, *, tq, tk, ktop):
    mx, mn, c0, cp = carry
    hi0 = jnp.max(mx, axis=0, keepdims=True)
    lo0 = jnp.min(mn, axis=0, keepdims=True)
    n_ge0 = jnp.sum(c0, axis=0, keepdims=True)
    n_gt0 = jnp.sum(cp, axis=0, keepdims=True)
    zero = jnp.zeros((ROWS, tq), F32)

    def count(pred):
        def body(c, acc):
            off = pl.multiple_of(c * tk, tk)
            for j in range(tk // ROWS):
                x = score_ref[slot, pl.ds(pl.multiple_of(off + j * ROWS, ROWS), ROWS), :]
                acc = acc + jnp.where(pred(x), 1.0, 0.0)
            return acc
        return jnp.sum(lax.fori_loop(0, nck, body, zero), axis=0, keepdims=True)

    n_valid = (qpos + 1).astype(F32)
    kk = jnp.minimum(n_valid, float(ktop))

    all_in = n_valid <= kk
    above = jnp.logical_and(jnp.logical_not(all_in), n_gt0 >= kk)
    below = jnp.logical_and(jnp.logical_not(all_in), n_ge0 < kk)
    lo = jnp.where(all_in, lo0, jnp.where(below, lo0, 0.0))
    clo = jnp.where(all_in, n_valid, jnp.where(below, n_valid, n_ge0))
    hi = jnp.where(below, 0.0, hi0)
    fin = jnp.where(jnp.logical_or(above, below), 0.0, 1.0)

    def bis_cond(st):
        return jnp.logical_and(st[0] < MAX_BISECT, jnp.min(st[-1]) < 0.5)

    def bis_body(st):
        it, lo, hi, clo, chi, fin = st
        mid0 = 0.5 * lo + 0.5 * hi
        collapsed = jnp.logical_or(mid0 <= lo, mid0 >= hi)
        mid = jnp.where(collapsed, hi, mid0)
        c = count(lambda x: x >= mid)
        ge = c >= kk
        live = fin < 0.5
        up = jnp.logical_and(live, ge)
        dn = jnp.logical_and(live, jnp.logical_not(ge))
        lo = jnp.where(up, mid, lo)
        clo = jnp.where(up, c, clo)
        hi = jnp.where(dn, mid, hi)
        chi = jnp.where(dn, c, chi)
        done = jnp.logical_or(c == kk, collapsed)
        fin = jnp.where(jnp.logical_and(live, done), 1.0, fin)
        return it + 1, lo, hi, clo, chi, fin

    state = lax.fori_loop(0, UNCHECKED_BISECT, lambda _, st: bis_body(st),
                          (jnp.int32(0), lo, hi, clo, jnp.zeros_like(lo), fin))
    _, thr, hi, clo, chi, _ = lax.while_loop(bis_cond, bis_body, state)
    thr_ref[slot] = thr
    tied = clo > kk
    has_tie = jnp.max(jnp.where(tied, 1.0, 0.0)) > 0.5

    @pl.when(has_tie)
    def _():
        n_gt = jnp.where(jnp.logical_or(above, below),
                         jnp.where(thr >= hi, 0.0, chi), n_gt0)
        need = jnp.where(tied, kk - n_gt, float(2 ** 24))
        lower = (lax.broadcasted_iota(jnp.int32, (RANK_ROWS, RANK_ROWS), 0)
                 >= lax.broadcasted_iota(jnp.int32, (RANK_ROWS, RANK_ROWS), 1))
        lower = jnp.where(lower, 1.0, 0.0).astype(BF16)

        def body(c, need):
            rows = [pl.multiple_of(c * tk + j * RANK_ROWS, RANK_ROWS)
                    for j in range(tk // RANK_ROWS)]
            xs = [score_ref[slot, pl.ds(r0, RANK_ROWS), :] for r0 in rows]
            ranks = [jnp.dot(lower, jnp.where(x == thr, 1.0, 0.0).astype(BF16),
                             preferred_element_type=F32) for x in xs]
            for r0, x, rank in zip(rows, xs, ranks):
                kept = jnp.where(x > thr, 1.0,
                                 jnp.where(rank <= need, jnp.where(x == thr, 1.0, -1.0), -1.0))
                score_ref[slot, pl.ds(r0, RANK_ROWS), :] = kept
                need = need - rank[RANK_ROWS - 1:RANK_ROWS, :]
            return need
        lax.fori_loop(0, nck, body, need)
        thr_ref[slot] = jnp.zeros((1, tq), F32)


def _dsa_kernel(qi_ref, wt_ref, qin_ref, wtn_ref, kidx_ref, q_ref, k_ref, vt_ref, o_ref,
                score_ref, thr_ref, qpad_ref, m_ref, acc_ref, st_ref,
                *, tq, tk, ktop):
    qi = pl.program_id(1)
    slot = qi % 2
    nslot = 1 - slot
    chunks = lambda blk: ((blk + 1) * tq + tk - 1) // tk
    nck = chunks(qi)
    lanes_q = lax.broadcasted_iota(jnp.int32, (1, tq), 1)
    select = functools.partial(_select_threshold, score_ref, thr_ref, tq=tq, tk=tk, ktop=ktop)

    @pl.when(qi == 0)
    def _():
        q2, wt = _indexer_operands(qi_ref, wt_ref)

        def body(c, carry):
            return _score_chunk(kidx_ref, score_ref, 0, q2, wt, lanes_q,
                                pl.multiple_of(c * tk, tk), carry, tq=tq, tk=tk)
        select(0, nck, lanes_q, lax.fori_loop(0, nck, body, _score_carry(tq)))

    _init_streams(q_ref, qpad_ref, m_ref, acc_ref)
    thr_sel = thr_ref[slot]

    def attend(off, between=None):
        bias = jnp.where(score_ref[slot, pl.ds(off, tk), :] >= thr_sel, 0.0, MASKED)
        return _attend_chunk(k_ref, vt_ref, qpad_ref, m_ref, acc_ref, st_ref, off, bias,
                             tkv=tk, dv=HEAD_DIM, v_head=lambda s: s, den_on_mxu=True,
                             between=between)

    has_next = qi + 1 < pl.num_programs(1)

    @pl.when(has_next)
    def _():
        q2, wt = _indexer_operands(qin_ref, wtn_ref)
        qpos = (qi + 1) * tq + lanes_q
        score = functools.partial(_score_chunk, kidx_ref, score_ref, nslot, q2, wt, qpos,
                                  tq=tq, tk=tk)

        def paired(c, carry):
            off = pl.multiple_of(c * tk, tk)
            return attend(off, between=lambda: score(off, carry))
        carry = lax.fori_loop(0, nck, paired, _score_carry(tq))
        nck_next = chunks(qi + 1)
        carry = lax.fori_loop(nck, nck_next,
                              lambda c, carry: score(pl.multiple_of(c * tk, tk), carry), carry)
        select(nslot, nck_next, qpos, carry)

    @pl.when(jnp.logical_not(has_next))
    def _():
        def body(c, _):
            attend(pl.multiple_of(c * tk, tk))
            return 0
        lax.fori_loop(0, nck, body, 0)

    for s in range(N_STREAMS):
        o_ref[0, s * HEAD_DIM:(s + 1) * HEAD_DIM, :] = _normalised(
            acc_ref, s, HEAD_DIM).astype(o_ref.dtype)


def _resident(shape):
    return pl.BlockSpec(shape, lambda b, i: (b, 0, 0), pipeline_mode=pl.Buffered(1))


def _dsa_branch(q_i, w_it, k_i, q, k, vt, tq=256, tk=512):
    B, S, W = q.shape
    nq = S // tq
    ktop = min(TOPK_MAX, S // 4)
    kern = functools.partial(_dsa_kernel, tq=tq, tk=tk, ktop=ktop)
    nxt = lambda i: jnp.minimum(i + 1, nq - 1)
    return pl.pallas_call(
        kern,
        grid=(B, nq),
        in_specs=[pl.BlockSpec((1, tq, IDX_QW), lambda b, i: (b, i, 0)),
                  pl.BlockSpec((1, IDX_HEADS, tq), lambda b, i: (b, 0, i)),
                  pl.BlockSpec((1, tq, IDX_QW), lambda b, i: (b, nxt(i), 0)),
                  pl.BlockSpec((1, IDX_HEADS, tq), lambda b, i: (b, 0, nxt(i))),
                  _resident((1, S, IDX_DIM)),
                  pl.BlockSpec((1, tq, W), lambda b, i: (b, i, 0)),
                  _resident((1, S, W)), _resident((1, W, S))],
        out_specs=pl.BlockSpec((1, W, tq), lambda b, i: (b, 0, i)),
        out_shape=jax.ShapeDtypeStruct((B, W, S), BF16),
        scratch_shapes=[pltpu.VMEM((2, S, tq), F32), pltpu.VMEM((2, 1, tq), F32),
                        pltpu.VMEM((N_STREAMS, tq, LANES), BF16),
                        pltpu.VMEM((N_STREAMS, 1, tq), F32),
                        pltpu.VMEM((N_STREAMS, HEAD_DIM + DEN_ROWS, tq), F32),
                        pltpu.VMEM((N_STREAMS, tk, tq), F32)],
        compiler_params=_cparams(("arbitrary", "arbitrary")),
        name="dsa_branch",
    )(q_i, w_it, q_i, w_it, k_i, q, k, vt)


def _diff_kernel(q_ref, k_ref, vt_ref, lq1_ref, lk1_ref, lq2_ref, lk2_ref, g_ref, o_ref,
                 qpad_ref, m_ref, acc_ref, ot_ref, st_ref, *, tq, tkv, lam_init):
    qi = pl.program_id(1)
    nck = ((qi + 1) * tq + tkv - 1) // tkv
    nfull = (qi * tq + 1) // tkv
    lam = (jnp.exp(jnp.sum(lq1_ref[...] * lk1_ref[...], axis=-1, keepdims=True))
           - jnp.exp(jnp.sum(lq2_ref[...] * lk2_ref[...], axis=-1, keepdims=True)) + lam_init)
    _init_streams(q_ref, qpad_ref, m_ref, acc_ref)
    attend = functools.partial(_attend_chunk, k_ref, vt_ref, qpad_ref, m_ref, acc_ref,
                               st_ref, tkv=tkv, dv=DIFF_VDIM, v_head=lambda s: s // 2,
                               den_on_mxu=False)

    def full_body(c, _):
        attend(pl.multiple_of(c * tkv, tkv), None)
        return 0
    lax.fori_loop(0, nfull, full_body, 0)

    qpos = qi * tq + lax.broadcasted_iota(jnp.int32, (1, tq), 1)
    krow = lax.broadcasted_iota(jnp.int32, (tkv, tq), 0)

    def diag_body(c, _):
        off = pl.multiple_of(c * tkv, tkv)
        attend(off, jnp.where((off + krow) <= qpos, 0.0, MASKED))
        return 0
    lax.fori_loop(nfull, nck, diag_body, 0)

    for h in range(DIFF_HEADS):
        o1 = _normalised(acc_ref, 2 * h, DIFF_VDIM)
        o2 = _normalised(acc_ref, 2 * h + 1, DIFF_VDIM)
        ot_ref[h * DIFF_VDIM:(h + 1) * DIFF_VDIM, :] = o1 - lam * o2
    o = ot_ref[...].T
    for h in range(DIFF_HEADS):
        blk = _rmsnorm(o[:, h * DIFF_VDIM:(h + 1) * DIFF_VDIM], g_ref[...], DIFF_SUBLN_EPS)
        o_ref[0, :, h * DIFF_VDIM:(h + 1) * DIFF_VDIM] = (blk * (1.0 - lam_init)).astype(o_ref.dtype)


def _diff_branch(q, k, vt, lq1, lk1, lq2, lk2, g, lam_init, tq=512, tkv=512):
    B, S, W = q.shape
    kern = functools.partial(_diff_kernel, tq=tq, tkv=tkv, lam_init=lam_init)
    vec = lambda a: pl.BlockSpec(a.shape, lambda b, i: (0, 0))
    return pl.pallas_call(
        kern,
        grid=(B, S // tq),
        in_specs=[pl.BlockSpec((1, tq, W), lambda b, i: (b, i, 0)),
                  _resident((1, S, W)), _resident((1, W, S)),
                  vec(lq1), vec(lk1), vec(lq2), vec(lk2), vec(g)],
        out_specs=pl.BlockSpec((1, tq, W), lambda b, i: (b, i, 0)),
        out_shape=jax.ShapeDtypeStruct((B, S, W), BF16),
        scratch_shapes=[pltpu.VMEM((N_STREAMS, tq, LANES), BF16),
                        pltpu.VMEM((N_STREAMS, 1, tq), F32),
                        pltpu.VMEM((N_STREAMS, DIFF_VDIM + DEN_ROWS, tq), F32),
                        pltpu.VMEM((W, tq), F32),
                        pltpu.VMEM((N_STREAMS, tkv, tq), F32)],
        compiler_params=_cparams(("parallel", "arbitrary")),
        name="diff_branch",
    )(q, k, vt, lq1, lk1, lq2, lk2, g)


def _merge_kernel(x_ref, oat_ref, ob_ref, g_ref, wg_ref, gb_ref, wa_ref, wb_ref, wo_ref, y_ref):
    x = x_ref[...]
    h = _rmsnorm(x, g_ref[...], NORM_EPS).astype(BF16)
    gates = jax.nn.sigmoid(jnp.dot(h, wg_ref[...], preferred_element_type=F32) + gb_ref[...])
    pa = lax.dot_general(oat_ref[0], wa_ref[...], (((0,), (0,)), ((), ())),
                         preferred_element_type=F32)
    pb = jnp.dot(ob_ref[...], wb_ref[...], preferred_element_type=F32)
    merged = gates[:, :D_MODEL] * pa + gates[:, D_MODEL:] * pb
    y_ref[...] = x + jnp.dot(merged.astype(BF16), wo_ref[...], preferred_element_type=F32)


def _merge_project(x2, oat, ob, g, wg, gb, wa, wb, wo, tm=512):
    M = x2.shape[0]
    per_batch = oat.shape[2] // tm
    row = lambda w: pl.BlockSpec((tm, w), lambda i: (i, 0))
    full = lambda a: pl.BlockSpec(a.shape, lambda i: (0,) * a.ndim, pipeline_mode=pl.Buffered(1))
    col = pl.BlockSpec((1, oat.shape[1], tm), lambda i: (i // per_batch, 0, i % per_batch))
    return pl.pallas_call(
        _merge_kernel,
        grid=(M // tm,),
        in_specs=[row(D_MODEL), col, row(512), full(g), full(wg), full(gb),
                  full(wa), full(wb), full(wo)],
        out_specs=row(D_MODEL),
        out_shape=jax.ShapeDtypeStruct((M, D_MODEL), F32),
        compiler_params=_cparams(("parallel",)),
        name="merge_project",
    )(x2, oat, ob, g, wg, gb, wa, wb, wo)


def _ffn_kernel(x_ref, g_ref, win_ref, wdown_ref, gf_ref, y_ref, *, final_norm):
    x = x_ref[...]
    h = _rmsnorm(x, g_ref[...], NORM_EPS).astype(BF16)
    gate = jnp.dot(h, win_ref[:, :D_FF], preferred_element_type=F32)
    up = jnp.dot(h, win_ref[:, D_FF:], preferred_element_type=F32)
    act = (gate * jax.nn.sigmoid(gate) * up).astype(BF16)
    y = x + jnp.dot(act, wdown_ref[...], preferred_element_type=F32)
    if final_norm:
        y = _rmsnorm(y, gf_ref[...], NORM_EPS)
    y_ref[...] = y


def _ffn(x2, g, win, wdown, gf, final_norm, tm=512):
    M = x2.shape[0]
    row = lambda w: pl.BlockSpec((tm, w), lambda i: (i, 0))
    full = lambda a: pl.BlockSpec(a.shape, lambda i: (0,) * a.ndim, pipeline_mode=pl.Buffered(1))
    return pl.pallas_call(
        functools.partial(_ffn_kernel, final_norm=final_norm),
        grid=(M // tm,),
        in_specs=[row(D_MODEL), full(g), full(win), full(wdown), full(gf)],
        out_specs=row(D_MODEL),
        out_shape=jax.ShapeDtypeStruct((M, D_MODEL), F32),
        compiler_params=_cparams(("parallel",)),
        name="swiglu_ffn",
    )(x2, g, win, wdown, gf)


def _inv_freq(rot_dim):
    return jnp.power(jnp.float32(ROPE_THETA), -jnp.arange(0, rot_dim, 2, dtype=F32) / rot_dim)


def _inv_freq_column():
    inv = jnp.concatenate([_inv_freq(ROT_DIM_HEAD), _inv_freq(ROT_DIM_IDX)])
    return jnp.pad(inv, (0, TRIG_ROWS - inv.shape[0]))[:, None]


def _pad_lanes(v):
    return jnp.pad(v.astype(F32), (0, LANES - v.shape[0]))[None, :]


def kernel(x, positions, norm_mix_g, w_in, idx_k_norm_g, idx_k_norm_b, diff_lambda_q1,
           diff_lambda_k1, diff_lambda_q2, diff_lambda_k2, diff_subln_g, gate_b, w_branch_dsa,
           w_branch_diff, w_out, norm_ffn_g, w_ffn_in, w_ffn_out, norm_final_g):
    B, S, D = x.shape
    M = B * S
    depth = w_in.shape[0]
    if depth == 0:
        raise ValueError("depth must be positive")
    pos_t = positions.astype(F32)[:, None, :]
    inv_col = _inv_freq_column()
    paired = _paired_columns(2 * DSA_WIDTH)
    row = lambda v: v.astype(F32)[None, :]

    n_idx = IDX_QW + IDX_DIM + IDX_HEADS
    o_a = 0
    o_i = 3 * DSA_WIDTH
    o_b = o_i + n_idx
    o_g = o_b + 2 * DIFF_QK_WIDTH + DIFF_WIDTH

    x2 = x.reshape(M, D)
    for l in range(depth):
        lam_init = 0.8 - 0.6 * math.exp(-0.3 * l)
        wl = w_in[l]
        wa = wl[:, o_a:o_a + 2 * DSA_WIDTH][:, paired].astype(BF16)
        wavt = wl[:, o_a + 2 * DSA_WIDTH:o_a + 3 * DSA_WIDTH].T.astype(BF16)
        wi = jnp.pad(wl[:, o_i:o_i + n_idx], ((0, 0), (0, IDX_QW + LANES - n_idx))).astype(BF16)
        wb = wl[:, o_b:o_b + 2 * DIFF_QK_WIDTH][:, paired].astype(BF16)
        wbvt = wl[:, o_b + 2 * DIFF_QK_WIDTH:o_b + 2 * DIFF_QK_WIDTH + DIFF_WIDTH].T.astype(BF16)
        wg = wl[:, o_g:o_g + N_BRANCH * D_MODEL].astype(BF16)

        (q_a, k_a, vt_a, q_b, k_b, vt_b, q_i, k_i, w_it) = _in_projection(
            x2.reshape(B, S, D), pos_t, row(norm_mix_g[l]), wa, wavt, wb, wbvt, wi, inv_col,
            _pad_lanes(idx_k_norm_g[l]), _pad_lanes(idx_k_norm_b[l]))

        o_dsa = _dsa_branch(q_i, w_it, k_i, q_a, k_a, vt_a)
        o_diff = _diff_branch(q_b, k_b, vt_b,
                              row(diff_lambda_q1[l]), row(diff_lambda_k1[l]),
                              row(diff_lambda_q2[l]), row(diff_lambda_k2[l]),
                              row(diff_subln_g[l]), lam_init)

        x2 = _merge_project(x2, o_dsa, o_diff.reshape(M, DIFF_WIDTH),
                            row(norm_mix_g[l]), wg, row(gate_b[l]),
                            w_branch_dsa[l].astype(BF16), w_branch_diff[l].astype(BF16),
                            w_out[l].astype(BF16))
        x2 = _ffn(x2, row(norm_ffn_g[l]), w_ffn_in[l].astype(BF16), w_ffn_out[l].astype(BF16),
                  row(norm_final_g), final_norm=(l == depth - 1))

    return x2.reshape(B, S, D)
```

```python
import functools
import math

import jax
import jax.numpy as jnp
from jax import lax
from jax.experimental import pallas as pl
from jax.experimental.pallas import tpu as pltpu

F32 = jnp.float32
BF16 = jnp.bfloat16

D_MODEL = 1024
HEAD_DIM = 64
DSA_HEADS = 8
DSA_WIDTH = DSA_HEADS * HEAD_DIM
IDX_HEADS = 8
IDX_DIM = 32
IDX_QW = IDX_HEADS * IDX_DIM
TOPK_MAX = 256
DIFF_HEADS = 4
DIFF_DIM = 64
DIFF_VDIM = 2 * DIFF_DIM
DIFF_QK_WIDTH = DIFF_HEADS * 2 * DIFF_DIM
DIFF_WIDTH = DIFF_HEADS * DIFF_VDIM
DIFF_SUBLN_EPS = 1e-5
N_BRANCH = 2
ROPE_THETA = 500000.0
ROT_DIM_HEAD = HEAD_DIM // 4
ROT_DIM_IDX = IDX_DIM // 4
FFN_MULT = 256
D_FF = -(-8 * D_MODEL // (3 * FFN_MULT)) * FFN_MULT
NORM_EPS = 1e-6

LANES = 128
VMEM_LIMIT = 56 * 1024 * 1024
MASKED = -1e30
MAX_BISECT = 512
UNCHECKED_BISECT = 16
LOG2E = 1.4426950408889634
N_STREAMS = 8
ROWS = 64
RANK_ROWS = 256
DEN_ROWS = 16
TRIG_ROWS = 16
TRIG_ONE = 12
TRIG_ZERO = TRIG_ROWS + 12

NT_DIMS = (((1,), (1,)), ((), ()))


def _cparams(sem):
    return pltpu.CompilerParams(dimension_semantics=sem, vmem_limit_bytes=VMEM_LIMIT)


def _rmsnorm(x, g, eps):
    return x * lax.rsqrt(jnp.mean(x * x, axis=-1, keepdims=True) + eps) * g


def _rope_block(y, cos, sin_lo, sin_hi, half):
    return (y * cos + pltpu.roll(y, half, 1) * sin_hi
            + pltpu.roll(y, LANES - half, 1) * sin_lo)


def _trig_table(pos_t, inv_col):
    ang_t = inv_col * pos_t
    pad = jnp.zeros((LANES - 2 * TRIG_ROWS, ang_t.shape[1]), F32)
    return jnp.concatenate([jnp.cos(ang_t), jnp.sin(ang_t), pad], axis=0).T


def _rope_tables(tbl, first, period, half):
    d = lax.broadcasted_iota(jnp.int32, tbl.shape, 1) % period
    rot = d < 2 * half
    take = lambda idx: jnp.take_along_axis(tbl, idx, axis=1)
    cos = take(jnp.where(rot, first + d % half, TRIG_ONE))
    sin_lo = -take(jnp.where(d < half, TRIG_ROWS + first + d, TRIG_ZERO))
    sin_hi = take(jnp.where(rot & (d >= half), TRIG_ROWS + first + d % half, TRIG_ZERO))
    return cos, sin_lo, sin_hi


def _rope_tables_paired(tbl):
    d = lax.broadcasted_iota(jnp.int32, tbl.shape, 1)
    x1 = d < ROT_DIM_HEAD
    rot = x1 | ((d >= LANES // 2) & (d < LANES // 2 + ROT_DIM_HEAD))
    f = d % (ROT_DIM_HEAD // 2)
    cos = jnp.take_along_axis(tbl, jnp.where(rot, f, TRIG_ONE), axis=1)
    sin = jnp.take_along_axis(tbl, jnp.where(rot, TRIG_ROWS + f, TRIG_ZERO), axis=1)
    return cos, jnp.where(x1, -sin, sin)


def _paired_head_lanes(second):
    lane = lax.broadcasted_iota(jnp.int32, (1, LANES), 1)
    half = ROT_DIM_HEAD // 2
    in_second = ((lane >= half) & (lane < 2 * half)) | (lane >= LANES // 2 + half)
    return in_second if second else jnp.logical_not(in_second)


def _paired_columns(width):
    half = ROT_DIM_HEAD // 2
    perm = list(range(LANES))
    perm[half:2 * half] = range(HEAD_DIM, HEAD_DIM + half)
    perm[LANES // 2:LANES // 2 + half] = range(half, 2 * half)
    return jnp.asarray([(c // LANES) * LANES + perm[c % LANES] for c in range(width)], jnp.int32)


def _inproj_kernel(x_ref, post_ref, g_ref, wa_ref, wavt_ref, wb_ref, wbvt_ref, wi_ref, invcol_ref,
                   lng_ref, lnb_ref,
                   qa_ref, ka_ref, vat_ref, qb_ref, kb_ref, vbt_ref, qi_ref, ki_ref, wit_ref):
    h = _rmsnorm(x_ref[0], g_ref[...], NORM_EPS).astype(BF16)
    tbl = _trig_table(post_ref[0], invcol_ref[...])
    cos_k, sin_k = _rope_tables_paired(tbl)
    cos_i, slo_i, shi_i = _rope_tables(tbl, ROT_DIM_HEAD // 2, IDX_DIM, ROT_DIM_IDX // 2)
    q_scale = HEAD_DIM ** -0.5 * LOG2E
    cos_q, sin_q = cos_k * q_scale, sin_k * q_scale

    def branch(w_ref, wvt_ref, q_ref, k_ref, vt_ref):
        for grp, (o_ref, cos, sin) in enumerate(((q_ref, cos_q, sin_q), (k_ref, cos_k, sin_k))):
            y = jnp.dot(h, w_ref[:, grp * 512:(grp + 1) * 512], preferred_element_type=F32)
            for j in range(512 // LANES):
                blk = y[:, j * LANES:(j + 1) * LANES]
                blk = blk * cos + pltpu.roll(blk, LANES // 2, 1) * sin
                o_ref[0, :, j * LANES:(j + 1) * LANES] = blk.astype(o_ref.dtype)
        vt_ref[0] = lax.dot_general(wvt_ref[...], h, NT_DIMS,
                                    preferred_element_type=F32).astype(vt_ref.dtype)

    branch(wa_ref, wavt_ref, qa_ref, ka_ref, vat_ref)
    branch(wb_ref, wbvt_ref, qb_ref, kb_ref, vbt_ref)

    r = jnp.dot(h, wi_ref[...], preferred_element_type=F32)
    for j in range(IDX_QW // LANES):
        blk = _rope_block(r[:, j * LANES:(j + 1) * LANES], cos_i, slo_i, shi_i, ROT_DIM_IDX // 2)
        qi_ref[0, :, j * LANES:(j + 1) * LANES] = blk.astype(qi_ref.dtype)
    kw = r[:, IDX_QW:IDX_QW + LANES]
    lane = lax.broadcasted_iota(jnp.int32, (1, LANES), 1)
    is_k = lane < IDX_DIM
    mu = jnp.sum(jnp.where(is_k, kw, 0.0), axis=-1, keepdims=True) * (1.0 / IDX_DIM)
    xc = jnp.where(is_k, kw - mu, 0.0)
    var = jnp.sum(xc * xc, axis=-1, keepdims=True) * (1.0 / IDX_DIM)
    kn = xc * lax.rsqrt(var + NORM_EPS) * lng_ref[...] + lnb_ref[...]
    kn = _rope_block(kn, cos_i, slo_i, shi_i, ROT_DIM_IDX // 2)
    ki_ref[0] = kn[:, :IDX_DIM].astype(ki_ref.dtype)
    w_scale = IDX_HEADS ** -0.5 * IDX_DIM ** -0.5
    wit_ref[0] = kw.T[IDX_DIM:IDX_DIM + IDX_HEADS, :] * w_scale


def _in_projection(x, pos_t, g, wa, wavt, wb, wbvt, wi, inv_col, lng, lnb, tm=256):
    B, S, _ = x.shape
    row = lambda w: pl.BlockSpec((1, tm, w), lambda b, i: (b, i, 0))
    col = lambda r: pl.BlockSpec((1, r, tm), lambda b, i: (b, 0, i))
    full = lambda a: pl.BlockSpec(a.shape, lambda b, i: (0,) * a.ndim,
                                  pipeline_mode=pl.Buffered(1))
    sds = jax.ShapeDtypeStruct
    out_shape = [sds((B, S, 512), BF16), sds((B, S, 512), BF16), sds((B, 512, S), BF16),
                 sds((B, S, 512), BF16), sds((B, S, 512), BF16), sds((B, 512, S), BF16),
                 sds((B, S, IDX_QW), BF16), sds((B, S, IDX_DIM), BF16), sds((B, IDX_HEADS, S), F32)]
    return pl.pallas_call(
        _inproj_kernel,
        grid=(B, S // tm),
        in_specs=[row(D_MODEL), col(1), full(g), full(wa), full(wavt), full(wb), full(wbvt),
                  full(wi), full(inv_col), full(lng), full(lnb)],
        out_specs=[row(512), row(512), col(512), row(512), row(512), col(512),
                   row(IDX_QW), row(IDX_DIM), col(IDX_HEADS)],
        out_shape=out_shape,
        compiler_params=_cparams(("parallel", "parallel")),
        name="in_projection",
    )(x, pos_t, g, wa, wavt, wb, wbvt, wi, inv_col, lng, lnb)


def _init_streams(q_ref, qpad_ref, m_ref, acc_ref):
    for s in range(N_STREAMS):
        blk = q_ref[0, :, (s // 2) * LANES:(s // 2 + 1) * LANES]
        qpad_ref[s] = jnp.where(_paired_head_lanes(s % 2 == 1), blk, jnp.zeros_like(blk))
    m_ref[...] = jnp.full(m_ref.shape, MASKED, F32)
    acc_ref[...] = jnp.zeros(acc_ref.shape, F32)


def _attend_chunk(k_ref, vt_ref, qpad_ref, m_ref, acc_ref, st_ref, off, bias,
                  *, tkv, dv, v_head, den_on_mxu, between=None):
    col_max = []
    for s in range(N_STREAMS):
        kp = k_ref[0, pl.ds(off, tkv), (s // 2) * LANES:(s // 2 + 1) * LANES]
        st = lax.dot_general(kp, qpad_ref[s], NT_DIMS, preferred_element_type=F32)
        if bias is not None:
            st = st + bias
        st_ref[s] = st
        col_max.append(jnp.max(st, axis=0, keepdims=True))
    extra = None if between is None else between()
    for s in range(N_STREAMS):
        m_old = m_ref[s]
        m_new = jnp.maximum(m_old, col_max[s])
        alpha = jnp.exp2(m_old - m_new)
        r0 = v_head(s) * dv
        v_c = vt_ref[0, r0:r0 + dv, pl.ds(off, tkv)]
        if den_on_mxu:
            p = jnp.exp2((st_ref[s] - m_new).astype(BF16))
            v_aug = jnp.concatenate([v_c, jnp.ones((DEN_ROWS, tkv), BF16)], axis=0)
            acc_ref[s] = alpha * acc_ref[s] + jnp.dot(v_aug, p, preferred_element_type=F32)
        else:
            p = jnp.exp2(st_ref[s] - m_new)
            acc_ref[s, 0:dv, :] = alpha * acc_ref[s, 0:dv, :] + jnp.dot(
                v_c, p.astype(BF16), preferred_element_type=F32)
            acc_ref[s, dv:dv + 1, :] = (alpha * acc_ref[s, dv:dv + 1, :]
                                        + jnp.sum(p, axis=0, keepdims=True))
        m_ref[s] = m_new
    return extra


def _normalised(acc_ref, s, dv):
    return acc_ref[s, 0:dv, :] / acc_ref[s, dv:dv + 1, :]


def _indexer_operands(qi_ref, wt_ref):
    qx = qi_ref[0]
    q2 = jnp.concatenate([qx[:, h * IDX_DIM:(h + 1) * IDX_DIM] for h in range(IDX_HEADS)], axis=0)
    return q2, wt_ref[0]


def _score_chunk(kidx_ref, score_ref, slot, q2, wt, qpos, off, carry, *, tq, tk, all_causal=False):
    mx, mn, c0, cp = carry
    kc = kidx_ref[0, pl.ds(off, tk), :]
    logits = lax.dot_general(kc, q2, NT_DIMS, preferred_element_type=F32)
    sc = jnp.maximum(logits[:, :tq], 0.0) * wt[0:1, :]
    for h in range(1, IDX_HEADS):
        sc = sc + jnp.maximum(logits[:, h * tq:(h + 1) * tq], 0.0) * wt[h:h + 1, :]
    if all_causal:
        s_hi = s_lo = sc
    else:
        causal = (off + lax.broadcasted_iota(jnp.int32, (tk, tq), 0)) <= qpos
        s_hi = jnp.where(causal, sc, -jnp.inf)
        s_lo = jnp.where(causal, sc, jnp.inf)
    score_ref[slot, pl.ds(off, tk), :] = s_hi
    for j in range(tk // ROWS):
        b_hi = s_hi[j * ROWS:(j + 1) * ROWS]
        mx = jnp.maximum(mx, b_hi)
        mn = jnp.minimum(mn, s_lo[j * ROWS:(j + 1) * ROWS])
        c0 = c0 + jnp.where(b_hi >= 0.0, 1.0, 0.0)
        cp = cp + jnp.where(b_hi > 0.0, 1.0, 0.0)
    return mx, mn, c0, cp


def _score_carry(tq):
    zero = jnp.zeros((ROWS, tq), F32)
    return zero - jnp.inf, zero + jnp.inf, zero, zero


def _select_threshold(score_ref, thr_ref, slot, nck, qpos, carry, *, tq, tk, ktop):
    mx, mn, c0, cp = carry
    hi0 = jnp.max(mx, axis=0, keepdims=True)
    lo0 = jnp.min(mn, axis=0, keepdims=True)
    n_ge0 = jnp.sum(c0, axis=0, keepdims=True)
    n_gt0 = jnp.sum(cp, axis=0, keepdims=True)
    zero = jnp.zeros((ROWS, tq), F32)

    def count(pred):
        def body(c, acc):
            off = pl.multiple_of(c * tk, tk)
            for j in range(tk // ROWS):
                x = score_ref[slot, pl.ds(pl.multiple_of(off + j * ROWS, ROWS), ROWS), :]
                acc = acc + jnp.where(pred(x), 1.0, 0.0)
            return acc
        return jnp.sum(lax.fori_loop(0, nck, body, zero), axis=0, keepdims=True)

    n_valid = (qpos + 1).astype(F32)
    kk = jnp.minimum(n_valid, float(ktop))

    all_in = n_valid <= kk
    above = jnp.logical_and(jnp.logical_not(all_in), n_gt0 >= kk)
    below = jnp.logical_and(jnp.logical_not(all_in), n_ge0 < kk)
    lo = jnp.where(all_in, lo0, jnp.where(below, lo0, 0.0))
    clo = jnp.where(all_in, n_valid, jnp.where(below, n_valid, n_ge0))
    hi = jnp.where(below, 0.0, hi0)
    fin = jnp.where(jnp.logical_or(above, below), 0.0, 1.0)

    def bis_cond(st):
        return jnp.logical_and(st[0] < MAX_BISECT, jnp.min(st[-1]) < 0.5)

    def bis_body(st):
        it, lo, hi, clo, chi, fin = st
        mid0 = 0.5 * lo + 0.5 * hi
        collapsed = jnp.logical_or(mid0 <= lo, mid0 >= hi)
        mid = jnp.where(collapsed, hi, mid0)
        c = count(lambda x: x >= mid)
        ge = c >= kk
        live = fin < 0.5
        up = jnp.logical_and(live, ge)
        dn = jnp.logical_and(live, jnp.logical_not(ge))
        lo = jnp.where(up, mid, lo)
        clo = jnp.where(up, c, clo)
        hi = jnp.where(dn, mid, hi)
        chi = jnp.where(dn, c, chi)
        done = jnp.logical_or(c == kk, collapsed)
        fin = jnp.where(jnp.logical_and(live, done), 1.0, fin)
        return it + 1, lo, hi, clo, chi, fin

    state = lax.fori_loop(0, UNCHECKED_BISECT, lambda _, st: bis_body(st),
                          (jnp.int32(0), lo, hi, clo, jnp.zeros_like(lo), fin))
    _, thr, hi, clo, chi, _ = lax.while_loop(bis_cond, bis_body, state)
    thr_ref[slot] = thr
    tied = clo > kk
    has_tie = jnp.max(jnp.where(tied, 1.0, 0.0)) > 0.5

    @pl.when(has_tie)
    def _():
        n_gt = jnp.where(jnp.logical_or(above, below),
                         jnp.where(thr >= hi, 0.0, chi), n_gt0)
        need = jnp.where(tied, kk - n_gt, float(2 ** 24))
        lower = (lax.broadcasted_iota(jnp.int32, (RANK_ROWS, RANK_ROWS), 0)
                 >= lax.broadcasted_iota(jnp.int32, (RANK_ROWS, RANK_ROWS), 1))
        lower = jnp.where(lower, 1.0, 0.0).astype(BF16)

        def body(c, need):
            rows = [pl.multiple_of(c * tk + j * RANK_ROWS, RANK_ROWS)
                    for j in range(tk // RANK_ROWS)]
            xs = [score_ref[slot, pl.ds(r0, RANK_ROWS), :] for r0 in rows]
            ranks = [jnp.dot(lower, jnp.where(x == thr, 1.0, 0.0).astype(BF16),
                             preferred_element_type=F32) for x in xs]
            for r0, x, rank in zip(rows, xs, ranks):
                kept = jnp.where(x > thr, 1.0,
                                 jnp.where(rank <= need, jnp.where(x == thr, 1.0, -1.0), -1.0))
                score_ref[slot, pl.ds(r0, RANK_ROWS), :] = kept
                need = need - rank[RANK_ROWS - 1:RANK_ROWS, :]
            return need
        lax.fori_loop(0, nck, body, need)
        thr_ref[slot] = jnp.zeros((1, tq), F32)


def _dsa_kernel(qi_ref, wt_ref, qin_ref, wtn_ref, kidx_ref, q_ref, k_ref, vt_ref, o_ref,
                score_ref, thr_ref, qpad_ref, m_ref, acc_ref, st_ref,
                *, tq, tk, ktop):
    qi = pl.program_id(1)
    slot = qi % 2
    nslot = 1 - slot
    chunks = lambda blk: ((blk + 1) * tq + tk - 1) // tk
    nck = chunks(qi)
    lanes_q = lax.broadcasted_iota(jnp.int32, (1, tq), 1)
    select = functools.partial(_select_threshold, score_ref, thr_ref, tq=tq, tk=tk, ktop=ktop)

    @pl.when(qi == 0)
    def _():
        q2, wt = _indexer_operands(qi_ref, wt_ref)

        def body(c, carry):
            return _score_chunk(kidx_ref, score_ref, 0, q2, wt, lanes_q,
                                pl.multiple_of(c * tk, tk), carry, tq=tq, tk=tk)
        select(0, nck, lanes_q, lax.fori_loop(0, nck, body, _score_carry(tq)))

    _init_streams(q_ref, qpad_ref, m_ref, acc_ref)
    thr_sel = thr_ref[slot]

    def attend(off, between=None):
        bias = jnp.where(score_ref[slot, pl.ds(off, tk), :] >= thr_sel, 0.0, MASKED)
        return _attend_chunk(k_ref, vt_ref, qpad_ref, m_ref, acc_ref, st_ref, off, bias,
                             tkv=tk, dv=HEAD_DIM, v_head=lambda s: s, den_on_mxu=True,
                             between=between)

    has_next = qi + 1 < pl.num_programs(1)

    @pl.when(has_next)
    def _():
        q2, wt = _indexer_operands(qin_ref, wtn_ref)
        qpos = (qi + 1) * tq + lanes_q
        score = functools.partial(_score_chunk, kidx_ref, score_ref, nslot, q2, wt, qpos,
                                  tq=tq, tk=tk)

        def paired(all_causal):
            def body(c, carry):
                off = pl.multiple_of(c * tk, tk)
                return attend(off, between=lambda: score(off, carry, all_causal=all_causal))
            return body
        nfull_next = jnp.minimum(((qi + 1) * tq + 1) // tk, nck)
        carry = lax.fori_loop(0, nfull_next, paired(True), _score_carry(tq))
        carry = lax.fori_loop(nfull_next, nck, paired(False), carry)
        nck_next = chunks(qi + 1)
        carry = lax.fori_loop(nck, nck_next,
                              lambda c, carry: score(pl.multiple_of(c * tk, tk), carry), carry)
        select(nslot, nck_next, qpos, carry)

    @pl.when(jnp.logical_not(has_next))
    def _():
        def body(c, _):
            attend(pl.multiple_of(c * tk, tk))
            return 0
        lax.fori_loop(0, nck, body, 0)

    for s in range(N_STREAMS):
        o_ref[0, s * HEAD_DIM:(s + 1) * HEAD_DIM, :] = _normalised(
            acc_ref, s, HEAD_DIM).astype(o_ref.dtype)


def _resident(shape):
    return pl.BlockSpec(shape, lambda b, i: (b, 0, 0), pipeline_mode=pl.Buffered(1))


def _dsa_branch(q_i, w_it, k_i, q, k, vt, tq=256, tk=512):
    B, S, W = q.shape
    nq = S // tq
    ktop = min(TOPK_MAX, S // 4)
    kern = functools.partial(_dsa_kernel, tq=tq, tk=tk, ktop=ktop)
    nxt = lambda i: jnp.minimum(i + 1, nq - 1)
    return pl.pallas_call(
        kern,
        grid=(B, nq),
        in_specs=[pl.BlockSpec((1, tq, IDX_QW), lambda b, i: (b, i, 0)),
                  pl.BlockSpec((1, IDX_HEADS, tq), lambda b, i: (b, 0, i)),
                  pl.BlockSpec((1, tq, IDX_QW), lambda b, i: (b, nxt(i), 0)),
                  pl.BlockSpec((1, IDX_HEADS, tq), lambda b, i: (b, 0, nxt(i))),
                  _resident((1, S, IDX_DIM)),
                  pl.BlockSpec((1, tq, W), lambda b, i: (b, i, 0)),
                  _resident((1, S, W)), _resident((1, W, S))],
        out_specs=pl.BlockSpec((1, W, tq), lambda b, i: (b, 0, i)),
        out_shape=jax.ShapeDtypeStruct((B, W, S), BF16),
        scratch_shapes=[pltpu.VMEM((2, S, tq), F32), pltpu.VMEM((2, 1, tq), F32),
                        pltpu.VMEM((N_STREAMS, tq, LANES), BF16),
                        pltpu.VMEM((N_STREAMS, 1, tq), F32),
                        pltpu.VMEM((N_STREAMS, HEAD_DIM + DEN_ROWS, tq), F32),
                        pltpu.VMEM((N_STREAMS, tk, tq), F32)],
        compiler_params=_cparams(("arbitrary", "arbitrary")),
        name="dsa_branch",
    )(q_i, w_it, q_i, w_it, k_i, q, k, vt)


def _diff_kernel(q_ref, k_ref, vt_ref, lq1_ref, lk1_ref, lq2_ref, lk2_ref, g_ref, o_ref,
                 qpad_ref, m_ref, acc_ref, ot_ref, st_ref, *, tq, tkv, lam_init):
    qi = pl.program_id(1)
    nck = ((qi + 1) * tq + tkv - 1) // tkv
    nfull = (qi * tq + 1) // tkv
    lam = (jnp.exp(jnp.sum(lq1_ref[...] * lk1_ref[...], axis=-1, keepdims=True))
           - jnp.exp(jnp.sum(lq2_ref[...] * lk2_ref[...], axis=-1, keepdims=True)) + lam_init)
    _init_streams(q_ref, qpad_ref, m_ref, acc_ref)
    attend = functools.partial(_attend_chunk, k_ref, vt_ref, qpad_ref, m_ref, acc_ref,
                               st_ref, tkv=tkv, dv=DIFF_VDIM, v_head=lambda s: s // 2,
                               den_on_mxu=False)

    def full_body(c, _):
        attend(pl.multiple_of(c * tkv, tkv), None)
        return 0
    lax.fori_loop(0, nfull, full_body, 0)

    qpos = qi * tq + lax.broadcasted_iota(jnp.int32, (1, tq), 1)
    krow = lax.broadcasted_iota(jnp.int32, (tkv, tq), 0)

    def diag_body(c, _):
        off = pl.multiple_of(c * tkv, tkv)
        attend(off, jnp.where((off + krow) <= qpos, 0.0, MASKED))
        return 0
    lax.fori_loop(nfull, nck, diag_body, 0)

    for h in range(DIFF_HEADS):
        o1 = _normalised(acc_ref, 2 * h, DIFF_VDIM)
        o2 = _normalised(acc_ref, 2 * h + 1, DIFF_VDIM)
        ot_ref[h * DIFF_VDIM:(h + 1) * DIFF_VDIM, :] = o1 - lam * o2
    o = ot_ref[...].T
    for h in range(DIFF_HEADS):
        blk = _rmsnorm(o[:, h * DIFF_VDIM:(h + 1) * DIFF_VDIM], g_ref[...], DIFF_SUBLN_EPS)
        o_ref[0, :, h * DIFF_VDIM:(h + 1) * DIFF_VDIM] = (blk * (1.0 - lam_init)).astype(o_ref.dtype)


def _diff_branch(q, k, vt, lq1, lk1, lq2, lk2, g, lam_init, tq=512, tkv=512):
    B, S, W = q.shape
    kern = functools.partial(_diff_kernel, tq=tq, tkv=tkv, lam_init=lam_init)
    vec = lambda a: pl.BlockSpec(a.shape, lambda b, i: (0, 0))
    return pl.pallas_call(
        kern,
        grid=(B, S // tq),
        in_specs=[pl.BlockSpec((1, tq, W), lambda b, i: (b, i, 0)),
                  _resident((1, S, W)), _resident((1, W, S)),
                  vec(lq1), vec(lk1), vec(lq2), vec(lk2), vec(g)],
        out_specs=pl.BlockSpec((1, tq, W), lambda b, i: (b, i, 0)),
        out_shape=jax.ShapeDtypeStruct((B, S, W), BF16),
        scratch_shapes=[pltpu.VMEM((N_STREAMS, tq, LANES), BF16),
                        pltpu.VMEM((N_STREAMS, 1, tq), F32),
                        pltpu.VMEM((N_STREAMS, DIFF_VDIM + DEN_ROWS, tq), F32),
                        pltpu.VMEM((W, tq), F32),
                        pltpu.VMEM((N_STREAMS, tkv, tq), F32)],
        compiler_params=_cparams(("parallel", "arbitrary")),
        name="diff_branch",
    )(q, k, vt, lq1, lk1, lq2, lk2, g)


def _merge_kernel(x_ref, oat_ref, ob_ref, g_ref, wg_ref, gb_ref, wa_ref, wb_ref, wo_ref, y_ref):
    x = x_ref[...]
    h = _rmsnorm(x, g_ref[...], NORM_EPS).astype(BF16)
    gates = jax.nn.sigmoid(jnp.dot(h, wg_ref[...], preferred_element_type=F32) + gb_ref[...])
    pa = lax.dot_general(oat_ref[0], wa_ref[...], (((0,), (0,)), ((), ())),
                         preferred_element_type=F32)
    pb = jnp.dot(ob_ref[...], wb_ref[...], preferred_element_type=F32)
    merged = gates[:, :D_MODEL] * pa + gates[:, D_MODEL:] * pb
    y_ref[...] = x + jnp.dot(merged.astype(BF16), wo_ref[...], preferred_element_type=F32)


def _merge_project(x2, oat, ob, g, wg, gb, wa, wb, wo, tm=512):
    M = x2.shape[0]
    per_batch = oat.shape[2] // tm
    row = lambda w: pl.BlockSpec((tm, w), lambda i: (i, 0))
    full = lambda a: pl.BlockSpec(a.shape, lambda i: (0,) * a.ndim, pipeline_mode=pl.Buffered(1))
    col = pl.BlockSpec((1, oat.shape[1], tm), lambda i: (i // per_batch, 0, i % per_batch))
    return pl.pallas_call(
        _merge_kernel,
        grid=(M // tm,),
        in_specs=[row(D_MODEL), col, row(512), full(g), full(wg), full(gb),
                  full(wa), full(wb), full(wo)],
        out_specs=row(D_MODEL),
        out_shape=jax.ShapeDtypeStruct((M, D_MODEL), F32),
        compiler_params=_cparams(("parallel",)),
        name="merge_project",
    )(x2, oat, ob, g, wg, gb, wa, wb, wo)


def _ffn_kernel(x_ref, g_ref, win_ref, wdown_ref, gf_ref, y_ref, *, final_norm):
    x = x_ref[...]
    h = _rmsnorm(x, g_ref[...], NORM_EPS).astype(BF16)
    gate = jnp.dot(h, win_ref[:, :D_FF], preferred_element_type=F32)
    up = jnp.dot(h, win_ref[:, D_FF:], preferred_element_type=F32)
    act = (gate * jax.nn.sigmoid(gate) * up).astype(BF16)
    y = x + jnp.dot(act, wdown_ref[...], preferred_element_type=F32)
    if final_norm:
        y = _rmsnorm(y, gf_ref[...], NORM_EPS)
    y_ref[...] = y


def _ffn(x2, g, win, wdown, gf, final_norm, tm=512):
    M = x2.shape[0]
    row = lambda w: pl.BlockSpec((tm, w), lambda i: (i, 0))
    full = lambda a: pl.BlockSpec(a.shape, lambda i: (0,) * a.ndim, pipeline_mode=pl.Buffered(1))
    return pl.pallas_call(
        functools.partial(_ffn_kernel, final_norm=final_norm),
        grid=(M // tm,),
        in_specs=[row(D_MODEL), full(g), full(win), full(wdown), full(gf)],
        out_specs=row(D_MODEL),
        out_shape=jax.ShapeDtypeStruct((M, D_MODEL), F32),
        compiler_params=_cparams(("parallel",)),
        name="swiglu_ffn",
    )(x2, g, win, wdown, gf)


def _inv_freq(rot_dim):
    return jnp.power(jnp.float32(ROPE_THETA), -jnp.arange(0, rot_dim, 2, dtype=F32) / rot_dim)


def _inv_freq_column():
    inv = jnp.concatenate([_inv_freq(ROT_DIM_HEAD), _inv_freq(ROT_DIM_IDX)])
    return jnp.pad(inv, (0, TRIG_ROWS - inv.shape[0]))[:, None]


def _pad_lanes(v):
    return jnp.pad(v.astype(F32), (0, LANES - v.shape[0]))[None, :]


def kernel(x, positions, norm_mix_g, w_in, idx_k_norm_g, idx_k_norm_b, diff_lambda_q1,
           diff_lambda_k1, diff_lambda_q2, diff_lambda_k2, diff_subln_g, gate_b, w_branch_dsa,
           w_branch_diff, w_out, norm_ffn_g, w_ffn_in, w_ffn_out, norm_final_g):
    B, S, D = x.shape
    M = B * S
    depth = w_in.shape[0]
    if depth == 0:
        raise ValueError("depth must be positive")
    pos_t = positions.astype(F32)[:, None, :]
    inv_col = _inv_freq_column()
    paired = _paired_columns(2 * DSA_WIDTH)
    row = lambda v: v.astype(F32)[None, :]

    n_idx = IDX_QW + IDX_DIM + IDX_HEADS
    o_a = 0
    o_i = 3 * DSA_WIDTH
    o_b = o_i + n_idx
    o_g = o_b + 2 * DIFF_QK_WIDTH + DIFF_WIDTH

    x2 = x.reshape(M, D)
    for l in range(depth):
        lam_init = 0.8 - 0.6 * math.exp(-0.3 * l)
        wl = w_in[l]
        wa = wl[:, o_a:o_a + 2 * DSA_WIDTH][:, paired].astype(BF16)
        wavt = wl[:, o_a + 2 * DSA_WIDTH:o_a + 3 * DSA_WIDTH].T.astype(BF16)
        wi = jnp.pad(wl[:, o_i:o_i + n_idx], ((0, 0), (0, IDX_QW + LANES - n_idx))).astype(BF16)
        wb = wl[:, o_b:o_b + 2 * DIFF_QK_WIDTH][:, paired].astype(BF16)
        wbvt = wl[:, o_b + 2 * DIFF_QK_WIDTH:o_b + 2 * DIFF_QK_WIDTH + DIFF_WIDTH].T.astype(BF16)
        wg = wl[:, o_g:o_g + N_BRANCH * D_MODEL].astype(BF16)

        (q_a, k_a, vt_a, q_b, k_b, vt_b, q_i, k_i, w_it) = _in_projection(
            x2.reshape(B, S, D), pos_t, row(norm_mix_g[l]), wa, wavt, wb, wbvt, wi, inv_col,
            _pad_lanes(idx_k_norm_g[l]), _pad_lanes(idx_k_norm_b[l]))

        o_dsa = _dsa_branch(q_i, w_it, k_i, q_a, k_a, vt_a)
        o_diff = _diff_branch(q_b, k_b, vt_b,
                              row(diff_lambda_q1[l]), row(diff_lambda_k1[l]),
                              row(diff_lambda_q2[l]), row(diff_lambda_k2[l]),
                              row(diff_subln_g[l]), lam_init)

        x2 = _merge_project(x2, o_dsa, o_diff.reshape(M, DIFF_WIDTH),
                            row(norm_mix_g[l]), wg, row(gate_b[l]),
                            w_branch_dsa[l].astype(BF16), w_branch_diff[l].astype(BF16),
                            w_out[l].astype(BF16))
        x2 = _ffn(x2, row(norm_ffn_g[l]), w_ffn_in[l].astype(BF16), w_ffn_out[l].astype(BF16),
                  row(norm_final_g), final_norm=(l == depth - 1))

    return x2.reshape(B, S, D)
```

```python
import functools
import math

import jax
import jax.numpy as jnp
from jax import lax
from jax.experimental import pallas as pl
from jax.experimental.pallas import tpu as pltpu

F32 = jnp.float32
BF16 = jnp.bfloat16

D_MODEL = 1024
HEAD_DIM = 64
DSA_HEADS = 8
DSA_WIDTH = DSA_HEADS * HEAD_DIM
IDX_HEADS = 8
IDX_DIM = 32
IDX_QW = IDX_HEADS * IDX_DIM
TOPK_MAX = 256
DIFF_HEADS = 4
DIFF_DIM = 64
DIFF_VDIM = 2 * DIFF_DIM
DIFF_QK_WIDTH = DIFF_HEADS * 2 * DIFF_DIM
DIFF_WIDTH = DIFF_HEADS * DIFF_VDIM
DIFF_SUBLN_EPS = 1e-5
N_BRANCH = 2
ROPE_THETA = 500000.0
ROT_DIM_HEAD = HEAD_DIM // 4
ROT_DIM_IDX = IDX_DIM // 4
FFN_MULT = 256
D_FF = -(-8 * D_MODEL // (3 * FFN_MULT)) * FFN_MULT
NORM_EPS = 1e-6

LANES = 128
VMEM_LIMIT = 56 * 1024 * 1024
MASKED = -1e30
MAX_BISECT = 512
UNCHECKED_BISECT = 16
LOG2E = 1.4426950408889634
N_STREAMS = 8
ROWS = 32
RANK_ROWS = 256
DEN_ROWS = 16
TRIG_ROWS = 16
TRIG_ONE = 12
TRIG_ZERO = TRIG_ROWS + 12

NT_DIMS = (((1,), (1,)), ((), ()))


def _cparams(sem):
    return pltpu.CompilerParams(dimension_semantics=sem, vmem_limit_bytes=VMEM_LIMIT)


def _rmsnorm(x, g, eps):
    return x * lax.rsqrt(jnp.mean(x * x, axis=-1, keepdims=True) + eps) * g


def _rope_block(y, cos, sin_lo, sin_hi, half):
    return (y * cos + pltpu.roll(y, half, 1) * sin_hi
            + pltpu.roll(y, LANES - half, 1) * sin_lo)


def _trig_table(pos_t, inv_col):
    ang_t = inv_col * pos_t
    pad = jnp.zeros((LANES - 2 * TRIG_ROWS, ang_t.shape[1]), F32)
    return jnp.concatenate([jnp.cos(ang_t), jnp.sin(ang_t), pad], axis=0).T


def _rope_tables(tbl, first, period, half):
    d = lax.broadcasted_iota(jnp.int32, tbl.shape, 1) % period
    rot = d < 2 * half
    take = lambda idx: jnp.take_along_axis(tbl, idx, axis=1)
    cos = take(jnp.where(rot, first + d % half, TRIG_ONE))
    sin_lo = -take(jnp.where(d < half, TRIG_ROWS + first + d, TRIG_ZERO))
    sin_hi = take(jnp.where(rot & (d >= half), TRIG_ROWS + first + d % half, TRIG_ZERO))
    return cos, sin_lo, sin_hi


def _rope_tables_paired(tbl):
    d = lax.broadcasted_iota(jnp.int32, tbl.shape, 1)
    x1 = d < ROT_DIM_HEAD
    rot = x1 | ((d >= LANES // 2) & (d < LANES // 2 + ROT_DIM_HEAD))
    f = d % (ROT_DIM_HEAD // 2)
    cos = jnp.take_along_axis(tbl, jnp.where(rot, f, TRIG_ONE), axis=1)
    sin = jnp.take_along_axis(tbl, jnp.where(rot, TRIG_ROWS + f, TRIG_ZERO), axis=1)
    return cos, jnp.where(x1, -sin, sin)


def _paired_head_lanes(second):
    lane = lax.broadcasted_iota(jnp.int32, (1, LANES), 1)
    half = ROT_DIM_HEAD // 2
    in_second = ((lane >= half) & (lane < 2 * half)) | (lane >= LANES // 2 + half)
    return in_second if second else jnp.logical_not(in_second)


def _paired_columns(width):
    half = ROT_DIM_HEAD // 2
    perm = list(range(LANES))
    perm[half:2 * half] = range(HEAD_DIM, HEAD_DIM + half)
    perm[LANES // 2:LANES // 2 + half] = range(half, 2 * half)
    return jnp.asarray([(c // LANES) * LANES + perm[c % LANES] for c in range(width)], jnp.int32)


def _inproj_kernel(x_ref, post_ref, g_ref, wa_ref, wavt_ref, wb_ref, wbvt_ref, wi_ref, invcol_ref,
                   lng_ref, lnb_ref,
                   qa_ref, ka_ref, vat_ref, qb_ref, kb_ref, vbt_ref, qi_ref, ki_ref, wit_ref):
    h = _rmsnorm(x_ref[0], g_ref[...], NORM_EPS).astype(BF16)
    tbl = _trig_table(post_ref[0], invcol_ref[...])
    cos_k, sin_k = _rope_tables_paired(tbl)
    cos_i, slo_i, shi_i = _rope_tables(tbl, ROT_DIM_HEAD // 2, IDX_DIM, ROT_DIM_IDX // 2)
    q_scale = HEAD_DIM ** -0.5 * LOG2E
    cos_q, sin_q = cos_k * q_scale, sin_k * q_scale

    def branch(w_ref, wvt_ref, q_ref, k_ref, vt_ref):
        for grp, (o_ref, cos, sin) in enumerate(((q_ref, cos_q, sin_q), (k_ref, cos_k, sin_k))):
            y = jnp.dot(h, w_ref[:, grp * 512:(grp + 1) * 512], preferred_element_type=F32)
            for j in range(512 // LANES):
                blk = y[:, j * LANES:(j + 1) * LANES]
                blk = blk * cos + pltpu.roll(blk, LANES // 2, 1) * sin
                o_ref[0, :, j * LANES:(j + 1) * LANES] = blk.astype(o_ref.dtype)
        vt_ref[0] = lax.dot_general(wvt_ref[...], h, NT_DIMS,
                                    preferred_element_type=F32).astype(vt_ref.dtype)

    branch(wa_ref, wavt_ref, qa_ref, ka_ref, vat_ref)
    branch(wb_ref, wbvt_ref, qb_ref, kb_ref, vbt_ref)

    r = jnp.dot(h, wi_ref[...], preferred_element_type=F32)
    for j in range(IDX_QW // LANES):
        blk = _rope_block(r[:, j * LANES:(j + 1) * LANES], cos_i, slo_i, shi_i, ROT_DIM_IDX // 2)
        qi_ref[0, :, j * LANES:(j + 1) * LANES] = blk.astype(qi_ref.dtype)
    kw = r[:, IDX_QW:IDX_QW + LANES]
    lane = lax.broadcasted_iota(jnp.int32, (1, LANES), 1)
    is_k = lane < IDX_DIM
    mu = jnp.sum(jnp.where(is_k, kw, 0.0), axis=-1, keepdims=True) * (1.0 / IDX_DIM)
    xc = jnp.where(is_k, kw - mu, 0.0)
    var = jnp.sum(xc * xc, axis=-1, keepdims=True) * (1.0 / IDX_DIM)
    kn = xc * lax.rsqrt(var + NORM_EPS) * lng_ref[...] + lnb_ref[...]
    kn = _rope_block(kn, cos_i, slo_i, shi_i, ROT_DIM_IDX // 2)
    ki_ref[0] = kn[:, :IDX_DIM].astype(ki_ref.dtype)
    w_scale = IDX_HEADS ** -0.5 * IDX_DIM ** -0.5
    wit_ref[0] = kw.T[IDX_DIM:IDX_DIM + IDX_HEADS, :] * w_scale


def _in_projection(x, pos_t, g, wa, wavt, wb, wbvt, wi, inv_col, lng, lnb, tm=256):
    B, S, _ = x.shape
    row = lambda w: pl.BlockSpec((1, tm, w), lambda b, i: (b, i, 0))
    col = lambda r: pl.BlockSpec((1, r, tm), lambda b, i: (b, 0, i))
    full = lambda a: pl.BlockSpec(a.shape, lambda b, i: (0,) * a.ndim,
                                  pipeline_mode=pl.Buffered(1))
    sds = jax.ShapeDtypeStruct
    out_shape = [sds((B, S, 512), BF16), sds((B, S, 512), BF16), sds((B, 512, S), BF16),
                 sds((B, S, 512), BF16), sds((B, S, 512), BF16), sds((B, 512, S), BF16),
                 sds((B, S, IDX_QW), BF16), sds((B, S, IDX_DIM), BF16), sds((B, IDX_HEADS, S), F32)]
    return pl.pallas_call(
        _inproj_kernel,
        grid=(B, S // tm),
        in_specs=[row(D_MODEL), col(1), full(g), full(wa), full(wavt), full(wb), full(wbvt),
                  full(wi), full(inv_col), full(lng), full(lnb)],
        out_specs=[row(512), row(512), col(512), row(512), row(512), col(512),
                   row(IDX_QW), row(IDX_DIM), col(IDX_HEADS)],
        out_shape=out_shape,
        compiler_params=_cparams(("parallel", "parallel")),
        name="in_projection",
    )(x, pos_t, g, wa, wavt, wb, wbvt, wi, inv_col, lng, lnb)


def _init_streams(q_ref, qpad_ref, m_ref, acc_ref):
    for s in range(N_STREAMS):
        blk = q_ref[0, :, (s // 2) * LANES:(s // 2 + 1) * LANES]
        qpad_ref[s] = jnp.where(_paired_head_lanes(s % 2 == 1), blk, jnp.zeros_like(blk))
    m_ref[...] = jnp.full(m_ref.shape, MASKED, F32)
    acc_ref[...] = jnp.zeros(acc_ref.shape, F32)


def _attend_chunk(k_ref, vt_ref, qpad_ref, m_ref, acc_ref, st_ref, off, bias,
                  *, tkv, dv, v_head, den_on_mxu, between=None):
    col_max = []
    for s in range(N_STREAMS):
        kp = k_ref[0, pl.ds(off, tkv), (s // 2) * LANES:(s // 2 + 1) * LANES]
        st = lax.dot_general(kp, qpad_ref[s], NT_DIMS, preferred_element_type=F32)
        if bias is not None:
            st = st + bias
        st_ref[s] = st
        col_max.append(jnp.max(st, axis=0, keepdims=True))
    extra = None if between is None else between()
    for s in range(N_STREAMS):
        m_old = m_ref[s]
        m_new = jnp.maximum(m_old, col_max[s])
        alpha = jnp.exp2(m_old - m_new)
        r0 = v_head(s) * dv
        v_c = vt_ref[0, r0:r0 + dv, pl.ds(off, tkv)]
        if den_on_mxu:
            p = jnp.exp2((st_ref[s] - m_new).astype(BF16))
            v_aug = jnp.concatenate([v_c, jnp.ones((DEN_ROWS, tkv), BF16)], axis=0)
            acc_ref[s] = alpha * acc_ref[s] + jnp.dot(v_aug, p, preferred_element_type=F32)
        else:
            p = jnp.exp2(st_ref[s] - m_new)
            acc_ref[s, 0:dv, :] = alpha * acc_ref[s, 0:dv, :] + jnp.dot(
                v_c, p.astype(BF16), preferred_element_type=F32)
            acc_ref[s, dv:dv + 1, :] = (alpha * acc_ref[s, dv:dv + 1, :]
                                        + jnp.sum(p, axis=0, keepdims=True))
        m_ref[s] = m_new
    return extra


def _normalised(acc_ref, s, dv):
    return acc_ref[s, 0:dv, :] / acc_ref[s, dv:dv + 1, :]


def _indexer_operands(qi_ref, wt_ref):
    qx = qi_ref[0]
    q2 = jnp.concatenate([qx[:, h * IDX_DIM:(h + 1) * IDX_DIM] for h in range(IDX_HEADS)], axis=0)
    return q2, wt_ref[0]


def _score_chunk(kidx_ref, score_ref, slot, q2, wt, qpos, off, carry, *, tq, tk, all_causal=False):
    mx, mn, c0, cp = carry
    kc = kidx_ref[0, pl.ds(off, tk), :]
    logits = lax.dot_general(kc, q2, NT_DIMS, preferred_element_type=F32)
    sc = jnp.maximum(logits[:, :tq], 0.0) * wt[0:1, :]
    for h in range(1, IDX_HEADS):
        sc = sc + jnp.maximum(logits[:, h * tq:(h + 1) * tq], 0.0) * wt[h:h + 1, :]
    if all_causal:
        s_hi = s_lo = sc
    else:
        causal = (off + lax.broadcasted_iota(jnp.int32, (tk, tq), 0)) <= qpos
        s_hi = jnp.where(causal, sc, -jnp.inf)
        s_lo = jnp.where(causal, sc, jnp.inf)
    score_ref[slot, pl.ds(off, tk), :] = s_hi
    for j in range(tk // ROWS):
        b_hi = s_hi[j * ROWS:(j + 1) * ROWS]
        mx = jnp.maximum(mx, b_hi)
        mn = jnp.minimum(mn, s_lo[j * ROWS:(j + 1) * ROWS])
        c0 = c0 + jnp.where(b_hi >= 0.0, 1.0, 0.0)
        cp = cp + jnp.where(b_hi > 0.0, 1.0, 0.0)
    return mx, mn, c0, cp


def _score_carry(tq):
    zero = jnp.zeros((ROWS, tq), F32)
    return zero - jnp.inf, zero + jnp.inf, zero, zero


def _select_threshold(score_ref, thr_ref, slot, nck, qpos, carry, *, tq, tk, ktop):
    mx, mn, c0, cp = carry
    hi0 = jnp.max(mx, axis=0, keepdims=True)
    lo0 = jnp.min(mn, axis=0, keepdims=True)
    n_ge0 = jnp.sum(c0, axis=0, keepdims=True)
    n_gt0 = jnp.sum(cp, axis=0, keepdims=True)
    zero = jnp.zeros((ROWS, tq), F32)

    def count(pred):
        def body(c, acc):
            off = pl.multiple_of(c * tk, tk)
            for j in range(tk // ROWS):
                x = score_ref[slot, pl.ds(pl.multiple_of(off + j * ROWS, ROWS), ROWS), :]
                acc = acc + jnp.where(pred(x), 1.0, 0.0)
            return acc
        return jnp.sum(lax.fori_loop(0, nck, body, zero), axis=0, keepdims=True)

    n_valid = (qpos + 1).astype(F32)
    kk = jnp.minimum(n_valid, float(ktop))

    all_in = n_valid <= kk
    above = jnp.logical_and(jnp.logical_not(all_in), n_gt0 >= kk)
    below = jnp.logical_and(jnp.logical_not(all_in), n_ge0 < kk)
    lo = jnp.where(all_in, lo0, jnp.where(below, lo0, 0.0))
    clo = jnp.where(all_in, n_valid, jnp.where(below, n_valid, n_ge0))
    hi = jnp.where(below, 0.0, hi0)
    fin = jnp.where(jnp.logical_or(above, below), 0.0, 1.0)

    def bis_cond(st):
        return jnp.logical_and(st[0] < MAX_BISECT, jnp.min(st[-1]) < 0.5)

    def bis_body(st):
        it, lo, hi, clo, chi, fin = st
        mid0 = 0.5 * lo + 0.5 * hi
        collapsed = jnp.logical_or(mid0 <= lo, mid0 >= hi)
        mid = jnp.where(collapsed, hi, mid0)
        c = count(lambda x: x >= mid)
        ge = c >= kk
        live = fin < 0.5
        up = jnp.logical_and(live, ge)
        dn = jnp.logical_and(live, jnp.logical_not(ge))
        lo = jnp.where(up, mid, lo)
        clo = jnp.where(up, c, clo)
        hi = jnp.where(dn, mid, hi)
        chi = jnp.where(dn, c, chi)
        done = jnp.logical_or(c == kk, collapsed)
        fin = jnp.where(jnp.logical_and(live, done), 1.0, fin)
        return it + 1, lo, hi, clo, chi, fin

    state = lax.fori_loop(0, UNCHECKED_BISECT, lambda _, st: bis_body(st),
                          (jnp.int32(0), lo, hi, clo, jnp.zeros_like(lo), fin))
    _, thr, hi, clo, chi, _ = lax.while_loop(bis_cond, bis_body, state)
    thr_ref[slot] = thr
    tied = clo > kk
    has_tie = jnp.max(jnp.where(tied, 1.0, 0.0)) > 0.5

    @pl.when(has_tie)
    def _():
        n_gt = jnp.where(jnp.logical_or(above, below),
                         jnp.where(thr >= hi, 0.0, chi), n_gt0)
        need = jnp.where(tied, kk - n_gt, float(2 ** 24))
        lower = (lax.broadcasted_iota(jnp.int32, (RANK_ROWS, RANK_ROWS), 0)
                 >= lax.broadcasted_iota(jnp.int32, (RANK_ROWS, RANK_ROWS), 1))
        lower = jnp.where(lower, 1.0, 0.0).astype(BF16)

        def body(c, need):
            rows = [pl.multiple_of(c * tk + j * RANK_ROWS, RANK_ROWS)
                    for j in range(tk // RANK_ROWS)]
            xs = [score_ref[slot, pl.ds(r0, RANK_ROWS), :] for r0 in rows]
            ranks = [jnp.dot(lower, jnp.where(x == thr, 1.0, 0.0).astype(BF16),
                             preferred_element_type=F32) for x in xs]
            for r0, x, rank in zip(rows, xs, ranks):
                kept = jnp.where(x > thr, 1.0,
                                 jnp.where(rank <= need, jnp.where(x == thr, 1.0, -1.0), -1.0))
                score_ref[slot, pl.ds(r0, RANK_ROWS), :] = kept
                need = need - rank[RANK_ROWS - 1:RANK_ROWS, :]
            return need
        lax.fori_loop(0, nck, body, need)
        thr_ref[slot] = jnp.zeros((1, tq), F32)


def _dsa_kernel(qi_ref, wt_ref, qin_ref, wtn_ref, kidx_ref, q_ref, k_ref, vt_ref, o_ref,
                score_ref, thr_ref, qpad_ref, m_ref, acc_ref, st_ref,
                *, tq, tk, ktop):
    qi = pl.program_id(1)
    slot = qi % 2
    nslot = 1 - slot
    chunks = lambda blk: ((blk + 1) * tq + tk - 1) // tk
    nck = chunks(qi)
    lanes_q = lax.broadcasted_iota(jnp.int32, (1, tq), 1)
    select = functools.partial(_select_threshold, score_ref, thr_ref, tq=tq, tk=tk, ktop=ktop)

    @pl.when(qi == 0)
    def _():
        q2, wt = _indexer_operands(qi_ref, wt_ref)

        def body(c, carry):
            return _score_chunk(kidx_ref, score_ref, 0, q2, wt, lanes_q,
                                pl.multiple_of(c * tk, tk), carry, tq=tq, tk=tk)
        select(0, nck, lanes_q, lax.fori_loop(0, nck, body, _score_carry(tq)))

    _init_streams(q_ref, qpad_ref, m_ref, acc_ref)
    thr_sel = thr_ref[slot]

    def attend(off, between=None):
        bias = jnp.where(score_ref[slot, pl.ds(off, tk), :] >= thr_sel, 0.0, MASKED)
        return _attend_chunk(k_ref, vt_ref, qpad_ref, m_ref, acc_ref, st_ref, off, bias,
                             tkv=tk, dv=HEAD_DIM, v_head=lambda s: s, den_on_mxu=True,
                             between=between)

    has_next = qi + 1 < pl.num_programs(1)

    @pl.when(has_next)
    def _():
        q2, wt = _indexer_operands(qin_ref, wtn_ref)
        qpos = (qi + 1) * tq + lanes_q
        score = functools.partial(_score_chunk, kidx_ref, score_ref, nslot, q2, wt, qpos,
                                  tq=tq, tk=tk)

        def paired(all_causal):
            def body(c, carry):
                off = pl.multiple_of(c * tk, tk)
                return attend(off, between=lambda: score(off, carry, all_causal=all_causal))
            return body
        nfull_next = jnp.minimum(((qi + 1) * tq + 1) // tk, nck)
        carry = lax.fori_loop(0, nfull_next, paired(True), _score_carry(tq))
        carry = lax.fori_loop(nfull_next, nck, paired(False), carry)
        nck_next = chunks(qi + 1)
        carry = lax.fori_loop(nck, nck_next,
                              lambda c, carry: score(pl.multiple_of(c * tk, tk), carry), carry)
        select(nslot, nck_next, qpos, carry)

    @pl.when(jnp.logical_not(has_next))
    def _():
        def body(c, _):
            attend(pl.multiple_of(c * tk, tk))
            return 0
        lax.fori_loop(0, nck, body, 0)

    for s in range(N_STREAMS):
        o_ref[0, s * HEAD_DIM:(s + 1) * HEAD_DIM, :] = _normalised(
            acc_ref, s, HEAD_DIM).astype(o_ref.dtype)


def _resident(shape):
    return pl.BlockSpec(shape, lambda b, i: (b, 0, 0), pipeline_mode=pl.Buffered(1))


def _dsa_branch(q_i, w_it, k_i, q, k, vt, tq=256, tk=512):
    B, S, W = q.shape
    nq = S // tq
    ktop = min(TOPK_MAX, S // 4)
    kern = functools.partial(_dsa_kernel, tq=tq, tk=tk, ktop=ktop)
    nxt = lambda i: jnp.minimum(i + 1, nq - 1)
    return pl.pallas_call(
        kern,
        grid=(B, nq),
        in_specs=[pl.BlockSpec((1, tq, IDX_QW), lambda b, i: (b, i, 0)),
                  pl.BlockSpec((1, IDX_HEADS, tq), lambda b, i: (b, 0, i)),
                  pl.BlockSpec((1, tq, IDX_QW), lambda b, i: (b, nxt(i), 0)),
                  pl.BlockSpec((1, IDX_HEADS, tq), lambda b, i: (b, 0, nxt(i))),
                  _resident((1, S, IDX_DIM)),
                  pl.BlockSpec((1, tq, W), lambda b, i: (b, i, 0)),
                  _resident((1, S, W)), _resident((1, W, S))],
        out_specs=pl.BlockSpec((1, W, tq), lambda b, i: (b, 0, i)),
        out_shape=jax.ShapeDtypeStruct((B, W, S), BF16),
        scratch_shapes=[pltpu.VMEM((2, S, tq), F32), pltpu.VMEM((2, 1, tq), F32),
                        pltpu.VMEM((N_STREAMS, tq, LANES), BF16),
                        pltpu.VMEM((N_STREAMS, 1, tq), F32),
                        pltpu.VMEM((N_STREAMS, HEAD_DIM + DEN_ROWS, tq), F32),
                        pltpu.VMEM((N_STREAMS, tk, tq), F32)],
        compiler_params=_cparams(("arbitrary", "arbitrary")),
        name="dsa_branch",
    )(q_i, w_it, q_i, w_it, k_i, q, k, vt)


def _diff_kernel(q_ref, k_ref, vt_ref, lq1_ref, lk1_ref, lq2_ref, lk2_ref, g_ref, o_ref,
                 qpad_ref, m_ref, acc_ref, ot_ref, st_ref, *, tq, tkv, lam_init):
    qi = pl.program_id(1)
    nck = ((qi + 1) * tq + tkv - 1) // tkv
    nfull = (qi * tq + 1) // tkv
    lam = (jnp.exp(jnp.sum(lq1_ref[...] * lk1_ref[...], axis=-1, keepdims=True))
           - jnp.exp(jnp.sum(lq2_ref[...] * lk2_ref[...], axis=-1, keepdims=True)) + lam_init)
    _init_streams(q_ref, qpad_ref, m_ref, acc_ref)
    attend = functools.partial(_attend_chunk, k_ref, vt_ref, qpad_ref, m_ref, acc_ref,
                               st_ref, tkv=tkv, dv=DIFF_VDIM, v_head=lambda s: s // 2,
                               den_on_mxu=False)

    def full_body(c, _):
        attend(pl.multiple_of(c * tkv, tkv), None)
        return 0
    lax.fori_loop(0, nfull, full_body, 0)

    qpos = qi * tq + lax.broadcasted_iota(jnp.int32, (1, tq), 1)
    krow = lax.broadcasted_iota(jnp.int32, (tkv, tq), 0)

    def diag_body(c, _):
        off = pl.multiple_of(c * tkv, tkv)
        attend(off, jnp.where((off + krow) <= qpos, 0.0, MASKED))
        return 0
    lax.fori_loop(nfull, nck, diag_body, 0)

    for h in range(DIFF_HEADS):
        o1 = _normalised(acc_ref, 2 * h, DIFF_VDIM)
        o2 = _normalised(acc_ref, 2 * h + 1, DIFF_VDIM)
        ot_ref[h * DIFF_VDIM:(h + 1) * DIFF_VDIM, :] = o1 - lam * o2
    o = ot_ref[...].T
    for h in range(DIFF_HEADS):
        blk = _rmsnorm(o[:, h * DIFF_VDIM:(h + 1) * DIFF_VDIM], g_ref[...], DIFF_SUBLN_EPS)
        o_ref[0, :, h * DIFF_VDIM:(h + 1) * DIFF_VDIM] = (blk * (1.0 - lam_init)).astype(o_ref.dtype)


def _diff_branch(q, k, vt, lq1, lk1, lq2, lk2, g, lam_init, tq=512, tkv=512):
    B, S, W = q.shape
    kern = functools.partial(_diff_kernel, tq=tq, tkv=tkv, lam_init=lam_init)
    vec = lambda a: pl.BlockSpec(a.shape, lambda b, i: (0, 0))
    return pl.pallas_call(
        kern,
        grid=(B, S // tq),
        in_specs=[pl.BlockSpec((1, tq, W), lambda b, i: (b, i, 0)),
                  _resident((1, S, W)), _resident((1, W, S)),
                  vec(lq1), vec(lk1), vec(lq2), vec(lk2), vec(g)],
        out_specs=pl.BlockSpec((1, tq, W), lambda b, i: (b, i, 0)),
        out_shape=jax.ShapeDtypeStruct((B, S, W), BF16),
        scratch_shapes=[pltpu.VMEM((N_STREAMS, tq, LANES), BF16),
                        pltpu.VMEM((N_STREAMS, 1, tq), F32),
                        pltpu.VMEM((N_STREAMS, DIFF_VDIM + DEN_ROWS, tq), F32),
                        pltpu.VMEM((W, tq), F32),
                        pltpu.VMEM((N_STREAMS, tkv, tq), F32)],
        compiler_params=_cparams(("parallel", "arbitrary")),
        name="diff_branch",
    )(q, k, vt, lq1, lk1, lq2, lk2, g)


def _merge_kernel(x_ref, oat_ref, ob_ref, g_ref, wg_ref, gb_ref, wa_ref, wb_ref, wo_ref, y_ref):
    x = x_ref[...]
    h = _rmsnorm(x, g_ref[...], NORM_EPS).astype(BF16)
    gates = jax.nn.sigmoid(jnp.dot(h, wg_ref[...], preferred_element_type=F32) + gb_ref[...])
    pa = lax.dot_general(oat_ref[0], wa_ref[...], (((0,), (0,)), ((), ())),
                         preferred_element_type=F32)
    pb = jnp.dot(ob_ref[...], wb_ref[...], preferred_element_type=F32)
    merged = gates[:, :D_MODEL] * pa + gates[:, D_MODEL:] * pb
    y_ref[...] = x + jnp.dot(merged.astype(BF16), wo_ref[...], preferred_element_type=F32)


def _merge_project(x2, oat, ob, g, wg, gb, wa, wb, wo, tm=512):
    M = x2.shape[0]
    per_batch = oat.shape[2] // tm
    row = lambda w: pl.BlockSpec((tm, w), lambda i: (i, 0))
    full = lambda a: pl.BlockSpec(a.shape, lambda i: (0,) * a.ndim, pipeline_mode=pl.Buffered(1))
    col = pl.BlockSpec((1, oat.shape[1], tm), lambda i: (i // per_batch, 0, i % per_batch))
    return pl.pallas_call(
        _merge_kernel,
        grid=(M // tm,),
        in_specs=[row(D_MODEL), col, row(512), full(g), full(wg), full(gb),
                  full(wa), full(wb), full(wo)],
        out_specs=row(D_MODEL),
        out_shape=jax.ShapeDtypeStruct((M, D_MODEL), F32),
        compiler_params=_cparams(("parallel",)),
        name="merge_project",
    )(x2, oat, ob, g, wg, gb, wa, wb, wo)


def _ffn_kernel(x_ref, g_ref, win_ref, wdown_ref, gf_ref, y_ref, *, final_norm):
    x = x_ref[...]
    h = _rmsnorm(x, g_ref[...], NORM_EPS).astype(BF16)
    gate = jnp.dot(h, win_ref[:, :D_FF], preferred_element_type=F32)
    up = jnp.dot(h, win_ref[:, D_FF:], preferred_element_type=F32)
    act = (gate * jax.nn.sigmoid(gate) * up).astype(BF16)
    y = x + jnp.dot(act, wdown_ref[...], preferred_element_type=F32)
    if final_norm:
        y = _rmsnorm(y, gf_ref[...], NORM_EPS)
    y_ref[...] = y


def _ffn(x2, g, win, wdown, gf, final_norm, tm=512):
    M = x2.shape[0]
    row = lambda w: pl.BlockSpec((tm, w), lambda i: (i, 0))
    full = lambda a: pl.BlockSpec(a.shape, lambda i: (0,) * a.ndim, pipeline_mode=pl.Buffered(1))
    return pl.pallas_call(
        functools.partial(_ffn_kernel, final_norm=final_norm),
        grid=(M // tm,),
        in_specs=[row(D_MODEL), full(g), full(win), full(wdown), full(gf)],
        out_specs=row(D_MODEL),
        out_shape=jax.ShapeDtypeStruct((M, D_MODEL), F32),
        compiler_params=_cparams(("parallel",)),
        name="swiglu_ffn",
    )(x2, g, win, wdown, gf)


def _inv_freq(rot_dim):
    return jnp.power(jnp.float32(ROPE_THETA), -jnp.arange(0, rot_dim, 2, dtype=F32) / rot_dim)


def _inv_freq_column():
    inv = jnp.concatenate([_inv_freq(ROT_DIM_HEAD), _inv_freq(ROT_DIM_IDX)])
    return jnp.pad(inv, (0, TRIG_ROWS - inv.shape[0]))[:, None]


def _pad_lanes(v):
    return jnp.pad(v.astype(F32), (0, LANES - v.shape[0]))[None, :]


def kernel(x, positions, norm_mix_g, w_in, idx_k_norm_g, idx_k_norm_b, diff_lambda_q1,
           diff_lambda_k1, diff_lambda_q2, diff_lambda_k2, diff_subln_g, gate_b, w_branch_dsa,
           w_branch_diff, w_out, norm_ffn_g, w_ffn_in, w_ffn_out, norm_final_g):
    B, S, D = x.shape
    M = B * S
    depth = w_in.shape[0]
    if depth == 0:
        raise ValueError("depth must be positive")
    pos_t = positions.astype(F32)[:, None, :]
    inv_col = _inv_freq_column()
    paired = _paired_columns(2 * DSA_WIDTH)
    row = lambda v: v.astype(F32)[None, :]

    n_idx = IDX_QW + IDX_DIM + IDX_HEADS
    o_a = 0
    o_i = 3 * DSA_WIDTH
    o_b = o_i + n_idx
    o_g = o_b + 2 * DIFF_QK_WIDTH + DIFF_WIDTH

    x2 = x.reshape(M, D)
    for l in range(depth):
        lam_init = 0.8 - 0.6 * math.exp(-0.3 * l)
        wl = w_in[l]
        wa = wl[:, o_a:o_a + 2 * DSA_WIDTH][:, paired].astype(BF16)
        wavt = wl[:, o_a + 2 * DSA_WIDTH:o_a + 3 * DSA_WIDTH].T.astype(BF16)
        wi = jnp.pad(wl[:, o_i:o_i + n_idx], ((0, 0), (0, IDX_QW + LANES - n_idx))).astype(BF16)
        wb = wl[:, o_b:o_b + 2 * DIFF_QK_WIDTH][:, paired].astype(BF16)
        wbvt = wl[:, o_b + 2 * DIFF_QK_WIDTH:o_b + 2 * DIFF_QK_WIDTH + DIFF_WIDTH].T.astype(BF16)
        wg = wl[:, o_g:o_g + N_BRANCH * D_MODEL].astype(BF16)

        (q_a, k_a, vt_a, q_b, k_b, vt_b, q_i, k_i, w_it) = _in_projection(
            x2.reshape(B, S, D), pos_t, row(norm_mix_g[l]), wa, wavt, wb, wbvt, wi, inv_col,
            _pad_lanes(idx_k_norm_g[l]), _pad_lanes(idx_k_norm_b[l]))

        o_dsa = _dsa_branch(q_i, w_it, k_i, q_a, k_a, vt_a)
        o_diff = _diff_branch(q_b, k_b, vt_b,
                              row(diff_lambda_q1[l]), row(diff_lambda_k1[l]),
                              row(diff_lambda_q2[l]), row(diff_lambda_k2[l]),
                              row(diff_subln_g[l]), lam_init)

        x2 = _merge_project(x2, o_dsa, o_diff.reshape(M, DIFF_WIDTH),
                            row(norm_mix_g[l]), wg, row(gate_b[l]),
                            w_branch_dsa[l].astype(BF16), w_branch_diff[l].astype(BF16),
                            w_out[l].astype(BF16))
        x2 = _ffn(x2, row(norm_ffn_g[l]), w_ffn_in[l].astype(BF16), w_ffn_out[l].astype(BF16),
                  row(norm_final_g), final_norm=(l == depth - 1))

    return x2.reshape(B, S, D)
```

```python
import functools
import math

import jax
import jax.numpy as jnp
from jax import lax
from jax.experimental import pallas as pl
from jax.experimental.pallas import tpu as pltpu

F32 = jnp.float32
BF16 = jnp.bfloat16

D_MODEL = 1024
HEAD_DIM = 64
DSA_HEADS = 8
DSA_WIDTH = DSA_HEADS * HEAD_DIM
IDX_HEADS = 8
IDX_DIM = 32
IDX_QW = IDX_HEADS * IDX_DIM
TOPK_MAX = 256
DIFF_HEADS = 4
DIFF_DIM = 64
DIFF_VDIM = 2 * DIFF_DIM
DIFF_QK_WIDTH = DIFF_HEADS * 2 * DIFF_DIM
DIFF_WIDTH = DIFF_HEADS * DIFF_VDIM
DIFF_SUBLN_EPS = 1e-5
N_BRANCH = 2
ROPE_THETA = 500000.0
ROT_DIM_HEAD = HEAD_DIM // 4
ROT_DIM_IDX = IDX_DIM // 4
FFN_MULT = 256
D_FF = -(-8 * D_MODEL // (3 * FFN_MULT)) * FFN_MULT
NORM_EPS = 1e-6

LANES = 128
VMEM_LIMIT = 56 * 1024 * 1024
MASKED = -1e30
MAX_BISECT = 512
UNCHECKED_BISECT = 16
LOG2E = 1.4426950408889634
N_STREAMS = 8
ROWS = 32
MERGE_FFN_TM = 512
RANK_ROWS = 256
DEN_ROWS = 16
TRIG_ROWS = 16
TRIG_ONE = 12
TRIG_ZERO = TRIG_ROWS + 12

NT_DIMS = (((1,), (1,)), ((), ()))


def _cparams(sem):
    return pltpu.CompilerParams(dimension_semantics=sem, vmem_limit_bytes=VMEM_LIMIT)


def _rmsnorm(x, g, eps):
    return x * lax.rsqrt(jnp.mean(x * x, axis=-1, keepdims=True) + eps) * g


def _rope_block(y, cos, sin_lo, sin_hi, half):
    return (y * cos + pltpu.roll(y, half, 1) * sin_hi
            + pltpu.roll(y, LANES - half, 1) * sin_lo)


def _trig_table(pos_t, inv_col):
    ang_t = inv_col * pos_t
    pad = jnp.zeros((LANES - 2 * TRIG_ROWS, ang_t.shape[1]), F32)
    return jnp.concatenate([jnp.cos(ang_t), jnp.sin(ang_t), pad], axis=0).T


def _rope_tables(tbl, first, period, half):
    d = lax.broadcasted_iota(jnp.int32, tbl.shape, 1) % period
    rot = d < 2 * half
    take = lambda idx: jnp.take_along_axis(tbl, idx, axis=1)
    cos = take(jnp.where(rot, first + d % half, TRIG_ONE))
    sin_lo = -take(jnp.where(d < half, TRIG_ROWS + first + d, TRIG_ZERO))
    sin_hi = take(jnp.where(rot & (d >= half), TRIG_ROWS + first + d % half, TRIG_ZERO))
    return cos, sin_lo, sin_hi


def _rope_tables_paired(tbl):
    d = lax.broadcasted_iota(jnp.int32, tbl.shape, 1)
    x1 = d < ROT_DIM_HEAD
    rot = x1 | ((d >= LANES // 2) & (d < LANES // 2 + ROT_DIM_HEAD))
    f = d % (ROT_DIM_HEAD // 2)
    cos = jnp.take_along_axis(tbl, jnp.where(rot, f, TRIG_ONE), axis=1)
    sin = jnp.take_along_axis(tbl, jnp.where(rot, TRIG_ROWS + f, TRIG_ZERO), axis=1)
    return cos, jnp.where(x1, -sin, sin)


def _paired_head_lanes(second):
    lane = lax.broadcasted_iota(jnp.int32, (1, LANES), 1)
    half = ROT_DIM_HEAD // 2
    in_second = ((lane >= half) & (lane < 2 * half)) | (lane >= LANES // 2 + half)
    return in_second if second else jnp.logical_not(in_second)


def _paired_columns(width):
    half = ROT_DIM_HEAD // 2
    perm = list(range(LANES))
    perm[half:2 * half] = range(HEAD_DIM, HEAD_DIM + half)
    perm[LANES // 2:LANES // 2 + half] = range(half, 2 * half)
    return jnp.asarray([(c // LANES) * LANES + perm[c % LANES] for c in range(width)], jnp.int32)


def _inproj_kernel(x_ref, post_ref, g_ref, wa_ref, wavt_ref, wb_ref, wbvt_ref, wi_ref, invcol_ref,
                   lng_ref, lnb_ref,
                   qa_ref, ka_ref, vat_ref, qb_ref, kb_ref, vbt_ref, qi_ref, ki_ref, wit_ref):
    h = _rmsnorm(x_ref[0], g_ref[...], NORM_EPS).astype(BF16)
    tbl = _trig_table(post_ref[0], invcol_ref[...])
    cos_k, sin_k = _rope_tables_paired(tbl)
    cos_i, slo_i, shi_i = _rope_tables(tbl, ROT_DIM_HEAD // 2, IDX_DIM, ROT_DIM_IDX // 2)
    q_scale = HEAD_DIM ** -0.5 * LOG2E
    cos_q, sin_q = cos_k * q_scale, sin_k * q_scale

    def branch(w_ref, wvt_ref, q_ref, k_ref, vt_ref):
        for grp, (o_ref, cos, sin) in enumerate(((q_ref, cos_q, sin_q), (k_ref, cos_k, sin_k))):
            y = jnp.dot(h, w_ref[:, grp * 512:(grp + 1) * 512], preferred_element_type=F32)
            for j in range(512 // LANES):
                blk = y[:, j * LANES:(j + 1) * LANES]
                blk = blk * cos + pltpu.roll(blk, LANES // 2, 1) * sin
                o_ref[0, :, j * LANES:(j + 1) * LANES] = blk.astype(o_ref.dtype)
        vt_ref[0] = lax.dot_general(wvt_ref[...], h, NT_DIMS,
                                    preferred_element_type=F32).astype(vt_ref.dtype)

    branch(wa_ref, wavt_ref, qa_ref, ka_ref, vat_ref)
    branch(wb_ref, wbvt_ref, qb_ref, kb_ref, vbt_ref)

    r = jnp.dot(h, wi_ref[...], preferred_element_type=F32)
    for j in range(IDX_QW // LANES):
        blk = _rope_block(r[:, j * LANES:(j + 1) * LANES], cos_i, slo_i, shi_i, ROT_DIM_IDX // 2)
        qi_ref[0, :, j * LANES:(j + 1) * LANES] = blk.astype(qi_ref.dtype)
    kw = r[:, IDX_QW:IDX_QW + LANES]
    lane = lax.broadcasted_iota(jnp.int32, (1, LANES), 1)
    is_k = lane < IDX_DIM
    mu = jnp.sum(jnp.where(is_k, kw, 0.0), axis=-1, keepdims=True) * (1.0 / IDX_DIM)
    xc = jnp.where(is_k, kw - mu, 0.0)
    var = jnp.sum(xc * xc, axis=-1, keepdims=True) * (1.0 / IDX_DIM)
    kn = xc * lax.rsqrt(var + NORM_EPS) * lng_ref[...] + lnb_ref[...]
    kn = _rope_block(kn, cos_i, slo_i, shi_i, ROT_DIM_IDX // 2)
    ki_ref[0] = kn[:, :IDX_DIM].astype(ki_ref.dtype)
    w_scale = IDX_HEADS ** -0.5 * IDX_DIM ** -0.5
    wit_ref[0] = kw.T[IDX_DIM:IDX_DIM + IDX_HEADS, :] * w_scale


def _in_projection(x, pos_t, g, wa, wavt, wb, wbvt, wi, inv_col, lng, lnb, tm=256):
    B, S, _ = x.shape
    row = lambda w: pl.BlockSpec((1, tm, w), lambda b, i: (b, i, 0))
    col = lambda r: pl.BlockSpec((1, r, tm), lambda b, i: (b, 0, i))
    full = lambda a: pl.BlockSpec(a.shape, lambda b, i: (0,) * a.ndim,
                                  pipeline_mode=pl.Buffered(1))
    sds = jax.ShapeDtypeStruct
    out_shape = [sds((B, S, 512), BF16), sds((B, S, 512), BF16), sds((B, 512, S), BF16),
                 sds((B, S, 512), BF16), sds((B, S, 512), BF16), sds((B, 512, S), BF16),
                 sds((B, S, IDX_QW), BF16), sds((B, S, IDX_DIM), BF16), sds((B, IDX_HEADS, S), F32)]
    return pl.pallas_call(
        _inproj_kernel,
        grid=(B, S // tm),
        in_specs=[row(D_MODEL), col(1), full(g), full(wa), full(wavt), full(wb), full(wbvt),
                  full(wi), full(inv_col), full(lng), full(lnb)],
        out_specs=[row(512), row(512), col(512), row(512), row(512), col(512),
                   row(IDX_QW), row(IDX_DIM), col(IDX_HEADS)],
        out_shape=out_shape,
        compiler_params=_cparams(("parallel", "parallel")),
        name="in_projection",
    )(x, pos_t, g, wa, wavt, wb, wbvt, wi, inv_col, lng, lnb)


def _init_streams(q_ref, qpad_ref, m_ref, acc_ref):
    for s in range(N_STREAMS):
        blk = q_ref[0, :, (s // 2) * LANES:(s // 2 + 1) * LANES]
        qpad_ref[s] = jnp.where(_paired_head_lanes(s % 2 == 1), blk, jnp.zeros_like(blk))
    m_ref[...] = jnp.full(m_ref.shape, MASKED, F32)
    acc_ref[...] = jnp.zeros(acc_ref.shape, F32)


def _attend_chunk(k_ref, vt_ref, qpad_ref, m_ref, acc_ref, st_ref, off, bias,
                  *, tkv, dv, v_head, den_on_mxu, between=None):
    col_max = []
    for s in range(N_STREAMS):
        kp = k_ref[0, pl.ds(off, tkv), (s // 2) * LANES:(s // 2 + 1) * LANES]
        st = lax.dot_general(kp, qpad_ref[s], NT_DIMS, preferred_element_type=F32)
        if bias is not None:
            st = st + bias
        st_ref[s] = st
        col_max.append(jnp.max(st, axis=0, keepdims=True))
    extra = None if between is None else between()
    for s in range(N_STREAMS):
        m_old = m_ref[s]
        m_new = jnp.maximum(m_old, col_max[s])
        alpha = jnp.exp2(m_old - m_new)
        r0 = v_head(s) * dv
        v_c = vt_ref[0, r0:r0 + dv, pl.ds(off, tkv)]
        if den_on_mxu:
            p = jnp.exp2((st_ref[s] - m_new).astype(BF16))
            v_aug = jnp.concatenate([v_c, jnp.ones((DEN_ROWS, tkv), BF16)], axis=0)
            acc_ref[s] = alpha * acc_ref[s] + jnp.dot(v_aug, p, preferred_element_type=F32)
        else:
            p = jnp.exp2(st_ref[s] - m_new)
            acc_ref[s, 0:dv, :] = alpha * acc_ref[s, 0:dv, :] + jnp.dot(
                v_c, p.astype(BF16), preferred_element_type=F32)
            acc_ref[s, dv:dv + 1, :] = (alpha * acc_ref[s, dv:dv + 1, :]
                                        + jnp.sum(p, axis=0, keepdims=True))
        m_ref[s] = m_new
    return extra


def _normalised(acc_ref, s, dv):
    return acc_ref[s, 0:dv, :] / acc_ref[s, dv:dv + 1, :]


def _indexer_operands(qi_ref, wt_ref):
    qx = qi_ref[0]
    q2 = jnp.concatenate([qx[:, h * IDX_DIM:(h + 1) * IDX_DIM] for h in range(IDX_HEADS)], axis=0)
    return q2, wt_ref[0]


def _score_chunk(kidx_ref, score_ref, slot, q2, wt, qpos, off, carry, *, tq, tk, all_causal=False):
    mx, mn, c0, cp = carry
    kc = kidx_ref[0, pl.ds(off, tk), :]
    logits = lax.dot_general(kc, q2, NT_DIMS, preferred_element_type=F32)
    sc = jnp.maximum(logits[:, :tq], 0.0) * wt[0:1, :]
    for h in range(1, IDX_HEADS):
        sc = sc + jnp.maximum(logits[:, h * tq:(h + 1) * tq], 0.0) * wt[h:h + 1, :]
    if all_causal:
        s_hi = s_lo = sc
    else:
        causal = (off + lax.broadcasted_iota(jnp.int32, (tk, tq), 0)) <= qpos
        s_hi = jnp.where(causal, sc, -jnp.inf)
        s_lo = jnp.where(causal, sc, jnp.inf)
    score_ref[slot, pl.ds(off, tk), :] = s_hi
    for j in range(tk // ROWS):
        b_hi = s_hi[j * ROWS:(j + 1) * ROWS]
        mx = jnp.maximum(mx, b_hi)
        mn = jnp.minimum(mn, s_lo[j * ROWS:(j + 1) * ROWS])
        c0 = c0 + jnp.where(b_hi >= 0.0, 1.0, 0.0)
        cp = cp + jnp.where(b_hi > 0.0, 1.0, 0.0)
    return mx, mn, c0, cp


def _score_carry(tq):
    zero = jnp.zeros((ROWS, tq), F32)
    return zero - jnp.inf, zero + jnp.inf, zero, zero


def _select_threshold(score_ref, thr_ref, slot, nck, qpos, carry, *, tq, tk, ktop):
    mx, mn, c0, cp = carry
    hi0 = jnp.max(mx, axis=0, keepdims=True)
    lo0 = jnp.min(mn, axis=0, keepdims=True)
    n_ge0 = jnp.sum(c0, axis=0, keepdims=True)
    n_gt0 = jnp.sum(cp, axis=0, keepdims=True)
    zero = jnp.zeros((ROWS, tq), F32)

    def count(pred):
        def body(c, acc):
            off = pl.multiple_of(c * tk, tk)
            for j in range(tk // ROWS):
                x = score_ref[slot, pl.ds(pl.multiple_of(off + j * ROWS, ROWS), ROWS), :]
                acc = acc + jnp.where(pred(x), 1.0, 0.0)
            return acc
        return jnp.sum(lax.fori_loop(0, nck, body, zero), axis=0, keepdims=True)

    n_valid = (qpos + 1).astype(F32)
    kk = jnp.minimum(n_valid, float(ktop))

    all_in = n_valid <= kk
    above = jnp.logical_and(jnp.logical_not(all_in), n_gt0 >= kk)
    below = jnp.logical_and(jnp.logical_not(all_in), n_ge0 < kk)
    lo = jnp.where(all_in, lo0, jnp.where(below, lo0, 0.0))
    clo = jnp.where(all_in, n_valid, jnp.where(below, n_valid, n_ge0))
    hi = jnp.where(below, 0.0, hi0)
    fin = jnp.where(jnp.logical_or(above, below), 0.0, 1.0)

    def bis_cond(st):
        return jnp.logical_and(st[0] < MAX_BISECT, jnp.min(st[-1]) < 0.5)

    def bis_body(st):
        it, lo, hi, clo, chi, fin = st
        mid0 = 0.5 * lo + 0.5 * hi
        collapsed = jnp.logical_or(mid0 <= lo, mid0 >= hi)
        mid = jnp.where(collapsed, hi, mid0)
        c = count(lambda x: x >= mid)
        ge = c >= kk
        live = fin < 0.5
        up = jnp.logical_and(live, ge)
        dn = jnp.logical_and(live, jnp.logical_not(ge))
        lo = jnp.where(up, mid, lo)
        clo = jnp.where(up, c, clo)
        hi = jnp.where(dn, mid, hi)
        chi = jnp.where(dn, c, chi)
        done = jnp.logical_or(c == kk, collapsed)
        fin = jnp.where(jnp.logical_and(live, done), 1.0, fin)
        return it + 1, lo, hi, clo, chi, fin

    state = lax.fori_loop(0, UNCHECKED_BISECT, lambda _, st: bis_body(st),
                          (jnp.int32(0), lo, hi, clo, jnp.zeros_like(lo), fin))
    _, thr, hi, clo, chi, _ = lax.while_loop(bis_cond, bis_body, state)
    thr_ref[slot] = thr
    tied = clo > kk
    has_tie = jnp.max(jnp.where(tied, 1.0, 0.0)) > 0.5

    @pl.when(has_tie)
    def _():
        n_gt = jnp.where(jnp.logical_or(above, below),
                         jnp.where(thr >= hi, 0.0, chi), n_gt0)
        need = jnp.where(tied, kk - n_gt, float(2 ** 24))
        lower = (lax.broadcasted_iota(jnp.int32, (RANK_ROWS, RANK_ROWS), 0)
                 >= lax.broadcasted_iota(jnp.int32, (RANK_ROWS, RANK_ROWS), 1))
        lower = jnp.where(lower, 1.0, 0.0).astype(BF16)

        def body(c, need):
            rows = [pl.multiple_of(c * tk + j * RANK_ROWS, RANK_ROWS)
                    for j in range(tk // RANK_ROWS)]
            xs = [score_ref[slot, pl.ds(r0, RANK_ROWS), :] for r0 in rows]
            ranks = [jnp.dot(lower, jnp.where(x == thr, 1.0, 0.0).astype(BF16),
                             preferred_element_type=F32) for x in xs]
            for r0, x, rank in zip(rows, xs, ranks):
                kept = jnp.where(x > thr, 1.0,
                                 jnp.where(rank <= need, jnp.where(x == thr, 1.0, -1.0), -1.0))
                score_ref[slot, pl.ds(r0, RANK_ROWS), :] = kept
                need = need - rank[RANK_ROWS - 1:RANK_ROWS, :]
            return need
        lax.fori_loop(0, nck, body, need)
        thr_ref[slot] = jnp.zeros((1, tq), F32)


def _dsa_kernel(qi_ref, wt_ref, qin_ref, wtn_ref, kidx_ref, q_ref, k_ref, vt_ref, o_ref,
                score_ref, thr_ref, qpad_ref, m_ref, acc_ref, st_ref,
                *, tq, tk, ktop):
    qi = pl.program_id(1)
    slot = qi % 2
    nslot = 1 - slot
    chunks = lambda blk: ((blk + 1) * tq + tk - 1) // tk
    nck = chunks(qi)
    lanes_q = lax.broadcasted_iota(jnp.int32, (1, tq), 1)
    select = functools.partial(_select_threshold, score_ref, thr_ref, tq=tq, tk=tk, ktop=ktop)

    @pl.when(qi == 0)
    def _():
        q2, wt = _indexer_operands(qi_ref, wt_ref)

        def body(c, carry):
            return _score_chunk(kidx_ref, score_ref, 0, q2, wt, lanes_q,
                                pl.multiple_of(c * tk, tk), carry, tq=tq, tk=tk)
        select(0, nck, lanes_q, lax.fori_loop(0, nck, body, _score_carry(tq)))

    _init_streams(q_ref, qpad_ref, m_ref, acc_ref)
    thr_sel = thr_ref[slot]

    def attend(off, between=None):
        bias = jnp.where(score_ref[slot, pl.ds(off, tk), :] >= thr_sel, 0.0, MASKED)
        return _attend_chunk(k_ref, vt_ref, qpad_ref, m_ref, acc_ref, st_ref, off, bias,
                             tkv=tk, dv=HEAD_DIM, v_head=lambda s: s, den_on_mxu=True,
                             between=between)

    has_next = qi + 1 < pl.num_programs(1)

    @pl.when(has_next)
    def _():
        q2, wt = _indexer_operands(qin_ref, wtn_ref)
        qpos = (qi + 1) * tq + lanes_q
        score = functools.partial(_score_chunk, kidx_ref, score_ref, nslot, q2, wt, qpos,
                                  tq=tq, tk=tk)

        def paired(all_causal):
            def body(c, carry):
                off = pl.multiple_of(c * tk, tk)
                return attend(off, between=lambda: score(off, carry, all_causal=all_causal))
            return body
        nfull_next = jnp.minimum(((qi + 1) * tq + 1) // tk, nck)
        carry = lax.fori_loop(0, nfull_next, paired(True), _score_carry(tq))
        carry = lax.fori_loop(nfull_next, nck, paired(False), carry)
        nck_next = chunks(qi + 1)
        carry = lax.fori_loop(nck, nck_next,
                              lambda c, carry: score(pl.multiple_of(c * tk, tk), carry), carry)
        select(nslot, nck_next, qpos, carry)

    @pl.when(jnp.logical_not(has_next))
    def _():
        def body(c, _):
            attend(pl.multiple_of(c * tk, tk))
            return 0
        lax.fori_loop(0, nck, body, 0)

    for s in range(N_STREAMS):
        o_ref[0, s * HEAD_DIM:(s + 1) * HEAD_DIM, :] = _normalised(
            acc_ref, s, HEAD_DIM).astype(o_ref.dtype)


def _resident(shape):
    return pl.BlockSpec(shape, lambda b, i: (b, 0, 0), pipeline_mode=pl.Buffered(1))


def _dsa_branch(q_i, w_it, k_i, q, k, vt, tq=256, tk=512):
    B, S, W = q.shape
    nq = S // tq
    ktop = min(TOPK_MAX, S // 4)
    kern = functools.partial(_dsa_kernel, tq=tq, tk=tk, ktop=ktop)
    nxt = lambda i: jnp.minimum(i + 1, nq - 1)
    return pl.pallas_call(
        kern,
        grid=(B, nq),
        in_specs=[pl.BlockSpec((1, tq, IDX_QW), lambda b, i: (b, i, 0)),
                  pl.BlockSpec((1, IDX_HEADS, tq), lambda b, i: (b, 0, i)),
                  pl.BlockSpec((1, tq, IDX_QW), lambda b, i: (b, nxt(i), 0)),
                  pl.BlockSpec((1, IDX_HEADS, tq), lambda b, i: (b, 0, nxt(i))),
                  _resident((1, S, IDX_DIM)),
                  pl.BlockSpec((1, tq, W), lambda b, i: (b, i, 0)),
                  _resident((1, S, W)), _resident((1, W, S))],
        out_specs=pl.BlockSpec((1, W, tq), lambda b, i: (b, 0, i)),
        out_shape=jax.ShapeDtypeStruct((B, W, S), BF16),
        scratch_shapes=[pltpu.VMEM((2, S, tq), F32), pltpu.VMEM((2, 1, tq), F32),
                        pltpu.VMEM((N_STREAMS, tq, LANES), BF16),
                        pltpu.VMEM((N_STREAMS, 1, tq), F32),
                        pltpu.VMEM((N_STREAMS, HEAD_DIM + DEN_ROWS, tq), F32),
                        pltpu.VMEM((N_STREAMS, tk, tq), F32)],
        compiler_params=_cparams(("arbitrary", "arbitrary")),
        name="dsa_branch",
    )(q_i, w_it, q_i, w_it, k_i, q, k, vt)


def _diff_kernel(q_ref, k_ref, vt_ref, lq1_ref, lk1_ref, lq2_ref, lk2_ref, g_ref, o_ref,
                 qpad_ref, m_ref, acc_ref, ot_ref, st_ref, *, tq, tkv, lam_init):
    qi = pl.program_id(1)
    nck = ((qi + 1) * tq + tkv - 1) // tkv
    nfull = (qi * tq + 1) // tkv
    lam = (jnp.exp(jnp.sum(lq1_ref[...] * lk1_ref[...], axis=-1, keepdims=True))
           - jnp.exp(jnp.sum(lq2_ref[...] * lk2_ref[...], axis=-1, keepdims=True)) + lam_init)
    _init_streams(q_ref, qpad_ref, m_ref, acc_ref)
    attend = functools.partial(_attend_chunk, k_ref, vt_ref, qpad_ref, m_ref, acc_ref,
                               st_ref, tkv=tkv, dv=DIFF_VDIM, v_head=lambda s: s // 2,
                               den_on_mxu=False)

    def full_body(c, _):
        attend(pl.multiple_of(c * tkv, tkv), None)
        return 0
    lax.fori_loop(0, nfull, full_body, 0)

    qpos = qi * tq + lax.broadcasted_iota(jnp.int32, (1, tq), 1)
    krow = lax.broadcasted_iota(jnp.int32, (tkv, tq), 0)

    def diag_body(c, _):
        off = pl.multiple_of(c * tkv, tkv)
        attend(off, jnp.where((off + krow) <= qpos, 0.0, MASKED))
        return 0
    lax.fori_loop(nfull, nck, diag_body, 0)

    for h in range(DIFF_HEADS):
        o1 = _normalised(acc_ref, 2 * h, DIFF_VDIM)
        o2 = _normalised(acc_ref, 2 * h + 1, DIFF_VDIM)
        ot_ref[h * DIFF_VDIM:(h + 1) * DIFF_VDIM, :] = o1 - lam * o2
    o = ot_ref[...].T
    for h in range(DIFF_HEADS):
        blk = _rmsnorm(o[:, h * DIFF_VDIM:(h + 1) * DIFF_VDIM], g_ref[...], DIFF_SUBLN_EPS)
        o_ref[0, :, h * DIFF_VDIM:(h + 1) * DIFF_VDIM] = (blk * (1.0 - lam_init)).astype(o_ref.dtype)


def _diff_branch(q, k, vt, lq1, lk1, lq2, lk2, g, lam_init, tq=512, tkv=512):
    B, S, W = q.shape
    kern = functools.partial(_diff_kernel, tq=tq, tkv=tkv, lam_init=lam_init)
    vec = lambda a: pl.BlockSpec(a.shape, lambda b, i: (0, 0))
    return pl.pallas_call(
        kern,
        grid=(B, S // tq),
        in_specs=[pl.BlockSpec((1, tq, W), lambda b, i: (b, i, 0)),
                  _resident((1, S, W)), _resident((1, W, S)),
                  vec(lq1), vec(lk1), vec(lq2), vec(lk2), vec(g)],
        out_specs=pl.BlockSpec((1, tq, W), lambda b, i: (b, i, 0)),
        out_shape=jax.ShapeDtypeStruct((B, S, W), BF16),
        scratch_shapes=[pltpu.VMEM((N_STREAMS, tq, LANES), BF16),
                        pltpu.VMEM((N_STREAMS, 1, tq), F32),
                        pltpu.VMEM((N_STREAMS, DIFF_VDIM + DEN_ROWS, tq), F32),
                        pltpu.VMEM((W, tq), F32),
                        pltpu.VMEM((N_STREAMS, tkv, tq), F32)],
        compiler_params=_cparams(("parallel", "arbitrary")),
        name="diff_branch",
    )(q, k, vt, lq1, lk1, lq2, lk2, g)


def _merge_kernel(x_ref, oat_ref, ob_ref, g_ref, wg_ref, gb_ref, wa_ref, wb_ref, wo_ref, y_ref):
    x = x_ref[...]
    h = _rmsnorm(x, g_ref[...], NORM_EPS).astype(BF16)
    gates = jax.nn.sigmoid(jnp.dot(h, wg_ref[...], preferred_element_type=F32) + gb_ref[...])
    pa = lax.dot_general(oat_ref[0], wa_ref[...], (((0,), (0,)), ((), ())),
                         preferred_element_type=F32)
    pb = jnp.dot(ob_ref[...], wb_ref[...], preferred_element_type=F32)
    merged = gates[:, :D_MODEL] * pa + gates[:, D_MODEL:] * pb
    y_ref[...] = x + jnp.dot(merged.astype(BF16), wo_ref[...], preferred_element_type=F32)


def _ffn_kernel(x_ref, g_ref, win_ref, wdown_ref, gf_ref, y_ref, *, final_norm):
    x = x_ref[...]
    h = _rmsnorm(x, g_ref[...], NORM_EPS).astype(BF16)
    gate = jnp.dot(h, win_ref[:, :D_FF], preferred_element_type=F32)
    up = jnp.dot(h, win_ref[:, D_FF:], preferred_element_type=F32)
    act = (gate * jax.nn.sigmoid(gate) * up).astype(BF16)
    y = x + jnp.dot(act, wdown_ref[...], preferred_element_type=F32)
    if final_norm:
        y = _rmsnorm(y, gf_ref[...], NORM_EPS)
    y_ref[...] = y


def _merge_ffn_kernel(x_ref, oat_ref, ob_ref, g_ref, wg_ref, gb_ref, wa_ref, wb_ref, wo_ref,
                      g2_ref, win_ref, wdown_ref, gf_ref, y_ref, x1_ref, *, final_norm):
    _merge_kernel(x_ref, oat_ref, ob_ref, g_ref, wg_ref, gb_ref, wa_ref, wb_ref, wo_ref, x1_ref)
    _ffn_kernel(x1_ref, g2_ref, win_ref, wdown_ref, gf_ref, y_ref, final_norm=final_norm)


def _merge_ffn(x2, oat, ob, g, wg, gb, wa, wb, wo, g2, win, wdown, gf, final_norm, tm=MERGE_FFN_TM):
    M = x2.shape[0]
    per_batch = oat.shape[2] // tm
    row = lambda w: pl.BlockSpec((tm, w), lambda i: (i, 0))
    full = lambda a: pl.BlockSpec(a.shape, lambda i: (0,) * a.ndim, pipeline_mode=pl.Buffered(1))
    col = pl.BlockSpec((1, oat.shape[1], tm), lambda i: (i // per_batch, 0, i % per_batch))
    weights = (g, wg, gb, wa, wb, wo, g2, win, wdown, gf)
    return pl.pallas_call(
        functools.partial(_merge_ffn_kernel, final_norm=final_norm),
        grid=(M // tm,),
        in_specs=[row(D_MODEL), col, row(512)] + [full(a) for a in weights],
        out_specs=row(D_MODEL),
        out_shape=jax.ShapeDtypeStruct((M, D_MODEL), F32),
        scratch_shapes=[pltpu.VMEM((tm, D_MODEL), F32)],
        compiler_params=_cparams(("parallel",)),
        name="merge_ffn",
    )(x2, oat, ob, *weights)


def _inv_freq(rot_dim):
    return jnp.power(jnp.float32(ROPE_THETA), -jnp.arange(0, rot_dim, 2, dtype=F32) / rot_dim)


def _inv_freq_column():
    inv = jnp.concatenate([_inv_freq(ROT_DIM_HEAD), _inv_freq(ROT_DIM_IDX)])
    return jnp.pad(inv, (0, TRIG_ROWS - inv.shape[0]))[:, None]


def _pad_lanes(v):
    return jnp.pad(v.astype(F32), (0, LANES - v.shape[0]))[None, :]


def kernel(x, positions, norm_mix_g, w_in, idx_k_norm_g, idx_k_norm_b, diff_lambda_q1,
           diff_lambda_k1, diff_lambda_q2, diff_lambda_k2, diff_subln_g, gate_b, w_branch_dsa,
           w_branch_diff, w_out, norm_ffn_g, w_ffn_in, w_ffn_out, norm_final_g):
    B, S, D = x.shape
    M = B * S
    depth = w_in.shape[0]
    if depth == 0:
        raise ValueError("depth must be positive")
    pos_t = positions.astype(F32)[:, None, :]
    inv_col = _inv_freq_column()
    paired = _paired_columns(2 * DSA_WIDTH)
    row = lambda v: v.astype(F32)[None, :]

    n_idx = IDX_QW + IDX_DIM + IDX_HEADS
    o_a = 0
    o_i = 3 * DSA_WIDTH
    o_b = o_i + n_idx
    o_g = o_b + 2 * DIFF_QK_WIDTH + DIFF_WIDTH

    x2 = x.reshape(M, D)
    for l in range(depth):
        lam_init = 0.8 - 0.6 * math.exp(-0.3 * l)
        wl = w_in[l]
        wa = wl[:, o_a:o_a + 2 * DSA_WIDTH][:, paired].astype(BF16)
        wavt = wl[:, o_a + 2 * DSA_WIDTH:o_a + 3 * DSA_WIDTH].T.astype(BF16)
        wi = jnp.pad(wl[:, o_i:o_i + n_idx], ((0, 0), (0, IDX_QW + LANES - n_idx))).astype(BF16)
        wb = wl[:, o_b:o_b + 2 * DIFF_QK_WIDTH][:, paired].astype(BF16)
        wbvt = wl[:, o_b + 2 * DIFF_QK_WIDTH:o_b + 2 * DIFF_QK_WIDTH + DIFF_WIDTH].T.astype(BF16)
        wg = wl[:, o_g:o_g + N_BRANCH * D_MODEL].astype(BF16)

        (q_a, k_a, vt_a, q_b, k_b, vt_b, q_i, k_i, w_it) = _in_projection(
            x2.reshape(B, S, D), pos_t, row(norm_mix_g[l]), wa, wavt, wb, wbvt, wi, inv_col,
            _pad_lanes(idx_k_norm_g[l]), _pad_lanes(idx_k_norm_b[l]))

        o_dsa = _dsa_branch(q_i, w_it, k_i, q_a, k_a, vt_a)
        o_diff = _diff_branch(q_b, k_b, vt_b,
                              row(diff_lambda_q1[l]), row(diff_lambda_k1[l]),
                              row(diff_lambda_q2[l]), row(diff_lambda_k2[l]),
                              row(diff_subln_g[l]), lam_init)

        x2 = _merge_ffn(x2, o_dsa, o_diff.reshape(M, DIFF_WIDTH),
                        row(norm_mix_g[l]), wg, row(gate_b[l]),
                        w_branch_dsa[l].astype(BF16), w_branch_diff[l].astype(BF16),
                        w_out[l].astype(BF16), row(norm_ffn_g[l]), w_ffn_in[l].astype(BF16),
                        w_ffn_out[l].astype(BF16), row(norm_final_g),
                        final_norm=(l == depth - 1))

    return x2.reshape(B, S, D)
```
